```python
import jax, jax.numpy as jnp
from jax import lax
import numpy as np

D_MODEL = 2048
BATCH = 8
SEQ = 2048
DEPTH = 1

D_FF = 5632
RWKV_HEADS = 16
RWKV_HEAD_DIM = 64
RWKV_WIDTH = RWKV_HEADS * RWKV_HEAD_DIM
DECAY_LORA = 64
ICLR_LORA = 64
GATE_LORA = 160
GN_EPS = 64e-5
NSA_HEADS = 16
NSA_KV_HEADS = 4
NSA_GROUP = NSA_HEADS // NSA_KV_HEADS
NSA_HEAD_DIM = 64
NSA_WIDTH = NSA_HEADS * NSA_HEAD_DIM
NSA_KV_WIDTH = NSA_KV_HEADS * NSA_HEAD_DIM
ROPE_DIM = NSA_HEAD_DIM // 4
ROPE_THETA = 500000.0
CMP_BLOCK = 32
CMP_STRIDE = 16
CMP_HIDDEN = 256
SEL_BLOCK = 64
SEL_TOP = 16
SEL_QBLOCK = 32
WINDOW = 512
WIN_QBLOCK = 128
NORM_EPS = 1e-6

RWKV_SPLITS = (RWKV_WIDTH, RWKV_WIDTH, RWKV_WIDTH, DECAY_LORA, ICLR_LORA, GATE_LORA)
NSA_SPLITS = (NSA_WIDTH,) + (NSA_KV_WIDTH,) * 6 + (3 * NSA_HEADS,)
MERGE_SPLITS = (D_MODEL, D_MODEL)
RWKV_COLS = sum(RWKV_SPLITS)
IN_COLS = RWKV_COLS + sum(NSA_SPLITS) + sum(MERGE_SPLITS)

kernel_name = "hybrid_rwkv7_nsa_macaron"


def _split(p, sizes):
    return jnp.split(p, np.cumsum(sizes)[:-1].tolist(), axis=-1)


def rms_norm(x, g, eps=NORM_EPS):
    xf = x.astype(jnp.float32)
    y = xf * lax.rsqrt(jnp.mean(xf * xf, -1, keepdims=True) + eps)
    return (y * g.astype(jnp.float32)).astype(x.dtype)


def swiglu(h, w_gate, w_up, w_down):
    return (jax.nn.silu(h @ w_gate) * (h @ w_up)) @ w_down


def token_shift(p):
    return jnp.pad(p, ((0, 0), (1, 0), (0, 0)))[:, :-1]


def partial_rope(x, pos):
    half = ROPE_DIM // 2
    inv = ROPE_THETA ** (-jnp.arange(half, dtype=jnp.float32) / half)
    ang = jnp.asarray(pos).astype(jnp.float32)[:, None] * inv[None, :]
    cos = jnp.cos(ang)[None, :, None, :]
    sin = jnp.sin(ang)[None, :, None, :]
    xr = x[..., :ROPE_DIM].astype(jnp.float32)
    x1, x2 = xr[..., :half], xr[..., half:]
    rot = jnp.concatenate([x1 * cos - x2 * sin, x2 * cos + x1 * sin], -1).astype(x.dtype)
    return jnp.concatenate([rot, x[..., ROPE_DIM:]], -1)


def masked_softmax(s, mask):
    s = jnp.where(mask, s, -jnp.inf)
    m = jnp.max(s, -1, keepdims=True)
    m = jnp.where(jnp.isfinite(m), m, 0.0)
    e = jnp.where(mask, jnp.exp(s - m), 0.0)
    return e / jnp.maximum(jnp.sum(e, -1, keepdims=True), 1e-30)


def rwkv7_mix(p_r, p_k, p_v, p_w, p_a, p_g, w0, w_up, a0, a_up, g_up,
              k_k, k_a, r_k, gn_w, gn_b):
    B, S, _ = p_r.shape
    H, N = RWKV_HEADS, RWKV_HEAD_DIM
    f32 = jnp.float32
    heads = lambda t: t.reshape(B, S, H, N)
    w = -jax.nn.softplus(-(w0 + jnp.tanh(p_w) @ w_up).astype(f32)) - 0.5
    decay = heads(jnp.exp(-jnp.exp(w)))
    a = heads(jax.nn.sigmoid((a0 + p_a @ a_up).astype(f32)))
    g = (jax.nn.sigmoid(p_g) @ g_up).astype(f32)
    r = heads(p_r).astype(f32)
    k = heads(p_k).astype(f32)
    v = heads(p_v).astype(f32)
    kk = k * k_k
    kk = kk * lax.rsqrt(jnp.maximum(jnp.sum(kk * kk, -1, keepdims=True), 1e-24))
    k = k * (1.0 + (a - 1.0) * k_a)
    tm = lambda t: jnp.moveaxis(t, 1, 0)

    def step(state, inp):
        r_t, w_t, k_t, v_t, kk_t, a_t = inp
        sa = jnp.einsum('bhvk,bhk->bhv', state, -kk_t)
        state = (state * w_t[:, :, None, :]
                 + sa[..., None] * (kk_t * a_t)[:, :, None, :]
                 + v_t[..., None] * k_t[:, :, None, :])
        return state, jnp.einsum('bhvk,bhk->bhv', state, r_t)

    s0 = jnp.zeros((B, H, N, N), f32)
    _, o = lax.scan(step, s0, (tm(r), tm(decay), tm(k), tm(v), tm(kk), tm(a)))
    o = jnp.moveaxis(o, 0, 1)
    mu = jnp.mean(o, -1, keepdims=True)
    var = jnp.mean(jnp.square(o - mu), -1, keepdims=True)
    o = (o - mu) * lax.rsqrt(var + GN_EPS) * gn_w + gn_b
    o = o + jnp.sum(r * k * r_k, -1, keepdims=True) * v
    return (o.reshape(B, S, H * N) * g).astype(p_r.dtype)


def nsa_mix(p_q, p_kc, p_vc, p_ks, p_vs, p_kw, p_vw, p_gate, q_norm, k_norm,
            cmp_pos_k, cmp_pos_v, cmp_k_w1, cmp_k_w2, cmp_v_w1, cmp_v_w2):
    B, S, _ = p_q.shape
    H, Hk, G, Dh = NSA_HEADS, NSA_KV_HEADS, NSA_GROUP, NSA_HEAD_DIM
    f32 = jnp.float32
    scale = Dh ** -0.5
    pos = np.arange(S)
    kv = lambda t: t.reshape(B, S, Hk, Dh)
    q = partial_rope(rms_norm(p_q.reshape(B, S, H, Dh), q_norm), pos).reshape(B, S, Hk, G, Dh)

    n_cmp = (S - CMP_BLOCK) // CMP_STRIDE + 1
    cmp_idx = np.arange(n_cmp)[:, None] * CMP_STRIDE + np.arange(CMP_BLOCK)[None, :]
    cmp_end = cmp_idx[:, -1]

    def compress(t, pos_emb, w1, w2):
        blk = kv(t)[:, cmp_idx] + pos_emb[None, None, :, None, :]
        blk = jnp.swapaxes(blk, 2, 3).reshape(B, n_cmp, Hk, CMP_BLOCK * Dh)
        return jax.nn.gelu(blk @ w1) @ w2

    k_cmp = partial_rope(rms_norm(compress(p_kc, cmp_pos_k, cmp_k_w1, cmp_k_w2), k_norm[0]), cmp_end)
    v_cmp = compress(p_vc, cmp_pos_v, cmp_v_w1, cmp_v_w2)
    s_cmp = jnp.einsum('bshgd,bchd->bhgsc', q, k_cmp).astype(f32) * scale
    p_cmp = masked_softmax(s_cmp, cmp_end[None, :] <= pos[:, None])
    o_cmp = jnp.einsum('bhgsc,bchd->bshgd', p_cmp.astype(v_cmp.dtype), v_cmp)

    n_sel = S // SEL_BLOCK
    n_top = min(SEL_TOP, n_sel)
    cs = np.arange(n_cmp)[:, None] * CMP_STRIDE
    ss = np.arange(n_sel)[None, :] * SEL_BLOCK
    overlap = np.clip(np.minimum(cs + CMP_BLOCK, ss + SEL_BLOCK) - np.maximum(cs, ss), 0, None) / CMP_BLOCK
    imp = jnp.einsum('bhgsc,cn->bhsn', p_cmp, jnp.asarray(overlap, f32))
    blk = np.arange(n_sel)[None, :]
    cur = (pos // SEL_BLOCK)[:, None]
    forced = (blk == 0) | (blk == cur) | (blk == cur - 1)
    imp = jnp.where(forced, jnp.inf, jnp.where(blk > cur, -jnp.inf, imp))
    _, sel_idx = lax.top_k(imp, n_top)

    ks = partial_rope(rms_norm(kv(p_ks), k_norm[1]), pos)
    kb = jnp.moveaxis(ks.reshape(B, n_sel, SEL_BLOCK, Hk, Dh), 3, 1)
    vb = jnp.moveaxis(kv(p_vs).reshape(B, n_sel, SEL_BLOCK, Hk, Dh), 3, 1)
    n_qc = S // SEL_QBLOCK
    q_c = jnp.moveaxis(q.reshape(B, n_qc, SEL_QBLOCK, Hk, G, Dh), 1, 0)
    idx_c = jnp.moveaxis(sel_idx.reshape(B, Hk, n_qc, SEL_QBLOCK, n_top), 2, 0)
    t_c = jnp.arange(S, dtype=jnp.int32).reshape(n_qc, SEL_QBLOCK)
    bi = jnp.arange(B)[:, None, None, None]
    hi = jnp.arange(Hk)[None, :, None, None]
    n_keys = n_top * SEL_BLOCK

    def sel_block(args):
        qb, ib, tb = args
        kg = kb[bi, hi, ib].reshape(B, Hk, SEL_QBLOCK, n_keys, Dh)
        vg = vb[bi, hi, ib].reshape(B, Hk, SEL_QBLOCK, n_keys, Dh)
        kpos = (ib[..., None] * SEL_BLOCK + jnp.arange(SEL_BLOCK)).reshape(B, Hk, SEL_QBLOCK, n_keys)
        s = jnp.einsum('bqhgd,bhqkd->bhgqk', qb, kg).astype(f32) * scale
        p = masked_softmax(s, (kpos <= tb[None, None, :, None])[:, :, None])
        return jnp.einsum('bhgqk,bhqkd->bqhgd', p.astype(vg.dtype), vg)

    o_slc = jnp.moveaxis(lax.map(sel_block, (q_c, idx_c, t_c)), 0, 1).reshape(B, S, Hk, G, Dh)

    kw = partial_rope(rms_norm(kv(p_kw), k_norm[2]), pos)
    n_wb = S // WIN_QBLOCK
    span = WINDOW + WIN_QBLOCK
    win_idx = np.arange(n_wb)[:, None] * WIN_QBLOCK + np.arange(span)[None, :]
    pad = ((0, 0), (WINDOW, 0), (0, 0), (0, 0))
    kwb = jnp.moveaxis(jnp.pad(kw, pad)[:, win_idx], 1, 0)
    vwb = jnp.moveaxis(jnp.pad(kv(p_vw), pad)[:, win_idx], 1, 0)
    q_w = jnp.moveaxis(q.reshape(B, n_wb, WIN_QBLOCK, Hk, G, Dh), 1, 0)
    kpos_w = jnp.asarray(win_idx - WINDOW, jnp.int32)
    t_w = jnp.arange(S, dtype=jnp.int32).reshape(n_wb, WIN_QBLOCK)

    def win_block(args):
        qb, kbw, vbw, kp, tb = args
        s = jnp.einsum('bqhgd,bkhd->bhgqk', qb, kbw).astype(f32) * scale
        kp_, tb_ = kp[None, :], tb[:, None]
        mask = (kp_ >= 0) & (kp_ <= tb_) & (kp_ > tb_ - WINDOW)
        p = masked_softmax(s, mask)
        return jnp.einsum('bhgqk,bkhd->bqhgd', p.astype(vbw.dtype), vbw)

    o_win = jnp.moveaxis(lax.map(win_block, (q_w, kwb, vwb, kpos_w, t_w)), 0, 1).reshape(B, S, Hk, G, Dh)

    gates = jax.nn.sigmoid(p_gate.astype(f32)).reshape(B, S, Hk, G, 3)
    o = (gates[..., 0:1] * o_cmp.astype(f32) + gates[..., 1:2] * o_slc.astype(f32)
         + gates[..., 2:3] * o_win.astype(f32))
    return o.reshape(B, S, NSA_WIDTH).astype(p_q.dtype)


def setup_inputs(seed: int = 0) -> dict:
    key = jax.random.key(seed)
    keys = jax.random.split(key, 48)
    counter = iter(range(48))
    L, D = DEPTH, D_MODEL
    H, N = RWKV_HEADS, RWKV_HEAD_DIM
    Dh = NSA_HEAD_DIM

    def normal(shape, scale):
        return jax.random.normal(keys[next(counter)], shape, jnp.float32) * scale

    def gain(shape):
        return 1.0 + normal(shape, 0.02)

    def unif(shape, lo, hi):
        return jax.random.uniform(keys[next(counter)], shape, jnp.float32, lo, hi)

    return {
        "x": normal((BATCH, SEQ, D), 1.0),
        "ffn1_norm": gain((L, D)),
        "ffn1_w_gate": normal((L, D, D_FF), D ** -0.5),
        "ffn1_w_up": normal((L, D, D_FF), D ** -0.5),
        "ffn1_w_down": normal((L, D_FF, D), D_FF ** -0.5),
        "mix_norm": gain((L, D)),
        "w_in": normal((L, D, IN_COLS), D ** -0.5),
        "rwkv_mix": unif((L, RWKV_COLS), 0.0, 1.0),
        "rwkv_w0": unif((L, RWKV_WIDTH), -6.0, -1.0),
        "rwkv_w_up": normal((L, DECAY_LORA, RWKV_WIDTH), 0.5 * DECAY_LORA ** -0.5),
        "rwkv_a0": normal((L, RWKV_WIDTH), 0.1),
        "rwkv_a_up": normal((L, ICLR_LORA, RWKV_WIDTH), 0.5 * ICLR_LORA ** -0.5),
        "rwkv_g_up": normal((L, GATE_LORA, RWKV_WIDTH), GATE_LORA ** -0.5),
        "rwkv_k_k": 0.85 + normal((L, H, N), 0.02),
        "rwkv_k_a": gain((L, H, N)),
        "rwkv_r_k": normal((L, H, N), 0.1),
        "rwkv_gn_w": gain((L, H, N)),
        "rwkv_gn_b": normal((L, H, N), 0.02),
        "nsa_q_norm": gain((L, Dh)),
        "nsa_k_norm": gain((L, 3, Dh)),
        "cmp_pos_k": normal((L, CMP_BLOCK, Dh), 0.02),
        "cmp_pos_v": normal((L, CMP_BLOCK, Dh), 0.02),
        "cmp_k_w1": normal((L, CMP_BLOCK * Dh, CMP_HIDDEN), (CMP_BLOCK * Dh) ** -0.5),
        "cmp_k_w2": normal((L, CMP_HIDDEN, Dh), CMP_HIDDEN ** -0.5),
        "cmp_v_w1": normal((L, CMP_BLOCK * Dh, CMP_HIDDEN), (CMP_BLOCK * Dh) ** -0.5),
        "cmp_v_w2": normal((L, CMP_HIDDEN, Dh), CMP_HIDDEN ** -0.5),
        "w_branch_rwkv": normal((L, RWKV_WIDTH, D), RWKV_WIDTH ** -0.5),
        "w_branch_nsa": normal((L, NSA_WIDTH, D), NSA_WIDTH ** -0.5),
        "w_out": normal((L, D, D), D ** -0.5),
        "ffn2_norm": gain((L, D)),
        "ffn2_w_gate": normal((L, D, D_FF), D ** -0.5),
        "ffn2_w_up": normal((L, D, D_FF), D ** -0.5),
        "ffn2_w_down": normal((L, D_FF, D), D_FF ** -0.5),
    }


def reference(x, ffn1_norm, ffn1_w_gate, ffn1_w_up, ffn1_w_down, mix_norm, w_in,
              rwkv_mix, rwkv_w0, rwkv_w_up, rwkv_a0, rwkv_a_up, rwkv_g_up,
              rwkv_k_k, rwkv_k_a, rwkv_r_k, rwkv_gn_w, rwkv_gn_b,
              nsa_q_norm, nsa_k_norm, cmp_pos_k, cmp_pos_v,
              cmp_k_w1, cmp_k_w2, cmp_v_w1, cmp_v_w2,
              w_branch_rwkv, w_branch_nsa, w_out,
              ffn2_norm, ffn2_w_gate, ffn2_w_up, ffn2_w_down):
    for l in range(DEPTH):
        x = x + 0.5 * swiglu(rms_norm(x, ffn1_norm[l]), ffn1_w_gate[l], ffn1_w_up[l], ffn1_w_down[l])

        h = rms_norm(x, mix_norm[l])
        proj = h @ w_in[l]
        p_rwkv = proj[..., :RWKV_COLS]
        p_rwkv = p_rwkv + rwkv_mix[l] * (token_shift(p_rwkv) - p_rwkv)
        p_r, p_k, p_v, p_w, p_a, p_g = _split(p_rwkv, RWKV_SPLITS)
        (p_q, p_kc, p_vc, p_ks, p_vs, p_kw, p_vw, p_gate,
         p_ga, p_gb) = _split(proj[..., RWKV_COLS:], NSA_SPLITS + MERGE_SPLITS)

        y_a = rwkv7_mix(p_r, p_k, p_v, p_w, p_a, p_g, rwkv_w0[l], rwkv_w_up[l], rwkv_a0[l],
                        rwkv_a_up[l], rwkv_g_up[l], rwkv_k_k[l], rwkv_k_a[l], rwkv_r_k[l],
                        rwkv_gn_w[l], rwkv_gn_b[l])
        y_b = nsa_mix(p_q, p_kc, p_vc, p_ks, p_vs, p_kw, p_vw, p_gate, nsa_q_norm[l], nsa_k_norm[l],
                      cmp_pos_k[l], cmp_pos_v[l], cmp_k_w1[l], cmp_k_w2[l], cmp_v_w1[l], cmp_v_w2[l])

        merged = (jax.nn.sigmoid(p_ga) * (y_a @ w_branch_rwkv[l])
                  + jax.nn.sigmoid(p_gb) * (y_b @ w_branch_nsa[l]))
        x = x + merged @ w_out[l]

        x = x + 0.5 * swiglu(rms_norm(x, ffn2_norm[l]), ffn2_w_gate[l], ffn2_w_up[l], ffn2_w_down[l])
    return x
```

```python
import functools

import numpy as np
import jax
import jax.numpy as jnp
from jax import lax
from jax.experimental import pallas as pl
from jax.experimental.pallas import tpu as pltpu

F32 = jnp.float32
BF16 = jnp.bfloat16

RWKV_HEAD_DIM = 64
DECAY_LORA = 64
ICLR_LORA = 64
GATE_LORA = 160
GN_EPS = 64e-5
NSA_HEADS = 16
NSA_KV_HEADS = 4
NSA_GROUP = NSA_HEADS // NSA_KV_HEADS
NSA_HEAD_DIM = 64
ROPE_DIM = NSA_HEAD_DIM // 4
ROPE_HALF = ROPE_DIM // 2
ROPE_THETA = 500000.0
CMP_BLOCK = 32
CMP_STRIDE = 16
SEL_BLOCK = 64
SEL_TOP = 16
WINDOW = 512
NORM_EPS = 1e-6

LANE = 128
CHUNK = 64
VMEM_LIMIT = 56 * 1024 * 1024
MASKED = -1e30


def _cparams(sem):
    return pltpu.CompilerParams(dimension_semantics=sem, vmem_limit_bytes=VMEM_LIMIT)


def _const_spec(shape):
    nd = len(shape)
    return pl.BlockSpec(shape, lambda *_: (0,) * nd, pipeline_mode=pl.Buffered(1))


def _mm(a, b):
    return lax.dot_general(a, b, (((1,), (0,)), ((), ())), preferred_element_type=F32)


def _mm_nt(a, b):
    return lax.dot_general(a, b, (((1,), (1,)), ((), ())), preferred_element_type=F32)


def _mm_tn(a, b):
    return lax.dot_general(a, b, (((0,), (0,)), ((), ())), preferred_element_type=F32)


def _split2(x):
    hi = x.astype(BF16)
    lo = (x - hi.astype(F32)).astype(BF16)
    return hi, lo


def _split3(x):
    h1 = x.astype(BF16)
    r1 = x - h1.astype(F32)
    h2 = r1.astype(BF16)
    h3 = (r1 - h2.astype(F32)).astype(BF16)
    return h1, h2, h3


def _dot3(a, b, mm=_mm):
    a1, a2 = _split2(a)
    b1, b2 = _split2(b)
    return mm(a1, b1) + (mm(a1, b2) + mm(a2, b1))


def _dot_exact_rhs(a, b_bf16):
    a1, a2, a3 = _split3(a)
    return _mm(a1, b_bf16) + (_mm(a2, b_bf16) + _mm(a3, b_bf16))


def _dot_exact_lhs(a_bf16, b):
    b1, b2, b3 = _split3(b)
    return _mm(a_bf16, b1) + (_mm(a_bf16, b2) + _mm(a_bf16, b3))


def _sigmoid(x):
    return 1.0 / (1.0 + jnp.exp(-x))


def _iota(shape, dim):
    return lax.broadcasted_iota(jnp.int32, shape, dim)


def _ffn_kernel(x_ref, g_ref, wg_ref, wu_ref, wd_ref, o_ref, h_ref, acc_ref):
    j = pl.program_id(1)

    @pl.when(j == 0)
    def _():
        x = x_ref[...]
        ms = jnp.mean(x * x, axis=-1, keepdims=True)
        h_ref[...] = (x * lax.rsqrt(ms + NORM_EPS) * g_ref[...]).astype(BF16)
        acc_ref[...] = jnp.zeros_like(acc_ref)

    h = h_ref[...]
    gate = _mm(h, wg_ref[...])
    up = _mm(h, wu_ref[...])
    act = (gate * _sigmoid(gate) * up).astype(BF16)
    acc_ref[...] += _mm(act, wd_ref[...])

    @pl.when(j == pl.num_programs(1) - 1)
    def _():
        o_ref[...] = x_ref[...] + 0.5 * acc_ref[...]


def _ffn(x, g, wg, wu, wd, *, tm=512, tf=512):
    t, d = x.shape
    f = wg.shape[1]
    return pl.pallas_call(
        _ffn_kernel,
        out_shape=jax.ShapeDtypeStruct((t, d), F32),
        grid=(t // tm, f // tf),
        in_specs=[
            pl.BlockSpec((tm, d), lambda i, j: (i, 0)),
            pl.BlockSpec((1, d), lambda i, j: (0, 0)),
            pl.BlockSpec((d, tf), lambda i, j: (0, j)),
            pl.BlockSpec((d, tf), lambda i, j: (0, j)),
            pl.BlockSpec((tf, d), lambda i, j: (j, 0)),
        ],
        out_specs=pl.BlockSpec((tm, d), lambda i, j: (i, 0)),
        scratch_shapes=[pltpu.VMEM((tm, d), BF16), pltpu.VMEM((tm, d), F32)],
        compiler_params=_cparams(("parallel", "arbitrary")),
        name="ffn",
    )(x, g, wg, wu, wd)


def _norm_proj_kernel(x_ref, g_ref, w_ref, o_ref):
    x = x_ref[...]
    ms = jnp.mean(x * x, axis=-1, keepdims=True)
    h = (x * lax.rsqrt(ms + NORM_EPS) * g_ref[...]).astype(BF16)
    o_ref[...] = _mm(h, w_ref[...]).astype(o_ref.dtype)


def _norm_proj(x, g, w, *, tm=256, name="norm_proj"):
    t, d = x.shape
    n = w.shape[1]
    return pl.pallas_call(
        _norm_proj_kernel,
        out_shape=jax.ShapeDtypeStruct((t, n), F32),
        grid=(t // tm,),
        in_specs=[
            pl.BlockSpec((tm, d), lambda i: (i, 0)),
            _const_spec((1, d)),
            _const_spec((d, n)),
        ],
        out_specs=pl.BlockSpec((tm, n), lambda i: (i, 0)),
        compiler_params=_cparams(("parallel",)),
        name=name,
    )(x, g, w)


def _rwkv_prep_kernel(p_ref, prev_ref, mix_ref, wwa_ref, gup_ref, w0_ref, a0_ref, kk_ref, ka_ref,
                      rk_ref, seg_ref, tri_ref,
                      rt_ref, at_ref, kt_ref, bt_ref, kh_ref, bh_ref, v_ref, dc_ref, bonus_ref,
                      gate_ref, *, seq_tiles):
    i = pl.program_id(0)
    ts = p_ref.shape[0]
    w = rt_ref.shape[1]
    p = p_ref[...]
    prev = prev_ref[7:8, :]
    prev = jnp.where(i % seq_tiles == 0, jnp.zeros_like(prev), prev)
    shifted = pltpu.roll(p, 1, axis=0)
    shifted = jnp.where(_iota(p.shape, 0) == 0, prev, shifted)
    xs = p + mix_ref[...] * (shifted - p)

    r = xs[:, 0:w]
    k = xs[:, w:2 * w]
    v = xs[:, 2 * w:3 * w]
    lo = 3 * w
    pwa = xs[:, lo:lo + LANE]
    pg = xs[:, lo + LANE:lo + 3 * LANE]
    lane = _iota(pwa.shape, 1)
    z = jnp.where(lane < DECAY_LORA, jnp.tanh(pwa), pwa)
    wa = _dot3(z, wwa_ref[...])
    wl = w0_ref[...] + wa[:, :w]
    neg = -wl
    softplus = jnp.maximum(neg, 0.0) + jnp.log(1.0 + jnp.exp(-jnp.abs(neg)))
    lw = -jnp.exp(-softplus - 0.5)
    a = _sigmoid(a0_ref[...] + wa[:, w:])
    gate_ref[...] = _dot3(_sigmoid(pg), gup_ref[...])

    seg = seg_ref[...]
    kk = k * kk_ref[...]
    ss = _dot_exact_rhs(kk * kk, seg)
    kk = kk * lax.rsqrt(jnp.maximum(ss, 1e-24))
    k2 = k * (1.0 + (a - 1.0) * ka_ref[...])
    bonus_ref[...] = _dot_exact_rhs(r * k2 * rk_ref[...], seg) * v

    gc = _dot_exact_lhs(tri_ref[...], lw)
    nc = ts // CHUNK
    ends = [gc[(q + 1) * CHUNK - 1:(q + 1) * CHUNK, :] for q in range(nc)]
    gend = jnp.concatenate([jnp.broadcast_to(e, (CHUNK, w)) for e in ends], axis=0)
    to_end = jnp.exp(gend - gc)
    e_in = jnp.exp(gc)
    e_out = jnp.exp(-gc)
    b = kk * a
    rt_ref[...] = r * e_in
    at_ref[...] = -kk * jnp.exp(gc - lw)
    kt_ref[...] = k2 * e_out
    bt_ref[...] = b * e_out
    kh_ref[...] = k2 * to_end
    bh_ref[...] = b * to_end
    v_ref[...] = v
    dc_ref[0] = jnp.concatenate([jnp.exp(e) for e in ends] + [jnp.zeros((8 - nc, w), F32)], axis=0)


def _rwkv_prep(p_rwkv, mix, wwa, gup, w0, a0, k_k, k_a, r_k, seg, tri, *, seq, ts=256):
    t, pc = p_rwkv.shape
    w = w0.shape[1]
    nt = t // ts
    row = lambda i: (i, 0)
    tok = pl.BlockSpec((ts, w), row)
    tok_shape = jax.ShapeDtypeStruct((t, w), F32)
    return pl.pallas_call(
        functools.partial(_rwkv_prep_kernel, seq_tiles=seq // ts),
        out_shape=[tok_shape] * 7 + [jax.ShapeDtypeStruct((nt, 8, w), F32), tok_shape, tok_shape],
        grid=(nt,),
        in_specs=[
            pl.BlockSpec((ts, pc), row),
            pl.BlockSpec((8, pc), lambda i: (jnp.maximum(i * (ts // 8) - 1, 0), 0)),
            _const_spec((1, pc)),
            _const_spec(wwa.shape),
            _const_spec(gup.shape),
            _const_spec((1, w)), _const_spec((1, w)), _const_spec((1, w)), _const_spec((1, w)),
            _const_spec((1, w)),
            _const_spec(seg.shape),
            _const_spec(tri.shape),
        ],
        out_specs=[tok] * 7 + [pl.BlockSpec((1, 8, w), lambda i: (i, 0, 0)), tok, tok],
        compiler_params=_cparams(("parallel",)),
        name="rwkv_prep",
    )(p_rwkv, p_rwkv, mix, wwa, gup, w0, a0, k_k, k_a, r_k, seg, tri)


def _unit_lower_inverse(a, row, col):
    eye = (row == col).astype(F32)
    same8 = (row // 8) == (col // 8)
    a8 = jnp.where(same8, a, 0.0)
    a8_2 = _dot3(a8, a8)
    a8_4 = _dot3(a8_2, a8_2)
    p = eye + a8 + a8_2 + _dot3(a8, a8_2)
    t = p + _dot3(p, a8_4)
    m = 16
    while m <= CHUNK:
        off = jnp.where(((row // m) == (col // m)) & ((row // (m // 2)) != (col // (m // 2))), a, 0.0)
        t = t + _dot3(_dot3(t, off), t)
        m *= 2
    return t


def _rwkv_chunk_kernel(rt_ref, at_ref, kt_ref, bt_ref, kh_ref, bh_ref, v_ref, dc_ref, o_ref, state_ref,
                       *, heads, chunks_per_tile):
    n = RWKV_HEAD_DIM
    c = pl.program_id(1)

    @pl.when(c == 0)
    def _():
        state_ref[...] = jnp.zeros_like(state_ref)

    row = _iota((CHUNK, CHUNK), 0)
    col = _iota((CHUNK, CHUNK), 1)
    strict = col < row
    row2 = _iota((CHUNK, 2 * CHUNK), 0)
    col2 = _iota((CHUNK, 2 * CHUNK), 1)
    incl2 = (col2 % CHUNK) <= row2
    dc_all = dc_ref[0, 0]
    dc_row = dc_all[0:1, :]
    for q in range(1, chunks_per_tile):
        dc_row = jnp.where(c % chunks_per_tile == q, dc_all[q:q + 1, :], dc_row)

    for h in range(heads):
        sl = slice(h * n, (h + 1) * n)
        at = at_ref[0, :, sl]
        rt = rt_ref[0, :, sl]
        v = v_ref[0, :, sl]
        x2 = jnp.concatenate([bt_ref[0, :, sl], kt_ref[0, :, sl]], axis=0)
        m1 = _dot3(at, x2, _mm_nt)
        m2 = _dot3(rt, x2, _mm_nt)
        h0 = state_ref[h]
        ah = _dot3(jnp.concatenate([at, rt], axis=0), h0)
        a_ak = jnp.where(strict, m1[:, CHUNK:], 0.0)
        wmat = ah[:CHUNK] + _dot3(a_ak, v)
        tinv = _unit_lower_inverse(jnp.where(strict, m1[:, :CHUNK], 0.0), row, col)
        u = _dot3(tinv, wmat)
        uv = jnp.concatenate([u, v], axis=0)
        o_ref[0, :, sl] = ah[CHUNK:] + _dot3(jnp.where(incl2, m2, 0.0), uv)
        dmat = jnp.where(row == col, jnp.broadcast_to(dc_row[:, sl], (n, n)), 0.0)
        bk = jnp.concatenate([bh_ref[0, :, sl], kh_ref[0, :, sl]], axis=0)
        state_ref[h] = _dot3(dmat, h0) + _dot3(bk, uv, _mm_tn)


def _rwkv_chunk(rt, at, kt, bt, kh, bh, v, dc, *, batch, seq, prep_ts):
    t, w = rt.shape
    heads = w // RWKV_HEAD_DIM
    nchunk = seq // CHUNK
    cpt = prep_ts // CHUNK
    r3 = lambda x: x.reshape(batch, seq, w)
    tok = pl.BlockSpec((1, CHUNK, w), lambda b, c: (b, c, 0))
    dc4 = dc.reshape(batch, seq // prep_ts, 8, w)
    return pl.pallas_call(
        functools.partial(_rwkv_chunk_kernel, heads=heads, chunks_per_tile=cpt),
        out_shape=jax.ShapeDtypeStruct((batch, seq, w), F32),
        grid=(batch, nchunk),
        in_specs=[tok] * 7 + [pl.BlockSpec((1, 1, 8, w), lambda b, c: (b, c // cpt, 0, 0))],
        out_specs=tok,
        scratch_shapes=[pltpu.VMEM((heads, RWKV_HEAD_DIM, RWKV_HEAD_DIM), F32)],
        compiler_params=_cparams(("parallel", "arbitrary")),
        name="rwkv_chunk",
    )(r3(rt), r3(at), r3(kt), r3(bt), r3(kh), r3(bh), r3(v), dc4).reshape(t, w)


def _rope_lanes(x, cos_t, sin_a, sin_b):
    width = x.shape[1]
    up = pltpu.roll(x, width - ROPE_HALF, axis=1)
    dn = pltpu.roll(x, ROPE_HALF, axis=1)
    return x * cos_t + up * sin_a + dn * sin_b


def _tile_lanes(tab, width):
    return jnp.concatenate([tab] * (width // tab.shape[1]), axis=1)


def _nsa_prep_kernel(p_ref, cos_ref, sa_ref, sb_ref, qn_ref, kn_ref, seg_ref,
                     q_ref, ks_ref, vs_ref, kw_ref, vw_ref, g_ref):
    dh = NSA_HEAD_DIM
    qw = q_ref.shape[1]
    kvw = NSA_KV_HEADS * dh
    seg = seg_ref[...]
    cos_t, sin_a, sin_b = cos_ref[...], sa_ref[...], sb_ref[...]

    def norm_rope(x, gain):
        wd = x.shape[1]
        ms = _dot_exact_rhs(x * x, seg[:wd, :wd]) * (1.0 / dh)
        y = x * lax.rsqrt(ms + NORM_EPS) * gain
        return _rope_lanes(y, _tile_lanes(cos_t, wd), _tile_lanes(sin_a, wd), _tile_lanes(sin_b, wd))

    q = norm_rope(p_ref[:, 0:qw], qn_ref[...])
    q_ref[...] = q * (dh ** -0.5)
    base = qw + 2 * kvw
    ks = norm_rope(p_ref[:, base:base + kvw], kn_ref[1:2, :]).astype(BF16)
    vs = p_ref[:, base + kvw:base + 2 * kvw].astype(BF16)
    kw = norm_rope(p_ref[:, base + 2 * kvw:base + 3 * kvw], kn_ref[2:3, :]).astype(BF16)
    vw = p_ref[:, base + 3 * kvw:base + 4 * kvw].astype(BF16)
    for h in range(NSA_KV_HEADS):
        sl = slice(h * dh, (h + 1) * dh)
        ks_ref[0, h] = ks[:, sl]
        vs_ref[0, h] = vs[:, sl]
        kw_ref[0, h] = kw[:, sl]
        vw_ref[0, h] = vw[:, sl]
    g_ref[...] = _sigmoid(p_ref[:, base + 4 * kvw:base + 4 * kvw + LANE])


def _nsa_prep(p_nsa, cos_t, sin_a, sin_b, qn, kn, seg, *, batch, seq, ts=256):
    t, pc = p_nsa.shape
    qw = NSA_HEADS * NSA_HEAD_DIM
    st = seq // ts
    tab = pl.BlockSpec((ts, LANE), lambda i: (i % st, 0))
    hm = pl.BlockSpec((1, NSA_KV_HEADS, ts, NSA_HEAD_DIM), lambda i: (i // st, 0, i % st, 0))
    hm_shape = jax.ShapeDtypeStruct((batch, NSA_KV_HEADS, seq, NSA_HEAD_DIM), BF16)
    return pl.pallas_call(
        _nsa_prep_kernel,
        out_shape=[jax.ShapeDtypeStruct((t, qw), F32)] + [hm_shape] * 4
        + [jax.ShapeDtypeStruct((t, LANE), F32)],
        grid=(t // ts,),
        in_specs=[pl.BlockSpec((ts, pc), lambda i: (i, 0)), tab, tab, tab,
                  _const_spec(qn.shape), _const_spec(kn.shape), _const_spec(seg.shape)],
        out_specs=[pl.BlockSpec((ts, qw), lambda i: (i, 0))] + [hm] * 4
        + [pl.BlockSpec((ts, LANE), lambda i: (i, 0))],
        compiler_params=_cparams(("parallel",)),
        name="nsa_prep",
    )(p_nsa, cos_t, sin_a, sin_b, qn, kn, seg)


def _gelu_tanh(x):
    return 0.5 * x * (1.0 + jnp.tanh(np.sqrt(2.0 / np.pi).astype(np.float32) * (x + 0.044715 * (x * x * x))))


def _compress_kernel(gk_ref, gv_ref, pk_ref, pv_ref, k1_ref, k2_ref, v1_ref, v2_ref, kn_ref,
                     cos_ref, sin_ref, rot_ref, kc_ref, vc_ref):
    half = k1_ref.shape[0] // 2

    def mlp(g, pos, w1_ref, w2_ref):
        ya = _dot3(g, w1_ref[0:half, :])
        yb = _dot3(g, w1_ref[half:, :])
        bias = _dot3(jnp.broadcast_to(pos, (8, pos.shape[1])), w1_ref[...])[0:1, :]
        n = g.shape[0]
        hid = ya + pltpu.roll(yb, n - 1, axis=0) + bias
        return _dot3(_gelu_tanh(hid), w2_ref[...])

    kc = mlp(gk_ref[0, 0], pk_ref[...], k1_ref, k2_ref)
    ms = jnp.mean(kc * kc, axis=-1, keepdims=True)
    kc = kc * lax.rsqrt(ms + NORM_EPS) * kn_ref[0:1, :]
    kc_ref[0, 0] = kc * cos_ref[...] + _dot_exact_rhs(kc, rot_ref[...]) * sin_ref[...]
    vc_ref[0, 0] = mlp(gv_ref[0, 0], pv_ref[...], v1_ref, v2_ref)


def _nsa_compress(gk, gv, pk, pv, k1, k2, v1, v2, kn, cos_c, sin_c, rot):
    b, hk, ng, gw = gk.shape
    dh = NSA_HEAD_DIM
    grp = pl.BlockSpec((1, 1, ng, gw), lambda i, j: (i, j, 0, 0))
    out = pl.BlockSpec((1, 1, ng, dh), lambda i, j: (i, j, 0, 0))
    shape = jax.ShapeDtypeStruct((b, hk, ng, dh), F32)
    consts = [pk, pv, k1, k2, v1, v2, kn, cos_c, sin_c, rot]
    return pl.pallas_call(
        _compress_kernel,
        out_shape=[shape, shape],
        grid=(b, hk),
        in_specs=[grp, grp] + [_const_spec(c.shape) for c in consts],
        out_specs=[out, out],
        compiler_params=_cparams(("parallel", "parallel")),
        name="nsa_compress",
    )(gk, gv, *consts)


def _softmax_rows(s, mask):
    s = jnp.where(mask, s, MASKED)
    m = jnp.max(s, axis=-1, keepdims=True)
    e = jnp.where(mask, jnp.exp(s - m), 0.0)
    return e / jnp.maximum(jnp.sum(e, axis=-1, keepdims=True), 1e-30)


def _nsa_attn_kernel(q_ref, kc_ref, vc_ref, ks_ref, vs_ref, kw_ref, vw_ref, g_ref, ov_ref, ex_ref,
                     o_ref, *, seq):
    dh = NSA_HEAD_DIM
    grp = NSA_GROUP
    hk = pl.program_id(1)
    qi = pl.program_id(2)
    tq = q_ref.shape[0]
    t0 = qi * tq
    ncmp = kc_ref.shape[2]
    nsel = ov_ref.shape[1]
    span = WINDOW + tq

    q = q_ref[...]
    qs = [q[:, g * dh:(g + 1) * dh] for g in range(grp)]
    gates = g_ref[...]

    kc = kc_ref[0, 0]
    vc = vc_ref[0, 0].astype(BF16)
    tpos = t0 + _iota((tq, ncmp), 0)
    cend = _iota((tq, ncmp), 1) * CMP_STRIDE + (CMP_BLOCK - 1)
    cmask = cend <= tpos
    psum = jnp.zeros((tq, ncmp), F32)
    o_cmp = []
    for g in range(grp):
        p = _softmax_rows(_dot3(qs[g], kc, _mm_nt), cmask)
        psum = psum + p
        o_cmp.append(_mm(p.astype(BF16), vc))
    imp = _dot_exact_rhs(psum, ov_ref[...])
    blk = _iota((tq, nsel), 1)
    cur = (t0 + _iota((tq, nsel), 0)) // SEL_BLOCK
    forced = (blk == 0) | (blk == cur) | (blk == cur - 1)
    imp = jnp.where(forced, jnp.inf, jnp.where(blk > cur, -jnp.inf, imp))
    rank = jnp.zeros((tq, nsel), jnp.int32)
    for m in range(nsel):
        im = imp[:, m:m + 1]
        ahead = (im > imp) | ((im == imp) & (m < blk))
        rank = rank + ahead.astype(jnp.int32)
    sel = jnp.where(rank < min(SEL_TOP, nsel), 1.0, 0.0).astype(BF16)
    smask = (_mm(sel, ex_ref[...]) > 0.5) & (_iota((tq, seq), 1) <= t0 + _iota((tq, seq), 0))

    ks = ks_ref[0, 0]
    vs = vs_ref[0, 0]
    w0 = pl.multiple_of(jnp.maximum(t0 - WINDOW, 0), tq)
    kw = kw_ref[0, 0, pl.ds(w0, span), :]
    vw = vw_ref[0, 0, pl.ds(w0, span), :]
    kpos = w0 + _iota((tq, span), 1)
    tw = t0 + _iota((tq, span), 0)
    wmask = (kpos <= tw) & (kpos > tw - WINDOW)
    outs = []
    for g in range(grp):
        qb = qs[g].astype(BF16)
        o_slc = _mm(_softmax_rows(_mm_nt(qb, ks), smask).astype(BF16), vs)
        o_win = _mm(_softmax_rows(_mm_nt(qb, kw), wmask).astype(BF16), vw)
        gi = 3 * g
        gcol = [jnp.zeros((tq, 1), F32)] * 3
        for j in range(NSA_KV_HEADS):
            for br in range(3):
                cidx = j * grp * 3 + gi + br
                gcol[br] = jnp.where(hk == j, gates[:, cidx:cidx + 1], gcol[br])
        outs.append(gcol[0] * o_cmp[g] + gcol[1] * o_slc + gcol[2] * o_win)
    o_ref[...] = jnp.concatenate(outs, axis=1)


def _nsa_attn(q, kc, vc, ks, vs, kw, vw, gates, overlap, expand, *, batch, seq, tq=128):
    t, qw = q.shape
    dh = NSA_HEAD_DIM
    gw = NSA_GROUP * dh
    st = seq // tq
    ncmp = kc.shape[2]
    cmp_spec = pl.BlockSpec((1, 1, ncmp, dh), lambda b, h, i: (b, h, 0, 0))
    kv_spec = pl.BlockSpec((1, 1, seq, dh), lambda b, h, i: (b, h, 0, 0))
    return pl.pallas_call(
        functools.partial(_nsa_attn_kernel, seq=seq),
        out_shape=jax.ShapeDtypeStruct((t, qw), F32),
        grid=(batch, NSA_KV_HEADS, st),
        in_specs=[
            pl.BlockSpec((tq, gw), lambda b, h, i: (b * st + i, h)),
            cmp_spec, cmp_spec, kv_spec, kv_spec, kv_spec, kv_spec,
            pl.BlockSpec((tq, LANE), lambda b, h, i: (b * st + i, 0)),
            _const_spec(overlap.shape), _const_spec(expand.shape),
        ],
        out_specs=pl.BlockSpec((tq, gw), lambda b, h, i: (b * st + i, h)),
        compiler_params=_cparams(("parallel", "parallel", "arbitrary")),
        name="nsa_attn",
    )(q, kc, vc, ks, vs, kw, vw, gates, overlap, expand)


def _merge_kernel(x_ref, o_ref, bonus_ref, gate_ref, yb_ref, pg_ref, gnw_ref, gnb_ref, seg_ref,
                  ua_ref, ub_ref, wo_ref, out_ref):
    d = x_ref.shape[1]
    n = RWKV_HEAD_DIM
    seg = seg_ref[...]
    o = o_ref[...]
    mu = _dot_exact_rhs(o, seg) * (1.0 / n)
    dlt = o - mu
    var = _dot_exact_rhs(dlt * dlt, seg) * (1.0 / n)
    on = dlt * lax.rsqrt(var + GN_EPS) * gnw_ref[...] + gnb_ref[...]
    ya = ((on + bonus_ref[...]) * gate_ref[...]).astype(BF16)
    yb = yb_ref[...].astype(BF16)
    merged = (_sigmoid(pg_ref[:, 0:d]) * _mm(ya, ua_ref[...])
              + _sigmoid(pg_ref[:, d:2 * d]) * _mm(yb, ub_ref[...]))
    out_ref[...] = x_ref[...] + _mm(merged.astype(BF16), wo_ref[...])


def _merge(x, o_rwkv, bonus, gate, yb, pg, gnw, gnb, seg, ua, ub, wo, *, tm=256):
    t, d = x.shape
    w = o_rwkv.shape[1]
    row = lambda i: (i, 0)
    tokw = pl.BlockSpec((tm, w), row)
    return pl.pallas_call(
        _merge_kernel,
        out_shape=jax.ShapeDtypeStruct((t, d), F32),
        grid=(t // tm,),
        in_specs=[pl.BlockSpec((tm, d), row), tokw, tokw, tokw, tokw,
                  pl.BlockSpec((tm, 2 * d), row),
                  _const_spec((1, w)), _const_spec((1, w)), _const_spec(seg.shape),
                  _const_spec(ua.shape), _const_spec(ub.shape), _const_spec(wo.shape)],
        out_specs=pl.BlockSpec((tm, d), row),
        compiler_params=_cparams(("parallel",)),
        name="merge",
    )(x, o_rwkv, bonus, gate, yb, pg, gnw, gnb, seg, ua, ub, wo)


def _block_diag_ones(width, block):
    idx = np.arange(width) // block
    return jnp.asarray(idx[:, None] == idx[None, :], BF16)


def _chunk_lower_ones(ts):
    i = np.arange(ts)
    return jnp.asarray((i[:, None] // CHUNK == i[None, :] // CHUNK) & (i[None, :] <= i[:, None]), BF16)


def _rope_tables(pos):
    inv = ROPE_THETA ** (-jnp.arange(ROPE_HALF, dtype=F32) / ROPE_HALF)
    ang = jnp.asarray(pos).astype(F32)[:, None] * inv[None, :]
    cos, sin = jnp.cos(ang), jnp.sin(ang)
    n = ang.shape[0]
    pad = jnp.zeros((n, NSA_HEAD_DIM - ROPE_DIM), F32)
    zero = jnp.zeros_like(sin)
    cos_h = jnp.concatenate([cos, cos, pad + 1.0], axis=1)
    sa_h = jnp.concatenate([-sin, zero, pad], axis=1)
    sb_h = jnp.concatenate([zero, sin, pad], axis=1)
    return cos_h, sa_h, sb_h


def _rot_half_matrix():
    r = np.zeros((NSA_HEAD_DIM, NSA_HEAD_DIM), np.float32)
    for l in range(ROPE_HALF):
        r[l + ROPE_HALF, l] = -1.0
        r[l, l + ROPE_HALF] = 1.0
    return jnp.asarray(r, BF16)


def _overlap_matrix(ncmp_pad, nsel):
    cs = np.arange(ncmp_pad)[:, None] * CMP_STRIDE
    ss = np.arange(nsel)[None, :] * SEL_BLOCK
    ov = np.clip(np.minimum(cs + CMP_BLOCK, ss + SEL_BLOCK) - np.maximum(cs, ss), 0, None) / CMP_BLOCK
    return jnp.asarray(ov, BF16)


def _expand_matrix(nsel, seq):
    return jnp.asarray(np.arange(nsel)[:, None] == (np.arange(seq)[None, :] // SEL_BLOCK), BF16)


def _pad_cols(x, width):
    return jnp.pad(x, ((0, 0), (0, width - x.shape[1])))


def _layer(x, l, ffn1_norm, ffn1_w_gate, ffn1_w_up, ffn1_w_down, mix_norm, w_in,
           rwkv_mix, rwkv_w0, rwkv_w_up, rwkv_a0, rwkv_a_up, rwkv_g_up,
           rwkv_k_k, rwkv_k_a, rwkv_r_k, rwkv_gn_w, rwkv_gn_b,
           nsa_q_norm, nsa_k_norm, cmp_pos_k, cmp_pos_v,
           cmp_k_w1, cmp_k_w2, cmp_v_w1, cmp_v_w2,
           w_branch_rwkv, w_branch_nsa, w_out,
           ffn2_norm, ffn2_w_gate, ffn2_w_up, ffn2_w_down, *, batch, seq):
    t, d = x.shape
    w = rwkv_w0.shape[1]
    dh = NSA_HEAD_DIM
    qw = NSA_HEADS * dh
    kvw = NSA_KV_HEADS * dh
    prep_ts = 256
    row = lambda v: v.reshape(1, -1)

    x = _ffn(x, row(ffn1_norm[l]), ffn1_w_gate[l].astype(BF16), ffn1_w_up[l].astype(BF16),
             ffn1_w_down[l].astype(BF16))

    wi = w_in[l]
    rwkv_cols = 3 * w + DECAY_LORA + ICLR_LORA + GATE_LORA
    rwkv_pad = 3 * w + 3 * LANE
    nsa_cols = qw + 6 * kvw + 3 * NSA_HEADS
    nsa_pad = qw + 6 * kvw + LANE
    g_mix = row(mix_norm[l])
    p_rwkv = _norm_proj(x, g_mix, _pad_cols(wi[:, :rwkv_cols], rwkv_pad).astype(BF16), name="proj_rwkv")
    p_nsa = _norm_proj(x, g_mix, _pad_cols(wi[:, rwkv_cols:rwkv_cols + nsa_cols], nsa_pad).astype(BF16),
                       name="proj_nsa")
    p_gate = _norm_proj(x, g_mix, wi[:, rwkv_cols + nsa_cols:].astype(BF16), name="proj_gate")

    wwa = jnp.zeros((LANE, 2 * w), F32)
    wwa = wwa.at[:DECAY_LORA, :w].set(rwkv_w_up[l]).at[DECAY_LORA:, w:].set(rwkv_a_up[l])
    gup = jnp.pad(rwkv_g_up[l], ((0, 2 * LANE - GATE_LORA), (0, 0)))
    seg_w = _block_diag_ones(w, RWKV_HEAD_DIM)
    (rt, at, kt, bt, kh, bh, v, dc, bonus, gate) = _rwkv_prep(
        p_rwkv, _pad_cols(row(rwkv_mix[l]), rwkv_pad), wwa, gup, row(rwkv_w0[l]), row(rwkv_a0[l]),
        row(rwkv_k_k[l]), row(rwkv_k_a[l]), row(rwkv_r_k[l]), seg_w, _chunk_lower_ones(prep_ts),
        seq=seq, ts=prep_ts)
    o_rwkv = _rwkv_chunk(rt, at, kt, bt, kh, bh, v, dc, batch=batch, seq=seq, prep_ts=prep_ts)

    cos_t, sin_a, sin_b = _rope_tables(np.arange(seq))
    two = lambda tab: jnp.concatenate([tab, tab], axis=1)
    qn = jnp.tile(row(nsa_q_norm[l]), (1, NSA_HEADS))
    kn = jnp.tile(nsa_k_norm[l], (1, NSA_KV_HEADS))
    q, ks, vs, kw, vw, gates = _nsa_prep(p_nsa, two(cos_t), two(sin_a), two(sin_b), qn, kn,
                                         _block_diag_ones(qw, dh), batch=batch, seq=seq)

    ngrp = seq // CMP_STRIDE
    grp_w = CMP_STRIDE * dh

    def groups(cols):
        g = cols.reshape(batch, ngrp, CMP_STRIDE, NSA_KV_HEADS, dh)
        return jnp.transpose(g, (0, 3, 1, 2, 4)).reshape(batch, NSA_KV_HEADS, ngrp, grp_w)

    cend = np.arange(ngrp) * CMP_STRIDE + CMP_BLOCK - 1
    cos_c, sa_c, sb_c = _rope_tables(cend)
    kc, vc = _nsa_compress(
        groups(p_nsa[:, qw:qw + kvw]), groups(p_nsa[:, qw + kvw:qw + 2 * kvw]),
        cmp_pos_k[l].reshape(1, -1), cmp_pos_v[l].reshape(1, -1),
        cmp_k_w1[l], cmp_k_w2[l], cmp_v_w1[l], cmp_v_w2[l], nsa_k_norm[l],
        cos_c, sb_c - sa_c, _rot_half_matrix())
    nsel = seq // SEL_BLOCK
    y_nsa = _nsa_attn(q, kc, vc, ks, vs, kw, vw, gates, _overlap_matrix(ngrp, nsel),
                      _expand_matrix(nsel, seq), batch=batch, seq=seq)

    x = _merge(x, o_rwkv, bonus, gate, y_nsa, p_gate, row(rwkv_gn_w[l]), row(rwkv_gn_b[l]), seg_w,
               w_branch_rwkv[l].astype(BF16), w_branch_nsa[l].astype(BF16), w_out[l].astype(BF16))
    return _ffn(x, row(ffn2_norm[l]), ffn2_w_gate[l].astype(BF16), ffn2_w_up[l].astype(BF16),
                ffn2_w_down[l].astype(BF16))


def kernel(x, ffn1_norm, ffn1_w_gate, ffn1_w_up, ffn1_w_down, mix_norm, w_in, rwkv_mix, rwkv_w0, rwkv_w_up, rwkv_a0, rwkv_a_up, rwkv_g_up, rwkv_k_k, rwkv_k_a, rwkv_r_k, rwkv_gn_w, rwkv_gn_b, nsa_q_norm, nsa_k_norm, cmp_pos_k, cmp_pos_v, cmp_k_w1, cmp_k_w2, cmp_v_w1, cmp_v_w2, w_branch_rwkv, w_branch_nsa, w_out, ffn2_norm, ffn2_w_gate, ffn2_w_up, ffn2_w_down):
    batch, seq, d = x.shape
    params = (ffn1_norm, ffn1_w_gate, ffn1_w_up, ffn1_w_down, mix_norm, w_in, rwkv_mix, rwkv_w0,
              rwkv_w_up, rwkv_a0, rwkv_a_up, rwkv_g_up, rwkv_k_k, rwkv_k_a, rwkv_r_k, rwkv_gn_w,
              rwkv_gn_b, nsa_q_norm, nsa_k_norm, cmp_pos_k, cmp_pos_v, cmp_k_w1, cmp_k_w2, cmp_v_w1,
              cmp_v_w2, w_branch_rwkv, w_branch_nsa, w_out, ffn2_norm, ffn2_w_gate, ffn2_w_up,
              ffn2_w_down)
    y = x.reshape(batch * seq, d)
    for l in range(ffn1_norm.shape[0]):
        y = _layer(y, l, *params, batch=batch, seq=seq)
    return y.reshape(batch, seq, d)
```

```python
import functools

import numpy as np
import jax
import jax.numpy as jnp
from jax import lax
from jax.experimental import pallas as pl
from jax.experimental.pallas import tpu as pltpu

F32 = jnp.float32
BF16 = jnp.bfloat16

RWKV_HEAD_DIM = 64
DECAY_LORA = 64
ICLR_LORA = 64
GATE_LORA = 160
GN_EPS = 64e-5
NSA_HEADS = 16
NSA_KV_HEADS = 4
NSA_GROUP = NSA_HEADS // NSA_KV_HEADS
NSA_HEAD_DIM = 64
ROPE_DIM = NSA_HEAD_DIM // 4
ROPE_HALF = ROPE_DIM // 2
ROPE_THETA = 500000.0
CMP_BLOCK = 32
CMP_STRIDE = 16
SEL_BLOCK = 64
SEL_TOP = 16
WINDOW = 512
NORM_EPS = 1e-6

LANE = 128
CHUNK = 64
VMEM_LIMIT = 56 * 1024 * 1024
MASKED = -1e30


def _cparams(sem):
    return pltpu.CompilerParams(dimension_semantics=sem, vmem_limit_bytes=VMEM_LIMIT)


def _const_spec(shape):
    nd = len(shape)
    return pl.BlockSpec(shape, lambda *_: (0,) * nd, pipeline_mode=pl.Buffered(1))


def _mm(a, b):
    return lax.dot_general(a, b, (((1,), (0,)), ((), ())), preferred_element_type=F32)


def _mm_nt(a, b):
    return lax.dot_general(a, b, (((1,), (1,)), ((), ())), preferred_element_type=F32)


def _mm_tn(a, b):
    return lax.dot_general(a, b, (((0,), (0,)), ((), ())), preferred_element_type=F32)


def _split2(x):
    hi = x.astype(BF16)
    lo = (x - hi.astype(F32)).astype(BF16)
    return hi, lo


def _split3(x):
    h1 = x.astype(BF16)
    r1 = x - h1.astype(F32)
    h2 = r1.astype(BF16)
    h3 = (r1 - h2.astype(F32)).astype(BF16)
    return h1, h2, h3


def _dot3(a, b, mm=_mm):
    a1, a2 = _split2(a)
    b1, b2 = _split2(b)
    return mm(a1, b1) + (mm(a1, b2) + mm(a2, b1))


def _dot_exact_rhs(a, b_bf16):
    a1, a2, a3 = _split3(a)
    return _mm(a1, b_bf16) + (_mm(a2, b_bf16) + _mm(a3, b_bf16))


def _dot_exact_lhs(a_bf16, b):
    b1, b2, b3 = _split3(b)
    return _mm(a_bf16, b1) + (_mm(a_bf16, b2) + _mm(a_bf16, b3))


def _sigmoid(x):
    return 1.0 / (1.0 + jnp.exp(-x))


def _iota(shape, dim):
    return lax.broadcasted_iota(jnp.int32, shape, dim)


def _ffn_kernel(x_ref, g_ref, wg_ref, wu_ref, wd_ref, o_ref, h_ref, acc_ref):
    j = pl.program_id(1)

    @pl.when(j == 0)
    def _():
        x = x_ref[...]
        ms = jnp.mean(x * x, axis=-1, keepdims=True)
        h_ref[...] = (x * lax.rsqrt(ms + NORM_EPS) * g_ref[...]).astype(BF16)
        acc_ref[...] = jnp.zeros_like(acc_ref)

    h = h_ref[...]
    gate = _mm(h, wg_ref[...])
    up = _mm(h, wu_ref[...])
    act = (gate * _sigmoid(gate) * up).astype(BF16)
    acc_ref[...] += _mm(act, wd_ref[...])

    @pl.when(j == pl.num_programs(1) - 1)
    def _():
        o_ref[...] = x_ref[...] + 0.5 * acc_ref[...]


def _ffn(x, g, wg, wu, wd, *, tm=512, tf=512):
    t, d = x.shape
    f = wg.shape[1]
    return pl.pallas_call(
        _ffn_kernel,
        out_shape=jax.ShapeDtypeStruct((t, d), F32),
        grid=(t // tm, f // tf),
        in_specs=[
            pl.BlockSpec((tm, d), lambda i, j: (i, 0)),
            pl.BlockSpec((1, d), lambda i, j: (0, 0)),
            pl.BlockSpec((d, tf), lambda i, j: (0, j)),
            pl.BlockSpec((d, tf), lambda i, j: (0, j)),
            pl.BlockSpec((tf, d), lambda i, j: (j, 0)),
        ],
        out_specs=pl.BlockSpec((tm, d), lambda i, j: (i, 0)),
        scratch_shapes=[pltpu.VMEM((tm, d), BF16), pltpu.VMEM((tm, d), F32)],
        compiler_params=_cparams(("parallel", "arbitrary")),
        name="ffn",
    )(x, g, wg, wu, wd)


def _norm_proj_kernel(x_ref, g_ref, w_ref, o_ref):
    x = x_ref[...]
    ms = jnp.mean(x * x, axis=-1, keepdims=True)
    h = (x * lax.rsqrt(ms + NORM_EPS) * g_ref[...]).astype(BF16)
    o_ref[...] = _mm(h, w_ref[...]).astype(o_ref.dtype)


def _norm_proj(x, g, w, *, tm=256, name="norm_proj"):
    t, d = x.shape
    n = w.shape[1]
    return pl.pallas_call(
        _norm_proj_kernel,
        out_shape=jax.ShapeDtypeStruct((t, n), F32),
        grid=(t // tm,),
        in_specs=[
            pl.BlockSpec((tm, d), lambda i: (i, 0)),
            _const_spec((1, d)),
            _const_spec((d, n)),
        ],
        out_specs=pl.BlockSpec((tm, n), lambda i: (i, 0)),
        compiler_params=_cparams(("parallel",)),
        name=name,
    )(x, g, w)


def _rwkv_prep_kernel(p_ref, prev_ref, mix_ref, wwa_ref, gup_ref, w0_ref, a0_ref, kk_ref, ka_ref,
                      rk_ref, seg_ref, tri_ref,
                      rt_ref, at_ref, kt_ref, bt_ref, kh_ref, bh_ref, v_ref, dc_ref, bonus_ref,
                      gate_ref, *, seq_tiles):
    i = pl.program_id(0)
    ts = p_ref.shape[0]
    w = rt_ref.shape[1]
    p = p_ref[...]
    prev = prev_ref[7:8, :]
    prev = jnp.where(i % seq_tiles == 0, jnp.zeros_like(prev), prev)
    shifted = pltpu.roll(p, 1, axis=0)
    shifted = jnp.where(_iota(p.shape, 0) == 0, prev, shifted)
    xs = p + mix_ref[...] * (shifted - p)

    r = xs[:, 0:w]
    k = xs[:, w:2 * w]
    v = xs[:, 2 * w:3 * w]
    lo = 3 * w
    pwa = xs[:, lo:lo + LANE]
    pg = xs[:, lo + LANE:lo + 3 * LANE]
    lane = _iota(pwa.shape, 1)
    z = jnp.where(lane < DECAY_LORA, jnp.tanh(pwa), pwa)
    wa = _dot3(z, wwa_ref[...])
    wl = w0_ref[...] + wa[:, :w]
    neg = -wl
    softplus = jnp.maximum(neg, 0.0) + jnp.log(1.0 + jnp.exp(-jnp.abs(neg)))
    lw = -jnp.exp(-softplus - 0.5)
    a = _sigmoid(a0_ref[...] + wa[:, w:])
    gate_ref[...] = _dot3(_sigmoid(pg), gup_ref[...])

    seg = seg_ref[...]
    kk = k * kk_ref[...]
    ss = _dot_exact_rhs(kk * kk, seg)
    kk = kk * lax.rsqrt(jnp.maximum(ss, 1e-24))
    k2 = k * (1.0 + (a - 1.0) * ka_ref[...])
    bonus_ref[...] = _dot_exact_rhs(r * k2 * rk_ref[...], seg) * v

    gc = _dot_exact_lhs(tri_ref[...], lw)
    nc = ts // CHUNK
    ends = [gc[(q + 1) * CHUNK - 1:(q + 1) * CHUNK, :] for q in range(nc)]
    gend = jnp.concatenate([jnp.broadcast_to(e, (CHUNK, w)) for e in ends], axis=0)
    to_end = jnp.exp(gend - gc)
    e_in = jnp.exp(gc)
    e_out = jnp.exp(-gc)
    b = kk * a
    rt_ref[...] = r * e_in
    at_ref[...] = -kk * jnp.exp(gc - lw)
    kt_ref[...] = k2 * e_out
    bt_ref[...] = b * e_out
    kh_ref[...] = k2 * to_end
    bh_ref[...] = b * to_end
    v_ref[...] = v
    dc_ref[0] = jnp.concatenate([jnp.exp(e) for e in ends] + [jnp.zeros((8 - nc, w), F32)], axis=0)


def _rwkv_prep(p_rwkv, mix, wwa, gup, w0, a0, k_k, k_a, r_k, seg, tri, *, seq, ts=256):
    t, pc = p_rwkv.shape
    w = w0.shape[1]
    nt = t // ts
    row = lambda i: (i, 0)
    tok = pl.BlockSpec((ts, w), row)
    tok_shape = jax.ShapeDtypeStruct((t, w), F32)
    return pl.pallas_call(
        functools.partial(_rwkv_prep_kernel, seq_tiles=seq // ts),
        out_shape=[tok_shape] * 7 + [jax.ShapeDtypeStruct((nt, 8, w), F32), tok_shape, tok_shape],
        grid=(nt,),
        in_specs=[
            pl.BlockSpec((ts, pc), row),
            pl.BlockSpec((8, pc), lambda i: (jnp.maximum(i * (ts // 8) - 1, 0), 0)),
            _const_spec((1, pc)),
            _const_spec(wwa.shape),
            _const_spec(gup.shape),
            _const_spec((1, w)), _const_spec((1, w)), _const_spec((1, w)), _const_spec((1, w)),
            _const_spec((1, w)),
            _const_spec(seg.shape),
            _const_spec(tri.shape),
        ],
        out_specs=[tok] * 7 + [pl.BlockSpec((1, 8, w), lambda i: (i, 0, 0)), tok, tok],
        compiler_params=_cparams(("parallel",)),
        name="rwkv_prep",
    )(p_rwkv, p_rwkv, mix, wwa, gup, w0, a0, k_k, k_a, r_k, seg, tri)


PASSES_INTRA = 3
PASSES_SOLVE = 3
PASSES_APPLY = 3
PASSES_SCAN = 3


def _operand(x, passes):
    return _split2(x) if passes == 3 else (x.astype(BF16),)


def _prod(a, b, mm=_mm):
    if len(a) == 2 and len(b) == 2:
        return mm(a[0], b[0]) + (mm(a[0], b[1]) + mm(a[1], b[0]))
    return mm(a[0], b[0])


def _unit_lower_inverse(a_list, row, col, passes):
    eye = (row == col).astype(F32)
    same8 = (row // 8) == (col // 8)
    a8 = [jnp.where(same8, a, 0.0) for a in a_list]
    s8 = [_operand(x, passes) for x in a8]
    a8_2 = [_prod(s, s) for s in s8]
    s8_2 = [_operand(x, passes) for x in a8_2]
    a8_4 = [_prod(s, s) for s in s8_2]
    p = [eye + x + x2 + _prod(s, s2) for x, x2, s, s2 in zip(a8, a8_2, s8, s8_2)]
    t = [pp + _prod(_operand(pp, passes), _operand(x4, passes)) for pp, x4 in zip(p, a8_4)]
    m = 16
    while m <= CHUNK:
        sel = ((row // m) == (col // m)) & ((row // (m // 2)) != (col // (m // 2)))
        off = [_operand(jnp.where(sel, a, 0.0), passes) for a in a_list]
        ts = [_operand(x, passes) for x in t]
        mid = [_prod(s, o) for s, o in zip(ts, off)]
        t = [x + _prod(_operand(md, passes), s) for x, md, s in zip(t, mid, ts)]
        m *= 2
    return t


def _rwkv_intra_kernel(rt_ref, at_ref, kt_ref, bt_ref, kh_ref, bh_ref, v_ref, dc_ref,
                       rr_ref, o0_ref, gh_ref, *, heads, chunks_per_tile):
    n = RWKV_HEAD_DIM
    i = pl.program_id(0)
    row = _iota((CHUNK, CHUNK), 0)
    col = _iota((CHUNK, CHUNK), 1)
    strict = col < row
    incl = col <= row
    dc_all = dc_ref[0]
    dc_row = dc_all[0:1, :]
    for q in range(1, chunks_per_tile):
        dc_row = jnp.where(i % chunks_per_tile == q, dc_all[q:q + 1, :], dc_row)

    sls = [slice(h * n, (h + 1) * n) for h in range(heads)]
    at = [at_ref[:, s] for s in sls]
    rt = [rt_ref[:, s] for s in sls]
    x2 = [_operand(jnp.concatenate([bt_ref[:, s], kt_ref[:, s]], axis=0), PASSES_INTRA) for s in sls]
    m1 = [_prod(_operand(a, PASSES_INTRA), x, _mm_nt) for a, x in zip(at, x2)]
    m2 = [_prod(_operand(r, PASSES_INTRA), x, _mm_nt) for r, x in zip(rt, x2)]
    v_s = [_operand(v_ref[:, s], PASSES_APPLY) for s in sls]
    akrk = [jnp.concatenate([jnp.where(strict, a[:, CHUNK:], 0.0), jnp.where(incl, r[:, CHUNK:], 0.0)],
                            axis=0) for a, r in zip(m1, m2)]
    avv = [_prod(_operand(a, PASSES_APPLY), x) for a, x in zip(akrk, v_s)]
    tinv = _unit_lower_inverse([jnp.where(strict, a[:, :CHUNK], 0.0) for a in m1], row, col,
                               PASSES_SOLVE)
    rhs = [jnp.concatenate([a, w[:CHUNK]], axis=1) for a, w in zip(at, avv)]
    tz = [_prod(_operand(t, PASSES_APPLY), _operand(r, PASSES_APPLY)) for t, r in zip(tinv, rhs)]
    tz_s = [_operand(z, PASSES_APPLY) for z in tz]
    rz = [_prod(_operand(jnp.where(incl, r[:, :CHUNK], 0.0), PASSES_APPLY), z) for r, z in zip(m2, tz_s)]
    bz = [_prod(_operand(bh_ref[:, s], PASSES_APPLY), z, _mm_tn) for s, z in zip(sls, tz_s)]
    kv = [_prod(_operand(kh_ref[:, s], PASSES_APPLY), x, _mm_tn) for s, x in zip(sls, v_s)]
    for h, s in enumerate(sls):
        rr_ref[:, s] = rt[h] + rz[h][:, :n]
        o0_ref[:, s] = rz[h][:, n:] + avv[h][CHUNK:]
        dmat = jnp.where(row == col, jnp.broadcast_to(dc_row[:, s], (n, n)), 0.0)
        gh_ref[0, h] = bz[h] + jnp.concatenate([dmat, kv[h]], axis=1)


def _rwkv_scan_kernel(rr_ref, o0_ref, gh_ref, o_ref, state_ref, *, heads):
    n = RWKV_HEAD_DIM

    @pl.when(pl.program_id(1) == 0)
    def _():
        state_ref[...] = jnp.zeros_like(state_ref)

    sls = [slice(h * n, (h + 1) * n) for h in range(heads)]
    h0 = [_operand(state_ref[h], PASSES_SCAN) for h in range(heads)]
    outs = [_prod(_operand(rr_ref[:, s], PASSES_SCAN), x) + o0_ref[:, s] for s, x in zip(sls, h0)]
    new = [_prod(_operand(gh_ref[0, h, :, 0:n], PASSES_SCAN), h0[h]) + gh_ref[0, h, :, n:2 * n]
           for h in range(heads)]
    for h, s in enumerate(sls):
        o_ref[:, s] = outs[h]
        state_ref[h] = new[h]


def _rwkv_chunk(rt, at, kt, bt, kh, bh, v, dc, *, batch, seq, prep_ts):
    t, w = rt.shape
    n = RWKV_HEAD_DIM
    heads = w // n
    nchunk = seq // CHUNK
    cpt = prep_ts // CHUNK
    tok = pl.BlockSpec((CHUNK, w), lambda i: (i, 0))
    tok_shape = jax.ShapeDtypeStruct((t, w), F32)
    rr, o0, gh = pl.pallas_call(
        functools.partial(_rwkv_intra_kernel, heads=heads, chunks_per_tile=cpt),
        out_shape=[tok_shape, tok_shape, jax.ShapeDtypeStruct((t // CHUNK, heads, n, 2 * n), F32)],
        grid=(t // CHUNK,),
        in_specs=[tok] * 7 + [pl.BlockSpec((1, 8, w), lambda i: (i // cpt, 0, 0))],
        out_specs=[tok, tok, pl.BlockSpec((1, heads, n, 2 * n), lambda i: (i, 0, 0, 0))],
        compiler_params=_cparams(("parallel",)),
        name="rwkv_intra",
    )(rt, at, kt, bt, kh, bh, v, dc)
    tok2 = pl.BlockSpec((CHUNK, w), lambda b, c: (b * nchunk + c, 0))
    return pl.pallas_call(
        functools.partial(_rwkv_scan_kernel, heads=heads),
        out_shape=tok_shape,
        grid=(batch, nchunk),
        in_specs=[tok2, tok2, pl.BlockSpec((1, heads, n, 2 * n), lambda b, c: (b * nchunk + c, 0, 0, 0))],
        out_specs=tok2,
        scratch_shapes=[pltpu.VMEM((heads, n, n), F32)],
        compiler_params=_cparams(("parallel", "arbitrary")),
        name="rwkv_scan",
    )(rr, o0, gh)


def _rope_lanes(x, cos_t, sin_a, sin_b):
    width = x.shape[1]
    up = pltpu.roll(x, width - ROPE_HALF, axis=1)
    dn = pltpu.roll(x, ROPE_HALF, axis=1)
    return x * cos_t + up * sin_a + dn * sin_b


def _tile_lanes(tab, width):
    return jnp.concatenate([tab] * (width // tab.shape[1]), axis=1)


def _nsa_prep_kernel(p_ref, cos_ref, sa_ref, sb_ref, qn_ref, kn_ref, seg_ref,
                     q_ref, ks_ref, vs_ref, kw_ref, vw_ref, g_ref):
    dh = NSA_HEAD_DIM
    qw = q_ref.shape[1]
    kvw = NSA_KV_HEADS * dh
    seg = seg_ref[...]
    cos_t, sin_a, sin_b = cos_ref[...], sa_ref[...], sb_ref[...]

    def norm_rope(x, gain):
        wd = x.shape[1]
        ms = _dot_exact_rhs(x * x, seg[:wd, :wd]) * (1.0 / dh)
        y = x * lax.rsqrt(ms + NORM_EPS) * gain
        return _rope_lanes(y, _tile_lanes(cos_t, wd), _tile_lanes(sin_a, wd), _tile_lanes(sin_b, wd))

    q = norm_rope(p_ref[:, 0:qw], qn_ref[...])
    q_ref[...] = q * (dh ** -0.5)
    base = qw + 2 * kvw
    ks = norm_rope(p_ref[:, base:base + kvw], kn_ref[1:2, :]).astype(BF16)
    vs = p_ref[:, base + kvw:base + 2 * kvw].astype(BF16)
    kw = norm_rope(p_ref[:, base + 2 * kvw:base + 3 * kvw], kn_ref[2:3, :]).astype(BF16)
    vw = p_ref[:, base + 3 * kvw:base + 4 * kvw].astype(BF16)
    for h in range(NSA_KV_HEADS):
        sl = slice(h * dh, (h + 1) * dh)
        ks_ref[0, h] = ks[:, sl]
        vs_ref[0, h] = vs[:, sl]
        kw_ref[0, h] = kw[:, sl]
        vw_ref[0, h] = vw[:, sl]
    sig = _sigmoid(p_ref[:, base + 4 * kvw:base + 4 * kvw + LANE])
    per_head = 3 * NSA_GROUP
    for h in range(NSA_KV_HEADS):
        g_ref[:, h * LANE:(h + 1) * LANE] = sig if h == 0 else pltpu.roll(sig, LANE - per_head * h, axis=1)


def _nsa_prep(p_nsa, cos_t, sin_a, sin_b, qn, kn, seg, *, batch, seq, ts=256):
    t, pc = p_nsa.shape
    qw = NSA_HEADS * NSA_HEAD_DIM
    st = seq // ts
    tab = pl.BlockSpec((ts, LANE), lambda i: (i % st, 0))
    hm = pl.BlockSpec((1, NSA_KV_HEADS, ts, NSA_HEAD_DIM), lambda i: (i // st, 0, i % st, 0))
    hm_shape = jax.ShapeDtypeStruct((batch, NSA_KV_HEADS, seq, NSA_HEAD_DIM), BF16)
    return pl.pallas_call(
        _nsa_prep_kernel,
        out_shape=[jax.ShapeDtypeStruct((t, qw), F32)] + [hm_shape] * 4
        + [jax.ShapeDtypeStruct((t, NSA_KV_HEADS * LANE), F32)],
        grid=(t // ts,),
        in_specs=[pl.BlockSpec((ts, pc), lambda i: (i, 0)), tab, tab, tab,
                  _const_spec(qn.shape), _const_spec(kn.shape), _const_spec(seg.shape)],
        out_specs=[pl.BlockSpec((ts, qw), lambda i: (i, 0))] + [hm] * 4
        + [pl.BlockSpec((ts, NSA_KV_HEADS * LANE), lambda i: (i, 0))],
        compiler_params=_cparams(("parallel",)),
        name="nsa_prep",
    )(p_nsa, cos_t, sin_a, sin_b, qn, kn, seg)


def _gelu_tanh(x):
    return 0.5 * x * (1.0 + jnp.tanh(np.sqrt(2.0 / np.pi).astype(np.float32) * (x + 0.044715 * (x * x * x))))


def _compress_kernel(gk_ref, gv_ref, pk_ref, pv_ref, k1_ref, k2_ref, v1_ref, v2_ref, kn_ref,
                     cos_ref, sin_ref, rot_ref, kc_ref, vc_ref):
    half = k1_ref.shape[0] // 2

    def mlp(g, pos, w1_ref, w2_ref):
        ya = _dot3(g, w1_ref[0:half, :])
        yb = _dot3(g, w1_ref[half:, :])
        bias = _dot3(jnp.broadcast_to(pos, (8, pos.shape[1])), w1_ref[...])[0:1, :]
        n = g.shape[0]
        hid = ya + pltpu.roll(yb, n - 1, axis=0) + bias
        return _dot3(_gelu_tanh(hid), w2_ref[...])

    kc = mlp(gk_ref[0, 0], pk_ref[...], k1_ref, k2_ref)
    ms = jnp.mean(kc * kc, axis=-1, keepdims=True)
    kc = kc * lax.rsqrt(ms + NORM_EPS) * kn_ref[0:1, :]
    kc_ref[0, 0] = kc * cos_ref[...] + _dot_exact_rhs(kc, rot_ref[...]) * sin_ref[...]
    vc_ref[0, 0] = mlp(gv_ref[0, 0], pv_ref[...], v1_ref, v2_ref)


def _nsa_compress(gk, gv, pk, pv, k1, k2, v1, v2, kn, cos_c, sin_c, rot):
    b, hk, ng, gw = gk.shape
    dh = NSA_HEAD_DIM
    grp = pl.BlockSpec((1, 1, ng, gw), lambda i, j: (i, j, 0, 0))
    out = pl.BlockSpec((1, 1, ng, dh), lambda i, j: (i, j, 0, 0))
    shape = jax.ShapeDtypeStruct((b, hk, ng, dh), F32)
    consts = [pk, pv, k1, k2, v1, v2, kn, cos_c, sin_c, rot]
    return pl.pallas_call(
        _compress_kernel,
        out_shape=[shape, shape],
        grid=(b, hk),
        in_specs=[grp, grp] + [_const_spec(c.shape) for c in consts],
        out_specs=[out, out],
        compiler_params=_cparams(("parallel", "parallel")),
        name="nsa_compress",
    )(gk, gv, *consts)


SLC_KEY_BLOCK = 256


def _nsa_attn_kernel(q_ref, kc_ref, vc_ref, ks_ref, vs_ref, kw_ref, vw_ref, g_ref, ovt_ref, ex_ref,
                     o_ref):
    dh = NSA_HEAD_DIM
    grp = NSA_GROUP
    qi = pl.program_id(2)
    tq = q_ref.shape[0]
    rows = grp * tq
    t0 = qi * tq
    ncmp = kc_ref.shape[2]
    nsel = ovt_ref.shape[0]
    kb = ex_ref.shape[2]
    span = WINDOW + tq

    q = q_ref[...]
    q4 = jnp.concatenate([q[:, g * dh:(g + 1) * dh] for g in range(grp)], axis=0)
    q4b = q4.astype(BF16)

    st = _dot3(kc_ref[0, 0], q4, _mm_nt)
    tl = t0 + _iota((ncmp, rows), 1) % tq
    cmask = _iota((ncmp, rows), 0) * CMP_STRIDE + (CMP_BLOCK - 1) <= tl
    sm = jnp.where(cmask, st, MASKED)
    e = jnp.where(cmask, jnp.exp(sm - jnp.max(sm, axis=0, keepdims=True)), 0.0)
    pt = e / jnp.maximum(jnp.sum(e, axis=0, keepdims=True), 1e-30)
    o_cmp = _mm_tn(pt.astype(BF16), vc_ref[0, 0].astype(BF16))
    psum = pt[:, 0:tq]
    for g in range(1, grp):
        psum = psum + pt[:, g * tq:(g + 1) * tq]

    imp = _dot_exact_lhs(ovt_ref[...], psum)
    blk = _iota((nsel, tq), 0)
    cur = (t0 + _iota((nsel, tq), 1)) // SEL_BLOCK
    forced = (blk == 0) | (blk == cur) | (blk == cur - 1)
    imp = jnp.where(forced, jnp.inf, jnp.where(blk > cur, -jnp.inf, imp))
    rank = jnp.zeros((nsel, tq), jnp.int32)
    for m in range(nsel):
        im = imp[m:m + 1, :]
        ahead = (im > imp) | ((im == imp) & (m < blk))
        rank = rank + ahead.astype(jnp.int32)
    sel_bias = jnp.where(rank < min(SEL_TOP, nsel), 0.0, MASKED).astype(BF16)

    def slc_step(j, carry):
        m_run, l_run, acc = carry
        k0 = pl.multiple_of(j * kb, kb)
        s = _mm_nt(q4b, ks_ref[0, 0, pl.ds(k0, kb), :])
        bias = _mm_tn(sel_bias, ex_ref[j])
        bias = jnp.where(k0 + _iota((tq, kb), 1) <= t0 + _iota((tq, kb), 0), bias, MASKED)
        s = (s.reshape(grp, tq, kb) + bias[None]).reshape(rows, kb)
        m_new = jnp.maximum(m_run, jnp.max(s, axis=1, keepdims=True))
        alpha = jnp.exp(m_run - m_new)
        p = jnp.exp(s - m_new)
        l_new = alpha * l_run + jnp.sum(p, axis=1, keepdims=True)
        acc = alpha * acc + _mm(p.astype(BF16), vs_ref[0, 0, pl.ds(k0, kb), :])
        return m_new, l_new, acc

    init = (jnp.full((rows, 1), MASKED, F32), jnp.zeros((rows, 1), F32), jnp.zeros((rows, dh), F32))
    _, l_slc, acc_slc = lax.fori_loop(0, (t0 + tq + kb - 1) // kb, slc_step, init)
    o_slc = acc_slc / l_slc

    w0 = pl.multiple_of(jnp.maximum(t0 - WINDOW, 0), tq)
    kpos = w0 + _iota((tq, span), 1)
    tw = t0 + _iota((tq, span), 0)
    wbias = jnp.where((kpos <= tw) & (kpos > tw - WINDOW), 0.0, MASKED)
    s = _mm_nt(q4b, kw_ref[0, 0, pl.ds(w0, span), :])
    s = (s.reshape(grp, tq, span) + wbias[None]).reshape(rows, span)
    p = jnp.exp(s - jnp.max(s, axis=1, keepdims=True))
    o_win = _mm(p.astype(BF16), vw_ref[0, 0, pl.ds(w0, span), :]) / jnp.sum(p, axis=1, keepdims=True)

    gates = g_ref[...]
    gcol = [jnp.concatenate([gates[:, 3 * g + br:3 * g + br + 1] for g in range(grp)], axis=0)
            for br in range(3)]
    o4 = gcol[0] * o_cmp + gcol[1] * o_slc + gcol[2] * o_win
    o_ref[...] = jnp.concatenate([o4[g * tq:(g + 1) * tq] for g in range(grp)], axis=1)


def _nsa_attn(q, kc, vc, ks, vs, kw, vw, gates, overlap_t, expand, *, batch, seq, tq=128):
    t, qw = q.shape
    dh = NSA_HEAD_DIM
    gw = NSA_GROUP * dh
    st = seq // tq
    ncmp = kc.shape[2]
    cmp_spec = pl.BlockSpec((1, 1, ncmp, dh), lambda b, h, i: (b, h, 0, 0))
    kv_spec = pl.BlockSpec((1, 1, seq, dh), lambda b, h, i: (b, h, 0, 0))
    return pl.pallas_call(
        _nsa_attn_kernel,
        out_shape=jax.ShapeDtypeStruct((t, qw), F32),
        grid=(batch, NSA_KV_HEADS, st),
        in_specs=[
            pl.BlockSpec((tq, gw), lambda b, h, i: (b * st + i, h)),
            cmp_spec, cmp_spec, kv_spec, kv_spec, kv_spec, kv_spec,
            pl.BlockSpec((tq, LANE), lambda b, h, i: (b * st + i, h)),
            _const_spec(overlap_t.shape), _const_spec(expand.shape),
        ],
        out_specs=pl.BlockSpec((tq, gw), lambda b, h, i: (b * st + i, h)),
        compiler_params=_cparams(("parallel", "parallel", "arbitrary")),
        name="nsa_attn",
    )(q, kc, vc, ks, vs, kw, vw, gates, overlap_t, expand)


def _merge_kernel(x_ref, o_ref, bonus_ref, gate_ref, yb_ref, pg_ref, gnw_ref, gnb_ref, seg_ref,
                  ua_ref, ub_ref, wo_ref, out_ref):
    d = x_ref.shape[1]
    n = RWKV_HEAD_DIM
    seg = seg_ref[...]
    o = o_ref[...]
    mu = _dot_exact_rhs(o, seg) * (1.0 / n)
    dlt = o - mu
    var = _dot_exact_rhs(dlt * dlt, seg) * (1.0 / n)
    on = dlt * lax.rsqrt(var + GN_EPS) * gnw_ref[...] + gnb_ref[...]
    ya = ((on + bonus_ref[...]) * gate_ref[...]).astype(BF16)
    yb = yb_ref[...].astype(BF16)
    merged = (_sigmoid(pg_ref[:, 0:d]) * _mm(ya, ua_ref[...])
              + _sigmoid(pg_ref[:, d:2 * d]) * _mm(yb, ub_ref[...]))
    out_ref[...] = x_ref[...] + _mm(merged.astype(BF16), wo_ref[...])


def _merge(x, o_rwkv, bonus, gate, yb, pg, gnw, gnb, seg, ua, ub, wo, *, tm=256):
    t, d = x.shape
    w = o_rwkv.shape[1]
    row = lambda i: (i, 0)
    tokw = pl.BlockSpec((tm, w), row)
    return pl.pallas_call(
        _merge_kernel,
        out_shape=jax.ShapeDtypeStruct((t, d), F32),
        grid=(t // tm,),
        in_specs=[pl.BlockSpec((tm, d), row), tokw, tokw, tokw, tokw,
                  pl.BlockSpec((tm, 2 * d), row),
                  _const_spec((1, w)), _const_spec((1, w)), _const_spec(seg.shape),
                  _const_spec(ua.shape), _const_spec(ub.shape), _const_spec(wo.shape)],
        out_specs=pl.BlockSpec((tm, d), row),
        compiler_params=_cparams(("parallel",)),
        name="merge",
    )(x, o_rwkv, bonus, gate, yb, pg, gnw, gnb, seg, ua, ub, wo)


def _block_diag_ones(width, block):
    idx = np.arange(width) // block
    return jnp.asarray(idx[:, None] == idx[None, :], BF16)


def _chunk_lower_ones(ts):
    i = np.arange(ts)
    return jnp.asarray((i[:, None] // CHUNK == i[None, :] // CHUNK) & (i[None, :] <= i[:, None]), BF16)


def _rope_tables(pos):
    inv = ROPE_THETA ** (-jnp.arange(ROPE_HALF, dtype=F32) / ROPE_HALF)
    ang = jnp.asarray(pos).astype(F32)[:, None] * inv[None, :]
    cos, sin = jnp.cos(ang), jnp.sin(ang)
    n = ang.shape[0]
    pad = jnp.zeros((n, NSA_HEAD_DIM - ROPE_DIM), F32)
    zero = jnp.zeros_like(sin)
    cos_h = jnp.concatenate([cos, cos, pad + 1.0], axis=1)
    sa_h = jnp.concatenate([-sin, zero, pad], axis=1)
    sb_h = jnp.concatenate([zero, sin, pad], axis=1)
    return cos_h, sa_h, sb_h


def _rot_half_matrix():
    r = np.zeros((NSA_HEAD_DIM, NSA_HEAD_DIM), np.float32)
    for l in range(ROPE_HALF):
        r[l + ROPE_HALF, l] = -1.0
        r[l, l + ROPE_HALF] = 1.0
    return jnp.asarray(r, BF16)


def _overlap_matrix_t(ncmp_pad, nsel):
    cs = np.arange(ncmp_pad)[None, :] * CMP_STRIDE
    ss = np.arange(nsel)[:, None] * SEL_BLOCK
    ov = np.clip(np.minimum(cs + CMP_BLOCK, ss + SEL_BLOCK) - np.maximum(cs, ss), 0, None) / CMP_BLOCK
    return jnp.asarray(ov, BF16)


def _expand_matrix(nsel, seq, kb):
    onehot = np.arange(nsel)[:, None] == (np.arange(seq)[None, :] // SEL_BLOCK)
    return jnp.asarray(onehot.reshape(nsel, seq // kb, kb).transpose(1, 0, 2), BF16)


def _pad_cols(x, width):
    return jnp.pad(x, ((0, 0), (0, width - x.shape[1])))


def _layer(x, l, ffn1_norm, ffn1_w_gate, ffn1_w_up, ffn1_w_down, mix_norm, w_in,
           rwkv_mix, rwkv_w0, rwkv_w_up, rwkv_a0, rwkv_a_up, rwkv_g_up,
           rwkv_k_k, rwkv_k_a, rwkv_r_k, rwkv_gn_w, rwkv_gn_b,
           nsa_q_norm, nsa_k_norm, cmp_pos_k, cmp_pos_v,
           cmp_k_w1, cmp_k_w2, cmp_v_w1, cmp_v_w2,
           w_branch_rwkv, w_branch_nsa, w_out,
           ffn2_norm, ffn2_w_gate, ffn2_w_up, ffn2_w_down, *, batch, seq):
    t, d = x.shape
    w = rwkv_w0.shape[1]
    dh = NSA_HEAD_DIM
    qw = NSA_HEADS * dh
    kvw = NSA_KV_HEADS * dh
    prep_ts = 256
    row = lambda v: v.reshape(1, -1)

    x = _ffn(x, row(ffn1_norm[l]), ffn1_w_gate[l].astype(BF16), ffn1_w_up[l].astype(BF16),
             ffn1_w_down[l].astype(BF16))

    wi = w_in[l]
    rwkv_cols = 3 * w + DECAY_LORA + ICLR_LORA + GATE_LORA
    rwkv_pad = 3 * w + 3 * LANE
    nsa_cols = qw + 6 * kvw + 3 * NSA_HEADS
    nsa_pad = qw + 6 * kvw + LANE
    g_mix = row(mix_norm[l])
    p_rwkv = _norm_proj(x, g_mix, _pad_cols(wi[:, :rwkv_cols], rwkv_pad).astype(BF16), name="proj_rwkv")
    p_nsa = _norm_proj(x, g_mix, _pad_cols(wi[:, rwkv_cols:rwkv_cols + nsa_cols], nsa_pad).astype(BF16),
                       name="proj_nsa")
    p_gate = _norm_proj(x, g_mix, wi[:, rwkv_cols + nsa_cols:].astype(BF16), name="proj_gate")

    wwa = jnp.zeros((LANE, 2 * w), F32)
    wwa = wwa.at[:DECAY_LORA, :w].set(rwkv_w_up[l]).at[DECAY_LORA:, w:].set(rwkv_a_up[l])
    gup = jnp.pad(rwkv_g_up[l], ((0, 2 * LANE - GATE_LORA), (0, 0)))
    seg_w = _block_diag_ones(w, RWKV_HEAD_DIM)
    (rt, at, kt, bt, kh, bh, v, dc, bonus, gate) = _rwkv_prep(
        p_rwkv, _pad_cols(row(rwkv_mix[l]), rwkv_pad), wwa, gup, row(rwkv_w0[l]), row(rwkv_a0[l]),
        row(rwkv_k_k[l]), row(rwkv_k_a[l]), row(rwkv_r_k[l]), seg_w, _chunk_lower_ones(prep_ts),
        seq=seq, ts=prep_ts)
    o_rwkv = _rwkv_chunk(rt, at, kt, bt, kh, bh, v, dc, batch=batch, seq=seq, prep_ts=prep_ts)

    cos_t, sin_a, sin_b = _rope_tables(np.arange(seq))
    two = lambda tab: jnp.concatenate([tab, tab], axis=1)
    qn = jnp.tile(row(nsa_q_norm[l]), (1, NSA_HEADS))
    kn = jnp.tile(nsa_k_norm[l], (1, NSA_KV_HEADS))
    q, ks, vs, kw, vw, gates = _nsa_prep(p_nsa, two(cos_t), two(sin_a), two(sin_b), qn, kn,
                                         _block_diag_ones(qw, dh), batch=batch, seq=seq)

    ngrp = seq // CMP_STRIDE
    grp_w = CMP_STRIDE * dh

    def groups(cols):
        g = cols.reshape(batch, ngrp, CMP_STRIDE, NSA_KV_HEADS, dh)
        return jnp.transpose(g, (0, 3, 1, 2, 4)).reshape(batch, NSA_KV_HEADS, ngrp, grp_w)

    cend = np.arange(ngrp) * CMP_STRIDE + CMP_BLOCK - 1
    cos_c, sa_c, sb_c = _rope_tables(cend)
    kc, vc = _nsa_compress(
        groups(p_nsa[:, qw:qw + kvw]), groups(p_nsa[:, qw + kvw:qw + 2 * kvw]),
        cmp_pos_k[l].reshape(1, -1), cmp_pos_v[l].reshape(1, -1),
        cmp_k_w1[l], cmp_k_w2[l], cmp_v_w1[l], cmp_v_w2[l], nsa_k_norm[l],
        cos_c, sb_c - sa_c, _rot_half_matrix())
    nsel = seq // SEL_BLOCK
    y_nsa = _nsa_attn(q, kc, vc, ks, vs, kw, vw, gates, _overlap_matrix_t(ngrp, nsel),
                      _expand_matrix(nsel, seq, SLC_KEY_BLOCK), batch=batch, seq=seq)

    x = _merge(x, o_rwkv, bonus, gate, y_nsa, p_gate, row(rwkv_gn_w[l]), row(rwkv_gn_b[l]), seg_w,
               w_branch_rwkv[l].astype(BF16), w_branch_nsa[l].astype(BF16), w_out[l].astype(BF16))
    return _ffn(x, row(ffn2_norm[l]), ffn2_w_gate[l].astype(BF16), ffn2_w_up[l].astype(BF16),
                ffn2_w_down[l].astype(BF16))


def kernel(x, ffn1_norm, ffn1_w_gate, ffn1_w_up, ffn1_w_down, mix_norm, w_in, rwkv_mix, rwkv_w0, rwkv_w_up, rwkv_a0, rwkv_a_up, rwkv_g_up, rwkv_k_k, rwkv_k_a, rwkv_r_k, rwkv_gn_w, rwkv_gn_b, nsa_q_norm, nsa_k_norm, cmp_pos_k, cmp_pos_v, cmp_k_w1, cmp_k_w2, cmp_v_w1, cmp_v_w2, w_branch_rwkv, w_branch_nsa, w_out, ffn2_norm, ffn2_w_gate, ffn2_w_up, ffn2_w_down):
    batch, seq, d = x.shape
    params = (ffn1_norm, ffn1_w_gate, ffn1_w_up, ffn1_w_down, mix_norm, w_in, rwkv_mix, rwkv_w0,
              rwkv_w_up, rwkv_a0, rwkv_a_up, rwkv_g_up, rwkv_k_k, rwkv_k_a, rwkv_r_k, rwkv_gn_w,
              rwkv_gn_b, nsa_q_norm, nsa_k_norm, cmp_pos_k, cmp_pos_v, cmp_k_w1, cmp_k_w2, cmp_v_w1,
              cmp_v_w2, w_branch_rwkv, w_branch_nsa, w_out, ffn2_norm, ffn2_w_gate, ffn2_w_up,
              ffn2_w_down)
    y = x.reshape(batch * seq, d)
    for l in range(ffn1_norm.shape[0]):
        y = _layer(y, l, *params, batch=batch, seq=seq)
    return y.reshape(batch, seq, d)
```

```python
import functools

import numpy as np
import jax
import jax.numpy as jnp
from jax import lax
from jax.experimental import pallas as pl
from jax.experimental.pallas import tpu as pltpu

F32 = jnp.float32
BF16 = jnp.bfloat16

RWKV_HEAD_DIM = 64
DECAY_LORA = 64
ICLR_LORA = 64
GATE_LORA = 160
GN_EPS = 64e-5
NSA_HEADS = 16
NSA_KV_HEADS = 4
NSA_GROUP = NSA_HEADS // NSA_KV_HEADS
NSA_HEAD_DIM = 64
ROPE_DIM = NSA_HEAD_DIM // 4
ROPE_HALF = ROPE_DIM // 2
ROPE_THETA = 500000.0
CMP_BLOCK = 32
CMP_STRIDE = 16
SEL_BLOCK = 64
SEL_TOP = 16
WINDOW = 512
NORM_EPS = 1e-6

LANE = 128
CHUNK = 64
VMEM_LIMIT = 56 * 1024 * 1024
MASKED = -1e30
LOG2_E = 1.4426950408889634
GATE_ROWS = 16


def _cparams(sem):
    return pltpu.CompilerParams(dimension_semantics=sem, vmem_limit_bytes=VMEM_LIMIT)


def _const_spec(shape):
    nd = len(shape)
    return pl.BlockSpec(shape, lambda *_: (0,) * nd, pipeline_mode=pl.Buffered(1))


def _mm(a, b):
    return lax.dot_general(a, b, (((1,), (0,)), ((), ())), preferred_element_type=F32)


def _mm_nt(a, b):
    return lax.dot_general(a, b, (((1,), (1,)), ((), ())), preferred_element_type=F32)


def _mm_tn(a, b):
    return lax.dot_general(a, b, (((0,), (0,)), ((), ())), preferred_element_type=F32)


def _split2(x):
    hi = x.astype(BF16)
    lo = (x - hi.astype(F32)).astype(BF16)
    return hi, lo


def _split3(x):
    h1 = x.astype(BF16)
    r1 = x - h1.astype(F32)
    h2 = r1.astype(BF16)
    h3 = (r1 - h2.astype(F32)).astype(BF16)
    return h1, h2, h3


def _dot3(a, b, mm=_mm):
    a1, a2 = _split2(a)
    b1, b2 = _split2(b)
    return mm(a1, b1) + (mm(a1, b2) + mm(a2, b1))


def _dot_exact_rhs(a, b_bf16):
    a1, a2 = _split2(a)
    return _mm(a1, b_bf16) + _mm(a2, b_bf16)


def _dot_exact_lhs(a_bf16, b):
    b1, b2, b3 = _split3(b)
    return _mm(a_bf16, b1) + (_mm(a_bf16, b2) + _mm(a_bf16, b3))


def _sigmoid(x):
    return 1.0 / (1.0 + jnp.exp(-x))


def _iota(shape, dim):
    return lax.broadcasted_iota(jnp.int32, shape, dim)


def _ffn_kernel(x_ref, g_ref, wg_ref, wu_ref, wd_ref, o_ref, h_ref, acc_ref):
    j = pl.program_id(1)

    @pl.when(j == 0)
    def _():
        x = x_ref[...]
        ms = jnp.mean(x * x, axis=-1, keepdims=True)
        h_ref[...] = (x * lax.rsqrt(ms + NORM_EPS) * g_ref[...]).astype(BF16)
        acc_ref[...] = jnp.zeros_like(acc_ref)

    h = h_ref[...]
    gate = _mm(h, wg_ref[...])
    up = _mm(h, wu_ref[...])
    act = (gate * _sigmoid(gate) * up).astype(BF16)
    acc_ref[...] += _mm(act, wd_ref[...])

    @pl.when(j == pl.num_programs(1) - 1)
    def _():
        o_ref[...] = x_ref[...] + 0.5 * acc_ref[...]


def _ffn(x, g, wg, wu, wd, *, tm=512, tf=512):
    t, d = x.shape
    f = wg.shape[1]
    return pl.pallas_call(
        _ffn_kernel,
        out_shape=jax.ShapeDtypeStruct((t, d), F32),
        grid=(t // tm, f // tf),
        in_specs=[
            pl.BlockSpec((tm, d), lambda i, j: (i, 0)),
            pl.BlockSpec((1, d), lambda i, j: (0, 0)),
            pl.BlockSpec((d, tf), lambda i, j: (0, j)),
            pl.BlockSpec((d, tf), lambda i, j: (0, j)),
            pl.BlockSpec((tf, d), lambda i, j: (j, 0)),
        ],
        out_specs=pl.BlockSpec((tm, d), lambda i, j: (i, 0)),
        scratch_shapes=[pltpu.VMEM((tm, d), BF16), pltpu.VMEM((tm, d), F32)],
        compiler_params=_cparams(("parallel", "arbitrary")),
        name="ffn",
    )(x, g, wg, wu, wd)


def _norm_proj_kernel(x_ref, g_ref, w_ref, o_ref):
    x = x_ref[...]
    ms = jnp.mean(x * x, axis=-1, keepdims=True)
    h = (x * lax.rsqrt(ms + NORM_EPS) * g_ref[...]).astype(BF16)
    o_ref[...] = _mm(h, w_ref[...]).astype(o_ref.dtype)


def _norm_proj(x, g, w, *, tm=256, name="norm_proj"):
    t, d = x.shape
    n = w.shape[1]
    return pl.pallas_call(
        _norm_proj_kernel,
        out_shape=jax.ShapeDtypeStruct((t, n), F32),
        grid=(t // tm,),
        in_specs=[
            pl.BlockSpec((tm, d), lambda i: (i, 0)),
            _const_spec((1, d)),
            _const_spec((d, n)),
        ],
        out_specs=pl.BlockSpec((tm, n), lambda i: (i, 0)),
        compiler_params=_cparams(("parallel",)),
        name=name,
    )(x, g, w)


def _rwkv_prep_kernel(p_ref, prev_ref, mix_ref, wwa_ref, gup_ref, w0_ref, a0_ref, kk_ref, ka_ref,
                      rk_ref, seg_ref, tri_ref,
                      rt_ref, at_ref, kt_ref, bt_ref, kh_ref, bh_ref, v_ref, dc_ref, bonus_ref,
                      gate_ref, *, seq_tiles):
    i = pl.program_id(0)
    ts = p_ref.shape[0]
    w = rt_ref.shape[1]
    p = p_ref[...]
    prev = prev_ref[7:8, :]
    prev = jnp.where(i % seq_tiles == 0, jnp.zeros_like(prev), prev)
    shifted = pltpu.roll(p, 1, axis=0)
    shifted = jnp.where(_iota(p.shape, 0) == 0, prev, shifted)
    xs = p + mix_ref[...] * (shifted - p)

    r = xs[:, 0:w]
    k = xs[:, w:2 * w]
    v = xs[:, 2 * w:3 * w]
    lo = 3 * w
    pwa = xs[:, lo:lo + LANE]
    pg = xs[:, lo + LANE:lo + 3 * LANE]
    lane = _iota(pwa.shape, 1)
    z = jnp.where(lane < DECAY_LORA, jnp.tanh(pwa), pwa)
    wa = _dot3(z, wwa_ref[...])
    wl = w0_ref[...] + wa[:, :w]
    neg = -wl
    softplus = jnp.maximum(neg, 0.0) + jnp.log(1.0 + jnp.exp(-jnp.abs(neg)))
    lw = -jnp.exp(-softplus - 0.5)
    a = _sigmoid(a0_ref[...] + wa[:, w:])
    gate_ref[...] = _dot3(_sigmoid(pg), gup_ref[...])

    seg = seg_ref[...]
    kk = k * kk_ref[...]
    ss = _dot_exact_rhs(kk * kk, seg)
    kk = kk * lax.rsqrt(jnp.maximum(ss, 1e-24))
    k2 = k * (1.0 + (a - 1.0) * ka_ref[...])
    bonus_ref[...] = _dot_exact_rhs(r * k2 * rk_ref[...], seg) * v

    gc = _dot_exact_lhs(tri_ref[...], lw)
    nc = ts // CHUNK
    ends = [gc[(q + 1) * CHUNK - 1:(q + 1) * CHUNK, :] for q in range(nc)]
    gend = jnp.concatenate([jnp.broadcast_to(e, (CHUNK, w)) for e in ends], axis=0)
    to_end = jnp.exp(gend - gc)
    e_in = jnp.exp(gc)
    e_out = jnp.exp(-gc)
    b = kk * a
    rt_ref[...] = r * e_in
    at_ref[...] = -kk * jnp.exp(gc - lw)
    kt_ref[...] = k2 * e_out
    bt_ref[...] = b * e_out
    kh_ref[...] = k2 * to_end
    bh_ref[...] = b * to_end
    v_ref[...] = v
    dc_ref[0] = jnp.concatenate([jnp.exp(e) for e in ends] + [jnp.zeros((8 - nc, w), F32)], axis=0)


def _rwkv_prep(p_rwkv, mix, wwa, gup, w0, a0, k_k, k_a, r_k, seg, tri, *, seq, ts=256):
    t, pc = p_rwkv.shape
    w = w0.shape[1]
    nt = t // ts
    row = lambda i: (i, 0)
    tok = pl.BlockSpec((ts, w), row)
    tok_shape = jax.ShapeDtypeStruct((t, w), F32)
    return pl.pallas_call(
        functools.partial(_rwkv_prep_kernel, seq_tiles=seq // ts),
        out_shape=[tok_shape] * 7 + [jax.ShapeDtypeStruct((nt, 8, w), F32), tok_shape, tok_shape],
        grid=(nt,),
        in_specs=[
            pl.BlockSpec((ts, pc), row),
            pl.BlockSpec((8, pc), lambda i: (jnp.maximum(i * (ts // 8) - 1, 0), 0)),
            _const_spec((1, pc)),
            _const_spec(wwa.shape),
            _const_spec(gup.shape),
            _const_spec((1, w)), _const_spec((1, w)), _const_spec((1, w)), _const_spec((1, w)),
            _const_spec((1, w)),
            _const_spec(seg.shape),
            _const_spec(tri.shape),
        ],
        out_specs=[tok] * 7 + [pl.BlockSpec((1, 8, w), lambda i: (i, 0, 0)), tok, tok],
        compiler_params=_cparams(("parallel",)),
        name="rwkv_prep",
    )(p_rwkv, p_rwkv, mix, wwa, gup, w0, a0, k_k, k_a, r_k, seg, tri)


PASSES_INTRA = 1
PASSES_SOLVE = 1
PASSES_APPLY = 1
PASSES_SCAN = 1


def _operand(x, passes):
    return _split2(x) if passes == 3 else (x.astype(BF16),)


def _prod(a, b, mm=_mm):
    if len(a) == 2 and len(b) == 2:
        return mm(a[0], b[0]) + (mm(a[0], b[1]) + mm(a[1], b[0]))
    return mm(a[0], b[0])


def _unit_lower_inverse(a_list, row, col, passes):
    eye = (row == col).astype(F32)
    same8 = (row // 8) == (col // 8)
    a8 = [jnp.where(same8, a, 0.0) for a in a_list]
    s8 = [_operand(x, passes) for x in a8]
    a8_2 = [_prod(s, s) for s in s8]
    s8_2 = [_operand(x, passes) for x in a8_2]
    a8_4 = [_prod(s, s) for s in s8_2]
    p = [eye + x + x2 + _prod(s, s2) for x, x2, s, s2 in zip(a8, a8_2, s8, s8_2)]
    t = [pp + _prod(_operand(pp, passes), _operand(x4, passes)) for pp, x4 in zip(p, a8_4)]
    m = 16
    while m <= CHUNK:
        sel = ((row // m) == (col // m)) & ((row // (m // 2)) != (col // (m // 2)))
        off = [_operand(jnp.where(sel, a, 0.0), passes) for a in a_list]
        ts = [_operand(x, passes) for x in t]
        mid = [_prod(s, o) for s, o in zip(ts, off)]
        t = [x + _prod(_operand(md, passes), s) for x, md, s in zip(t, mid, ts)]
        m *= 2
    return t


def _rwkv_intra_kernel(rt_ref, at_ref, kt_ref, bt_ref, kh_ref, bh_ref, v_ref, dc_ref,
                       rr_ref, o0_ref, gh_ref, *, heads, chunks_per_tile):
    n = RWKV_HEAD_DIM
    i = pl.program_id(0)
    row = _iota((CHUNK, CHUNK), 0)
    col = _iota((CHUNK, CHUNK), 1)
    strict = col < row
    incl = col <= row
    dc_all = dc_ref[0]
    dc_row = dc_all[0:1, :]
    for q in range(1, chunks_per_tile):
        dc_row = jnp.where(i % chunks_per_tile == q, dc_all[q:q + 1, :], dc_row)

    sls = [slice(h * n, (h + 1) * n) for h in range(heads)]
    at = [at_ref[:, s] for s in sls]
    rt = [rt_ref[:, s] for s in sls]
    x2 = [_operand(jnp.concatenate([bt_ref[:, s], kt_ref[:, s]], axis=0), PASSES_INTRA) for s in sls]
    m1 = [_prod(_operand(a, PASSES_INTRA), x, _mm_nt) for a, x in zip(at, x2)]
    m2 = [_prod(_operand(r, PASSES_INTRA), x, _mm_nt) for r, x in zip(rt, x2)]
    v_s = [_operand(v_ref[:, s], PASSES_APPLY) for s in sls]
    akrk = [jnp.concatenate([jnp.where(strict, a[:, CHUNK:], 0.0), jnp.where(incl, r[:, CHUNK:], 0.0)],
                            axis=0) for a, r in zip(m1, m2)]
    avv = [_prod(_operand(a, PASSES_APPLY), x) for a, x in zip(akrk, v_s)]
    tinv = _unit_lower_inverse([jnp.where(strict, a[:, :CHUNK], 0.0) for a in m1], row, col,
                               PASSES_SOLVE)
    rhs = [jnp.concatenate([a, w[:CHUNK]], axis=1) for a, w in zip(at, avv)]
    tz = [_prod(_operand(t, PASSES_APPLY), _operand(r, PASSES_APPLY)) for t, r in zip(tinv, rhs)]
    tz_s = [_operand(z, PASSES_APPLY) for z in tz]
    rz = [_prod(_operand(jnp.where(incl, r[:, :CHUNK], 0.0), PASSES_APPLY), z) for r, z in zip(m2, tz_s)]
    bz = [_prod(_operand(bh_ref[:, s], PASSES_APPLY), z, _mm_tn) for s, z in zip(sls, tz_s)]
    kv = [_prod(_operand(kh_ref[:, s], PASSES_APPLY), x, _mm_tn) for s, x in zip(sls, v_s)]
    for h, s in enumerate(sls):
        rr_ref[:, s] = rt[h] + rz[h][:, :n]
        o0_ref[:, s] = rz[h][:, n:] + avv[h][CHUNK:]
        dmat = jnp.where(row == col, jnp.broadcast_to(dc_row[:, s], (n, n)), 0.0)
        gh_ref[0, h] = bz[h] + jnp.concatenate([dmat, kv[h]], axis=1)


def _rwkv_scan_kernel(rr_ref, o0_ref, gh_ref, o_ref, state_ref, *, heads):
    n = RWKV_HEAD_DIM

    @pl.when(pl.program_id(1) == 0)
    def _():
        state_ref[...] = jnp.zeros_like(state_ref)

    sls = [slice(h * n, (h + 1) * n) for h in range(heads)]
    h0 = [_operand(state_ref[h], PASSES_SCAN) for h in range(heads)]
    outs = [_prod(_operand(rr_ref[:, s], PASSES_SCAN), x) + o0_ref[:, s] for s, x in zip(sls, h0)]
    new = [_prod(_operand(gh_ref[0, h, :, 0:n], PASSES_SCAN), h0[h]) + gh_ref[0, h, :, n:2 * n]
           for h in range(heads)]
    for h, s in enumerate(sls):
        o_ref[:, s] = outs[h]
        state_ref[h] = new[h]


def _rwkv_chunk(rt, at, kt, bt, kh, bh, v, dc, *, batch, seq, prep_ts):
    t, w = rt.shape
    n = RWKV_HEAD_DIM
    heads = w // n
    nchunk = seq // CHUNK
    cpt = prep_ts // CHUNK
    tok = pl.BlockSpec((CHUNK, w), lambda i: (i, 0))
    tok_shape = jax.ShapeDtypeStruct((t, w), F32)
    rr, o0, gh = pl.pallas_call(
        functools.partial(_rwkv_intra_kernel, heads=heads, chunks_per_tile=cpt),
        out_shape=[tok_shape, tok_shape, jax.ShapeDtypeStruct((t // CHUNK, heads, n, 2 * n), F32)],
        grid=(t // CHUNK,),
        in_specs=[tok] * 7 + [pl.BlockSpec((1, 8, w), lambda i: (i // cpt, 0, 0))],
        out_specs=[tok, tok, pl.BlockSpec((1, heads, n, 2 * n), lambda i: (i, 0, 0, 0))],
        compiler_params=_cparams(("parallel",)),
        name="rwkv_intra",
    )(rt, at, kt, bt, kh, bh, v, dc)
    tok2 = pl.BlockSpec((CHUNK, w), lambda b, c: (b * nchunk + c, 0))
    return pl.pallas_call(
        functools.partial(_rwkv_scan_kernel, heads=heads),
        out_shape=tok_shape,
        grid=(batch, nchunk),
        in_specs=[tok2, tok2, pl.BlockSpec((1, heads, n, 2 * n), lambda b, c: (b * nchunk + c, 0, 0, 0))],
        out_specs=tok2,
        scratch_shapes=[pltpu.VMEM((heads, n, n), F32)],
        compiler_params=_cparams(("parallel", "arbitrary")),
        name="rwkv_scan",
    )(rr, o0, gh)


def _rope_lanes(x, cos_t, sin_a, sin_b):
    width = x.shape[1]
    up = pltpu.roll(x, width - ROPE_HALF, axis=1)
    dn = pltpu.roll(x, ROPE_HALF, axis=1)
    return x * cos_t + up * sin_a + dn * sin_b


def _tile_lanes(tab, width):
    return jnp.concatenate([tab] * (width // tab.shape[1]), axis=1)


def _nsa_prep_kernel(p_ref, cos_ref, sa_ref, sb_ref, qn_ref, kn_ref, seg_ref,
                     q_ref, ks_ref, vs_ref, kw_ref, vw_ref, g_ref):
    dh = NSA_HEAD_DIM
    qw = q_ref.shape[1]
    kvw = NSA_KV_HEADS * dh
    seg = seg_ref[...]
    cos_t, sin_a, sin_b = cos_ref[...], sa_ref[...], sb_ref[...]

    def norm_rope(x, gain):
        wd = x.shape[1]
        ms = _dot_exact_rhs(x * x, seg[:wd, :wd]) * (1.0 / dh)
        y = x * lax.rsqrt(ms + NORM_EPS) * gain
        return _rope_lanes(y, _tile_lanes(cos_t, wd), _tile_lanes(sin_a, wd), _tile_lanes(sin_b, wd))

    q = norm_rope(p_ref[:, 0:qw], qn_ref[...])
    q_ref[...] = q * (dh ** -0.5)
    base = qw + 2 * kvw
    ks = norm_rope(p_ref[:, base:base + kvw], kn_ref[1:2, :]).astype(BF16)
    vs = p_ref[:, base + kvw:base + 2 * kvw].astype(BF16)
    kw = norm_rope(p_ref[:, base + 2 * kvw:base + 3 * kvw], kn_ref[2:3, :]).astype(BF16)
    vw = p_ref[:, base + 3 * kvw:base + 4 * kvw].astype(BF16)
    for h in range(NSA_KV_HEADS):
        sl = slice(h * dh, (h + 1) * dh)
        ks_ref[0, h] = ks[:, sl]
        vs_ref[0, h] = vs[:, sl]
        kw_ref[0, h] = kw[:, sl]
        vw_ref[0, h] = vw[:, sl]
    sig = _sigmoid(p_ref[:, base + 4 * kvw:base + 4 * kvw + LANE])
    sig_t = sig.T
    per_head = 3 * NSA_GROUP
    for h in range(NSA_KV_HEADS):
        g_ref[h] = sig_t[per_head * h:per_head * h + GATE_ROWS, :]


def _nsa_prep(p_nsa, cos_t, sin_a, sin_b, qn, kn, seg, *, batch, seq, ts=256):
    t, pc = p_nsa.shape
    qw = NSA_HEADS * NSA_HEAD_DIM
    st = seq // ts
    tab = pl.BlockSpec((ts, LANE), lambda i: (i % st, 0))
    hm = pl.BlockSpec((1, NSA_KV_HEADS, ts, NSA_HEAD_DIM), lambda i: (i // st, 0, i % st, 0))
    hm_shape = jax.ShapeDtypeStruct((batch, NSA_KV_HEADS, seq, NSA_HEAD_DIM), BF16)
    return pl.pallas_call(
        _nsa_prep_kernel,
        out_shape=[jax.ShapeDtypeStruct((t, qw), F32)] + [hm_shape] * 4
        + [jax.ShapeDtypeStruct((NSA_KV_HEADS, GATE_ROWS, t), F32)],
        grid=(t // ts,),
        in_specs=[pl.BlockSpec((ts, pc), lambda i: (i, 0)), tab, tab, tab,
                  _const_spec(qn.shape), _const_spec(kn.shape), _const_spec(seg.shape)],
        out_specs=[pl.BlockSpec((ts, qw), lambda i: (i, 0))] + [hm] * 4
        + [pl.BlockSpec((NSA_KV_HEADS, GATE_ROWS, ts), lambda i: (0, 0, i))],
        compiler_params=_cparams(("parallel",)),
        name="nsa_prep",
    )(p_nsa, cos_t, sin_a, sin_b, qn, kn, seg)


def _gelu_tanh(x):
    return 0.5 * x * (1.0 + jnp.tanh(np.sqrt(2.0 / np.pi).astype(np.float32) * (x + 0.044715 * (x * x * x))))


def _compress_kernel(gk_ref, gv_ref, pk_ref, pv_ref, k1_ref, k2_ref, v1_ref, v2_ref, kn_ref,
                     cos_ref, sin_ref, rot_ref, kc_ref, vc_ref):
    half = k1_ref.shape[0] // 2

    def mlp(g, pos, w1_ref, w2_ref):
        ya = _dot3(g, w1_ref[0:half, :])
        yb = _dot3(g, w1_ref[half:, :])
        bias = _dot3(jnp.broadcast_to(pos, (8, pos.shape[1])), w1_ref[...])[0:1, :]
        n = g.shape[0]
        hid = ya + pltpu.roll(yb, n - 1, axis=0) + bias
        return _dot3(_gelu_tanh(hid), w2_ref[...])

    kc = mlp(gk_ref[0, 0], pk_ref[...], k1_ref, k2_ref)
    ms = jnp.mean(kc * kc, axis=-1, keepdims=True)
    kc = kc * lax.rsqrt(ms + NORM_EPS) * kn_ref[0:1, :]
    kc_ref[0, 0] = kc * cos_ref[...] + _dot_exact_rhs(kc, rot_ref[...]) * sin_ref[...]
    vc_ref[0, 0] = mlp(gv_ref[0, 0], pv_ref[...], v1_ref, v2_ref)


def _nsa_compress(gk, gv, pk, pv, k1, k2, v1, v2, kn, cos_c, sin_c, rot):
    b, hk, ng, gw = gk.shape
    dh = NSA_HEAD_DIM
    grp = pl.BlockSpec((1, 1, ng, gw), lambda i, j: (i, j, 0, 0))
    out = pl.BlockSpec((1, 1, ng, dh), lambda i, j: (i, j, 0, 0))
    shape = jax.ShapeDtypeStruct((b, hk, ng, dh), F32)
    consts = [pk, pv, k1, k2, v1, v2, kn, cos_c, sin_c, rot]
    return pl.pallas_call(
        _compress_kernel,
        out_shape=[shape, shape],
        grid=(b, hk),
        in_specs=[grp, grp] + [_const_spec(c.shape) for c in consts],
        out_specs=[out, out],
        compiler_params=_cparams(("parallel", "parallel")),
        name="nsa_compress",
    )(gk, gv, *consts)


SLC_KEY_BLOCK = 256


def _nsa_attn_kernel(q_ref, kc_ref, vc_ref, ks_ref, vs_ref, kw_ref, vw_ref, g_ref, ovt_ref, ex_ref,
                     o_ref):
    dh = NSA_HEAD_DIM
    grp = NSA_GROUP
    qi = pl.program_id(2)
    tq = q_ref.shape[0]
    cols = grp * tq
    t0 = qi * tq
    ncmp = kc_ref.shape[2]
    nsel = ovt_ref.shape[0]
    kb = ex_ref.shape[1]
    span = WINDOW + tq

    q = q_ref[...]
    q4 = jnp.concatenate([q[:, g * dh:(g + 1) * dh] for g in range(grp)], axis=0)
    q4b = (q4 * LOG2_E).astype(BF16)

    def per_head(x):
        return jnp.concatenate([x] * grp, axis=1)

    st = _dot3(kc_ref[0, 0], q4, _mm_nt)
    tl = t0 + _iota((ncmp, cols), 1) % tq
    cmask = _iota((ncmp, cols), 0) * CMP_STRIDE + (CMP_BLOCK - 1) <= tl
    sm = jnp.where(cmask, st, MASKED)
    e = jnp.where(cmask, jnp.exp(sm - jnp.max(sm, axis=0, keepdims=True)), 0.0)
    pt = e / jnp.maximum(jnp.sum(e, axis=0, keepdims=True), 1e-30)
    o_cmp = _mm_tn(vc_ref[0, 0].astype(BF16), pt.astype(BF16))
    psum = pt[:, 0:tq]
    for g in range(1, grp):
        psum = psum + pt[:, g * tq:(g + 1) * tq]

    imp = _dot_exact_lhs(ovt_ref[...], psum)
    blk = _iota((nsel, tq), 0)
    cur = (t0 + _iota((nsel, tq), 1)) // SEL_BLOCK
    forced = (blk == 0) | (blk == cur) | (blk == cur - 1)
    imp = jnp.where(forced, jnp.inf, jnp.where(blk > cur, -jnp.inf, imp))
    rank = jnp.zeros((nsel, tq), jnp.int32)
    for m in range(nsel):
        im = imp[m:m + 1, :]
        ahead = (im > imp) | ((im == imp) & (m < blk))
        rank = rank + ahead.astype(jnp.int32)
    sel_bias = jnp.where(rank < min(SEL_TOP, nsel), 0.0, MASKED).astype(BF16)

    def slc_step(j, carry):
        m_run, l_run, acc = carry
        k0 = pl.multiple_of(j * kb, kb)
        s = _mm_nt(ks_ref[0, 0, pl.ds(k0, kb), :], q4b)
        bias = _mm(ex_ref[j], sel_bias)
        bias = jnp.where(k0 + _iota((kb, tq), 0) <= t0 + _iota((kb, tq), 1), bias, MASKED)
        s = s + per_head(bias)
        m_new = jnp.maximum(m_run, jnp.max(s, axis=0, keepdims=True))
        alpha = jnp.exp2(m_run - m_new)
        p = jnp.exp2(s - m_new)
        l_new = alpha * l_run + jnp.sum(p, axis=0, keepdims=True)
        acc = alpha * acc + _mm_tn(vs_ref[0, 0, pl.ds(k0, kb), :], p.astype(BF16))
        return m_new, l_new, acc

    init = (jnp.full((1, cols), MASKED, F32), jnp.zeros((1, cols), F32), jnp.zeros((dh, cols), F32))
    _, l_slc, acc_slc = lax.fori_loop(0, (t0 + tq + kb - 1) // kb, slc_step, init)
    o_slc = acc_slc / l_slc

    w0 = pl.multiple_of(jnp.maximum(t0 - WINDOW, 0), tq)
    kpos = w0 + _iota((span, tq), 0)
    tw = t0 + _iota((span, tq), 1)
    wbias = jnp.where((kpos <= tw) & (kpos > tw - WINDOW), 0.0, MASKED)
    s = _mm_nt(kw_ref[0, 0, pl.ds(w0, span), :], q4b) + per_head(wbias)
    p = jnp.exp2(s - jnp.max(s, axis=0, keepdims=True))
    o_win = (_mm_tn(vw_ref[0, 0, pl.ds(w0, span), :], p.astype(BF16))
             / jnp.sum(p, axis=0, keepdims=True))

    gates = g_ref[0]
    grow = [jnp.concatenate([gates[3 * g + br:3 * g + br + 1, :] for g in range(grp)], axis=1)
            for br in range(3)]
    o4 = grow[0] * o_cmp + grow[1] * o_slc + grow[2] * o_win
    o_ref[...] = jnp.concatenate([o4[:, g * tq:(g + 1) * tq] for g in range(grp)], axis=0)


def _nsa_attn(q, kc, vc, ks, vs, kw, vw, gates_t, overlap_t, expand_t, *, batch, seq, tq=256):
    t, qw = q.shape
    dh = NSA_HEAD_DIM
    gw = NSA_GROUP * dh
    st = seq // tq
    ncmp = kc.shape[2]
    cmp_spec = pl.BlockSpec((1, 1, ncmp, dh), lambda b, h, i: (b, h, 0, 0))
    kv_spec = pl.BlockSpec((1, 1, seq, dh), lambda b, h, i: (b, h, 0, 0))
    return pl.pallas_call(
        _nsa_attn_kernel,
        out_shape=jax.ShapeDtypeStruct((qw, t), F32),
        grid=(batch, NSA_KV_HEADS, st),
        in_specs=[
            pl.BlockSpec((tq, gw), lambda b, h, i: (b * st + i, h)),
            cmp_spec, cmp_spec, kv_spec, kv_spec, kv_spec, kv_spec,
            pl.BlockSpec((1, gates_t.shape[1], tq), lambda b, h, i: (h, 0, b * st + i)),
            _const_spec(overlap_t.shape), _const_spec(expand_t.shape),
        ],
        out_specs=pl.BlockSpec((gw, tq), lambda b, h, i: (h, b * st + i)),
        compiler_params=_cparams(("parallel", "parallel", "arbitrary")),
        name="nsa_attn",
    )(q, kc, vc, ks, vs, kw, vw, gates_t, overlap_t, expand_t)


def _merge_kernel(x_ref, o_ref, bonus_ref, gate_ref, ybt_ref, pg_ref, gnw_ref, gnb_ref, seg_ref,
                  ua_ref, ub_ref, wo_ref, out_ref):
    d = x_ref.shape[1]
    n = RWKV_HEAD_DIM
    seg = seg_ref[...]
    o = o_ref[...]
    mu = _dot_exact_rhs(o, seg) * (1.0 / n)
    dlt = o - mu
    var = _dot_exact_rhs(dlt * dlt, seg) * (1.0 / n)
    on = dlt * lax.rsqrt(var + GN_EPS) * gnw_ref[...] + gnb_ref[...]
    ya = ((on + bonus_ref[...]) * gate_ref[...]).astype(BF16)
    yb_t = ybt_ref[...].astype(BF16)
    merged = (_sigmoid(pg_ref[:, 0:d]) * _mm(ya, ua_ref[...])
              + _sigmoid(pg_ref[:, d:2 * d]) * _mm_tn(yb_t, ub_ref[...]))
    out_ref[...] = x_ref[...] + _mm(merged.astype(BF16), wo_ref[...])


def _merge(x, o_rwkv, bonus, gate, yb_t, pg, gnw, gnb, seg, ua, ub, wo, *, tm=256):
    t, d = x.shape
    w = o_rwkv.shape[1]
    row = lambda i: (i, 0)
    tokw = pl.BlockSpec((tm, w), row)
    return pl.pallas_call(
        _merge_kernel,
        out_shape=jax.ShapeDtypeStruct((t, d), F32),
        grid=(t // tm,),
        in_specs=[pl.BlockSpec((tm, d), row), tokw, tokw, tokw,
                  pl.BlockSpec((yb_t.shape[0], tm), lambda i: (0, i)),
                  pl.BlockSpec((tm, 2 * d), row),
                  _const_spec((1, w)), _const_spec((1, w)), _const_spec(seg.shape),
                  _const_spec(ua.shape), _const_spec(ub.shape), _const_spec(wo.shape)],
        out_specs=pl.BlockSpec((tm, d), row),
        compiler_params=_cparams(("parallel",)),
        name="merge",
    )(x, o_rwkv, bonus, gate, yb_t, pg, gnw, gnb, seg, ua, ub, wo)


def _block_diag_ones(width, block):
    idx = np.arange(width) // block
    return jnp.asarray(idx[:, None] == idx[None, :], BF16)


def _chunk_lower_ones(ts):
    i = np.arange(ts)
    return jnp.asarray((i[:, None] // CHUNK == i[None, :] // CHUNK) & (i[None, :] <= i[:, None]), BF16)


def _rope_tables(pos):
    inv = ROPE_THETA ** (-jnp.arange(ROPE_HALF, dtype=F32) / ROPE_HALF)
    ang = jnp.asarray(pos).astype(F32)[:, None] * inv[None, :]
    cos, sin = jnp.cos(ang), jnp.sin(ang)
    n = ang.shape[0]
    pad = jnp.zeros((n, NSA_HEAD_DIM - ROPE_DIM), F32)
    zero = jnp.zeros_like(sin)
    cos_h = jnp.concatenate([cos, cos, pad + 1.0], axis=1)
    sa_h = jnp.concatenate([-sin, zero, pad], axis=1)
    sb_h = jnp.concatenate([zero, sin, pad], axis=1)
    return cos_h, sa_h, sb_h


def _rot_half_matrix():
    r = np.zeros((NSA_HEAD_DIM, NSA_HEAD_DIM), np.float32)
    for l in range(ROPE_HALF):
        r[l + ROPE_HALF, l] = -1.0
        r[l, l + ROPE_HALF] = 1.0
    return jnp.asarray(r, BF16)


def _overlap_matrix_t(ncmp_pad, nsel):
    cs = np.arange(ncmp_pad)[None, :] * CMP_STRIDE
    ss = np.arange(nsel)[:, None] * SEL_BLOCK
    ov = np.clip(np.minimum(cs + CMP_BLOCK, ss + SEL_BLOCK) - np.maximum(cs, ss), 0, None) / CMP_BLOCK
    return jnp.asarray(ov, BF16)


def _expand_matrix_t(nsel, seq, kb):
    onehot = (np.arange(seq)[:, None] // SEL_BLOCK) == np.arange(nsel)[None, :]
    return jnp.asarray(onehot.reshape(seq // kb, kb, nsel), BF16)


def _pad_cols(x, width):
    return jnp.pad(x, ((0, 0), (0, width - x.shape[1])))


def _layer(x, l, ffn1_norm, ffn1_w_gate, ffn1_w_up, ffn1_w_down, mix_norm, w_in,
           rwkv_mix, rwkv_w0, rwkv_w_up, rwkv_a0, rwkv_a_up, rwkv_g_up,
           rwkv_k_k, rwkv_k_a, rwkv_r_k, rwkv_gn_w, rwkv_gn_b,
           nsa_q_norm, nsa_k_norm, cmp_pos_k, cmp_pos_v,
           cmp_k_w1, cmp_k_w2, cmp_v_w1, cmp_v_w2,
           w_branch_rwkv, w_branch_nsa, w_out,
           ffn2_norm, ffn2_w_gate, ffn2_w_up, ffn2_w_down, *, batch, seq):
    t, d = x.shape
    w = rwkv_w0.shape[1]
    dh = NSA_HEAD_DIM
    qw = NSA_HEADS * dh
    kvw = NSA_KV_HEADS * dh
    prep_ts = 256
    row = lambda v: v.reshape(1, -1)

    x = _ffn(x, row(ffn1_norm[l]), ffn1_w_gate[l].astype(BF16), ffn1_w_up[l].astype(BF16),
             ffn1_w_down[l].astype(BF16))

    wi = w_in[l]
    rwkv_cols = 3 * w + DECAY_LORA + ICLR_LORA + GATE_LORA
    rwkv_pad = 3 * w + 3 * LANE
    nsa_cols = qw + 6 * kvw + 3 * NSA_HEADS
    nsa_pad = qw + 6 * kvw + LANE
    g_mix = row(mix_norm[l])
    p_rwkv = _norm_proj(x, g_mix, _pad_cols(wi[:, :rwkv_cols], rwkv_pad).astype(BF16), name="proj_rwkv")
    p_nsa = _norm_proj(x, g_mix, _pad_cols(wi[:, rwkv_cols:rwkv_cols + nsa_cols], nsa_pad).astype(BF16),
                       name="proj_nsa")
    p_gate = _norm_proj(x, g_mix, wi[:, rwkv_cols + nsa_cols:].astype(BF16), name="proj_gate")

    wwa = jnp.zeros((LANE, 2 * w), F32)
    wwa = wwa.at[:DECAY_LORA, :w].set(rwkv_w_up[l]).at[DECAY_LORA:, w:].set(rwkv_a_up[l])
    gup = jnp.pad(rwkv_g_up[l], ((0, 2 * LANE - GATE_LORA), (0, 0)))
    seg_w = _block_diag_ones(w, RWKV_HEAD_DIM)
    (rt, at, kt, bt, kh, bh, v, dc, bonus, gate) = _rwkv_prep(
        p_rwkv, _pad_cols(row(rwkv_mix[l]), rwkv_pad), wwa, gup, row(rwkv_w0[l]), row(rwkv_a0[l]),
        row(rwkv_k_k[l]), row(rwkv_k_a[l]), row(rwkv_r_k[l]), seg_w, _chunk_lower_ones(prep_ts),
        seq=seq, ts=prep_ts)
    o_rwkv = _rwkv_chunk(rt, at, kt, bt, kh, bh, v, dc, batch=batch, seq=seq, prep_ts=prep_ts)

    cos_t, sin_a, sin_b = _rope_tables(np.arange(seq))
    two = lambda tab: jnp.concatenate([tab, tab], axis=1)
    qn = jnp.tile(row(nsa_q_norm[l]), (1, NSA_HEADS))
    kn = jnp.tile(nsa_k_norm[l], (1, NSA_KV_HEADS))
    q, ks, vs, kw, vw, gates = _nsa_prep(p_nsa, two(cos_t), two(sin_a), two(sin_b), qn, kn,
                                         _block_diag_ones(qw, dh), batch=batch, seq=seq)

    ngrp = seq // CMP_STRIDE
    grp_w = CMP_STRIDE * dh

    def groups(cols):
        g = cols.reshape(batch, ngrp, CMP_STRIDE, NSA_KV_HEADS, dh)
        return jnp.transpose(g, (0, 3, 1, 2, 4)).reshape(batch, NSA_KV_HEADS, ngrp, grp_w)

    cend = np.arange(ngrp) * CMP_STRIDE + CMP_BLOCK - 1
    cos_c, sa_c, sb_c = _rope_tables(cend)
    kc, vc = _nsa_compress(
        groups(p_nsa[:, qw:qw + kvw]), groups(p_nsa[:, qw + kvw:qw + 2 * kvw]),
        cmp_pos_k[l].reshape(1, -1), cmp_pos_v[l].reshape(1, -1),
        cmp_k_w1[l], cmp_k_w2[l], cmp_v_w1[l], cmp_v_w2[l], nsa_k_norm[l],
        cos_c, sb_c - sa_c, _rot_half_matrix())
    nsel = seq // SEL_BLOCK
    y_nsa = _nsa_attn(q, kc, vc, ks, vs, kw, vw, gates, _overlap_matrix_t(ngrp, nsel),
                      _expand_matrix_t(nsel, seq, SLC_KEY_BLOCK), batch=batch, seq=seq)

    x = _merge(x, o_rwkv, bonus, gate, y_nsa, p_gate, row(rwkv_gn_w[l]), row(rwkv_gn_b[l]), seg_w,
               w_branch_rwkv[l].astype(BF16), w_branch_nsa[l].astype(BF16), w_out[l].astype(BF16))
    return _ffn(x, row(ffn2_norm[l]), ffn2_w_gate[l].astype(BF16), ffn2_w_up[l].astype(BF16),
                ffn2_w_down[l].astype(BF16))


def kernel(x, ffn1_norm, ffn1_w_gate, ffn1_w_up, ffn1_w_down, mix_norm, w_in, rwkv_mix, rwkv_w0, rwkv_w_up, rwkv_a0, rwkv_a_up, rwkv_g_up, rwkv_k_k, rwkv_k_a, rwkv_r_k, rwkv_gn_w, rwkv_gn_b, nsa_q_norm, nsa_k_norm, cmp_pos_k, cmp_pos_v, cmp_k_w1, cmp_k_w2, cmp_v_w1, cmp_v_w2, w_branch_rwkv, w_branch_nsa, w_out, ffn2_norm, ffn2_w_gate, ffn2_w_up, ffn2_w_down):
    batch, seq, d = x.shape
    params = (ffn1_norm, ffn1_w_gate, ffn1_w_up, ffn1_w_down, mix_norm, w_in, rwkv_mix, rwkv_w0,
              rwkv_w_up, rwkv_a0, rwkv_a_up, rwkv_g_up, rwkv_k_k, rwkv_k_a, rwkv_r_k, rwkv_gn_w,
              rwkv_gn_b, nsa_q_norm, nsa_k_norm, cmp_pos_k, cmp_pos_v, cmp_k_w1, cmp_k_w2, cmp_v_w1,
              cmp_v_w2, w_branch_rwkv, w_branch_nsa, w_out, ffn2_norm, ffn2_w_gate, ffn2_w_up,
              ffn2_w_down)
    y = x.reshape(batch * seq, d)
    for l in range(ffn1_norm.shape[0]):
        y = _layer(y, l, *params, batch=batch, seq=seq)
    return y.reshape(batch, seq, d)
```

```python
import functools

import numpy as np
import jax
import jax.numpy as jnp
from jax import lax
from jax.experimental import pallas as pl
from jax.experimental.pallas import tpu as pltpu

F32 = jnp.float32
BF16 = jnp.bfloat16

RWKV_HEAD_DIM = 64
DECAY_LORA = 64
ICLR_LORA = 64
GATE_LORA = 160
GN_EPS = 64e-5
NSA_HEADS = 16
NSA_KV_HEADS = 4
NSA_GROUP = NSA_HEADS // NSA_KV_HEADS
NSA_HEAD_DIM = 64
ROPE_DIM = NSA_HEAD_DIM // 4
ROPE_HALF = ROPE_DIM // 2
ROPE_THETA = 500000.0
CMP_BLOCK = 32
CMP_STRIDE = 16
SEL_BLOCK = 64
SEL_TOP = 16
WINDOW = 512
NORM_EPS = 1e-6

LANE = 128
CHUNK = 64
VMEM_LIMIT = 56 * 1024 * 1024
MASKED = -1e30
LOG2_E = 1.4426950408889634
GATE_ROWS = 16


def _cparams(sem):
    return pltpu.CompilerParams(dimension_semantics=sem, vmem_limit_bytes=VMEM_LIMIT)


def _const_spec(shape):
    nd = len(shape)
    return pl.BlockSpec(shape, lambda *_: (0,) * nd, pipeline_mode=pl.Buffered(1))


def _mm(a, b):
    return lax.dot_general(a, b, (((1,), (0,)), ((), ())), preferred_element_type=F32)


def _mm_nt(a, b):
    return lax.dot_general(a, b, (((1,), (1,)), ((), ())), preferred_element_type=F32)


def _mm_tn(a, b):
    return lax.dot_general(a, b, (((0,), (0,)), ((), ())), preferred_element_type=F32)


def _split2(x):
    hi = x.astype(BF16)
    lo = (x - hi.astype(F32)).astype(BF16)
    return hi, lo


def _split3(x):
    h1 = x.astype(BF16)
    r1 = x - h1.astype(F32)
    h2 = r1.astype(BF16)
    h3 = (r1 - h2.astype(F32)).astype(BF16)
    return h1, h2, h3


def _dot3(a, b, mm=_mm):
    a1, a2 = _split2(a)
    b1, b2 = _split2(b)
    return mm(a1, b1) + (mm(a1, b2) + mm(a2, b1))


def _dot_exact_rhs(a, b_bf16):
    a1, a2 = _split2(a)
    return _mm(a1, b_bf16) + _mm(a2, b_bf16)


def _head_mean(x, seg_bf16, width):
    return _mm(x.astype(BF16), seg_bf16) * (1.0 / width)


def _dot_exact_lhs(a_bf16, b):
    b1, b2, b3 = _split3(b)
    return _mm(a_bf16, b1) + (_mm(a_bf16, b2) + _mm(a_bf16, b3))


def _sigmoid(x):
    return 1.0 / (1.0 + jnp.exp(-x))


def _iota(shape, dim):
    return lax.broadcasted_iota(jnp.int32, shape, dim)


def _ffn_kernel(x_ref, g_ref, wg_ref, wu_ref, wd_ref, o_ref, h_ref, acc_ref):
    j = pl.program_id(1)

    @pl.when(j == 0)
    def _():
        x = x_ref[...]
        ms = jnp.mean(x * x, axis=-1, keepdims=True)
        h_ref[...] = (x * lax.rsqrt(ms + NORM_EPS) * g_ref[...]).astype(BF16)
        acc_ref[...] = jnp.zeros_like(acc_ref)

    h = h_ref[...]
    gate = _mm(h, wg_ref[...])
    up = _mm(h, wu_ref[...])
    act = (gate * _sigmoid(gate) * up).astype(BF16)
    acc_ref[...] += _mm(act, wd_ref[...])

    @pl.when(j == pl.num_programs(1) - 1)
    def _():
        o_ref[...] = x_ref[...] + 0.5 * acc_ref[...]


def _ffn(x, g, wg, wu, wd, *, tm=512, tf=512):
    t, d = x.shape
    f = wg.shape[1]
    return pl.pallas_call(
        _ffn_kernel,
        out_shape=jax.ShapeDtypeStruct((t, d), F32),
        grid=(t // tm, f // tf),
        in_specs=[
            pl.BlockSpec((tm, d), lambda i, j: (i, 0)),
            pl.BlockSpec((1, d), lambda i, j: (0, 0)),
            pl.BlockSpec((d, tf), lambda i, j: (0, j)),
            pl.BlockSpec((d, tf), lambda i, j: (0, j)),
            pl.BlockSpec((tf, d), lambda i, j: (j, 0)),
        ],
        out_specs=pl.BlockSpec((tm, d), lambda i, j: (i, 0)),
        scratch_shapes=[pltpu.VMEM((tm, d), BF16), pltpu.VMEM((tm, d), F32)],
        compiler_params=_cparams(("parallel", "arbitrary")),
        name="ffn",
    )(x, g, wg, wu, wd)


def _norm_proj_kernel(x_ref, g_ref, w_ref, o_ref):
    x = x_ref[...]
    ms = jnp.mean(x * x, axis=-1, keepdims=True)
    h = (x * lax.rsqrt(ms + NORM_EPS) * g_ref[...]).astype(BF16)
    o_ref[...] = _mm(h, w_ref[...]).astype(o_ref.dtype)


def _norm_proj(x, g, w, *, tm=256, name="norm_proj"):
    t, d = x.shape
    n = w.shape[1]
    return pl.pallas_call(
        _norm_proj_kernel,
        out_shape=jax.ShapeDtypeStruct((t, n), F32),
        grid=(t // tm,),
        in_specs=[
            pl.BlockSpec((tm, d), lambda i: (i, 0)),
            _const_spec((1, d)),
            _const_spec((d, n)),
        ],
        out_specs=pl.BlockSpec((tm, n), lambda i: (i, 0)),
        compiler_params=_cparams(("parallel",)),
        name=name,
    )(x, g, w)


def _rwkv_prep_kernel(p_ref, prev_ref, mix_ref, wwa_ref, gup_ref, w0_ref, a0_ref, kk_ref, ka_ref,
                      rk_ref, seg_ref, tri_ref,
                      rt_ref, at_ref, kt_ref, bt_ref, kh_ref, bh_ref, v_ref, dc_ref, bonus_ref,
                      gate_ref, *, seq_tiles):
    i = pl.program_id(0)
    ts = p_ref.shape[0]
    w = rt_ref.shape[1]
    p = p_ref[...]
    prev = prev_ref[7:8, :]
    prev = jnp.where(i % seq_tiles == 0, jnp.zeros_like(prev), prev)
    shifted = pltpu.roll(p, 1, axis=0)
    shifted = jnp.where(_iota(p.shape, 0) == 0, prev, shifted)
    xs = p + mix_ref[...] * (shifted - p)

    r = xs[:, 0:w]
    k = xs[:, w:2 * w]
    v = xs[:, 2 * w:3 * w]
    lo = 3 * w
    pwa = xs[:, lo:lo + LANE]
    pg = xs[:, lo + LANE:lo + 3 * LANE]
    lane = _iota(pwa.shape, 1)
    z = jnp.where(lane < DECAY_LORA, jnp.tanh(pwa), pwa)
    wa = _dot3(z, wwa_ref[...])
    wl = w0_ref[...] + wa[:, :w]
    neg = -wl
    softplus = jnp.maximum(neg, 0.0) + jnp.log(1.0 + jnp.exp(-jnp.abs(neg)))
    lw = -jnp.exp(-softplus - 0.5)
    a = _sigmoid(a0_ref[...] + wa[:, w:])
    gate_ref[...] = _dot3(_sigmoid(pg), gup_ref[...])

    seg = seg_ref[...]
    kk = k * kk_ref[...]
    ss = _dot_exact_rhs(kk * kk, seg)
    kk = kk * lax.rsqrt(jnp.maximum(ss, 1e-24))
    k2 = k * (1.0 + (a - 1.0) * ka_ref[...])
    bonus_ref[...] = _dot_exact_rhs(r * k2 * rk_ref[...], seg) * v

    gc = _dot_exact_lhs(tri_ref[...], lw)
    nc = ts // CHUNK
    ends = [gc[(q + 1) * CHUNK - 1:(q + 1) * CHUNK, :] for q in range(nc)]
    gend = jnp.concatenate([jnp.broadcast_to(e, (CHUNK, w)) for e in ends], axis=0)
    to_end = jnp.exp(gend - gc)
    e_in = jnp.exp(gc)
    e_out = jnp.exp(-gc)
    b = kk * a
    rt_ref[...] = r * e_in
    at_ref[...] = -kk * jnp.exp(gc - lw)
    kt_ref[...] = k2 * e_out
    bt_ref[...] = b * e_out
    kh_ref[...] = k2 * to_end
    bh_ref[...] = b * to_end
    v_ref[...] = v
    dc_ref[0] = jnp.concatenate([jnp.exp(e) for e in ends] + [jnp.zeros((8 - nc, w), F32)], axis=0)


def _rwkv_prep(p_rwkv, mix, wwa, gup, w0, a0, k_k, k_a, r_k, seg, tri, *, seq, ts=256):
    t, pc = p_rwkv.shape
    w = w0.shape[1]
    nt = t // ts
    row = lambda i: (i, 0)
    tok = pl.BlockSpec((ts, w), row)
    tok_shape = jax.ShapeDtypeStruct((t, w), F32)
    return pl.pallas_call(
        functools.partial(_rwkv_prep_kernel, seq_tiles=seq // ts),
        out_shape=[tok_shape] * 7 + [jax.ShapeDtypeStruct((nt, 8, w), F32), tok_shape, tok_shape],
        grid=(nt,),
        in_specs=[
            pl.BlockSpec((ts, pc), row),
            pl.BlockSpec((8, pc), lambda i: (jnp.maximum(i * (ts // 8) - 1, 0), 0)),
            _const_spec((1, pc)),
            _const_spec(wwa.shape),
            _const_spec(gup.shape),
            _const_spec((1, w)), _const_spec((1, w)), _const_spec((1, w)), _const_spec((1, w)),
            _const_spec((1, w)),
            _const_spec(seg.shape),
            _const_spec(tri.shape),
        ],
        out_specs=[tok] * 7 + [pl.BlockSpec((1, 8, w), lambda i: (i, 0, 0)), tok, tok],
        compiler_params=_cparams(("parallel",)),
        name="rwkv_prep",
    )(p_rwkv, p_rwkv, mix, wwa, gup, w0, a0, k_k, k_a, r_k, seg, tri)


PASSES_SCAN = 1
QUAD = 4


def _operand(x, passes):
    return _split2(x) if passes == 3 else (x.astype(BF16),)


def _prod(a, b, mm=_mm):
    if len(a) == 2 and len(b) == 2:
        return mm(a[0], b[0]) + (mm(a[0], b[1]) + mm(a[1], b[0]))
    return mm(a[0], b[0])


def _block_diag(y, n):
    c = y.shape[0]
    tiled = jnp.concatenate([y] * (y.shape[1] // n), axis=0)
    keep = (_iota(tiled.shape, 0) // c) == (_iota(tiled.shape, 1) // n)
    return jnp.where(keep, tiled, jnp.zeros_like(tiled))


def _quad_mm(x, y, n, mm=_mm):
    return mm(x.astype(BF16), _block_diag(y.astype(BF16), n))


def _unit_lower_inverse(a_list, row, col):
    n = CHUNK
    eye = (row == col).astype(F32)
    same8 = (row // 8) == (col // 8)
    a8 = [jnp.where(same8, a, 0.0) for a in a_list]
    d8 = [_block_diag(x.astype(BF16), n) for x in a8]
    a8_2 = [_mm(x.astype(BF16), d) for x, d in zip(a8, d8)]
    d8_2 = [_block_diag(x.astype(BF16), n) for x in a8_2]
    a8_4 = [_mm(x.astype(BF16), d) for x, d in zip(a8_2, d8_2)]
    p = [eye + x + x2 + _mm(x.astype(BF16), d2) for x, x2, d2 in zip(a8, a8_2, d8_2)]
    t = [pp + _quad_mm(pp, x4, n) for pp, x4 in zip(p, a8_4)]
    m = 16
    while m <= CHUNK:
        sel = ((row // m) == (col // m)) & ((row // (m // 2)) != (col // (m // 2)))
        mid = [_quad_mm(x, jnp.where(sel, a, 0.0), n) for x, a in zip(t, a_list)]
        t = [x + _quad_mm(md, x, n) for x, md in zip(t, mid)]
        m *= 2
    return t


def _rwkv_intra_kernel(rt_ref, at_ref, kt_ref, bt_ref, kh_ref, bh_ref, v_ref, dc_ref,
                       rr_ref, o0_ref, gh_ref, *, heads, chunks_per_tile):
    n = RWKV_HEAD_DIM
    qw = QUAD * n
    i = pl.program_id(0)
    row = _iota((CHUNK, qw), 0)
    col = _iota((CHUNK, qw), 1) % n
    strict = col < row
    incl = col <= row
    dc_all = dc_ref[0]
    dc_row = dc_all[0:1, :]
    for q in range(1, chunks_per_tile):
        dc_row = jnp.where(i % chunks_per_tile == q, dc_all[q:q + 1, :], dc_row)

    quads = [slice(j * qw, (j + 1) * qw) for j in range(heads // QUAD)]
    at = [at_ref[:, s] for s in quads]
    rt = [rt_ref[:, s] for s in quads]
    v = [v_ref[:, s] for s in quads]
    bd_b = [_block_diag(bt_ref[:, s].astype(BF16), n) for s in quads]
    bd_k = [_block_diag(kt_ref[:, s].astype(BF16), n) for s in quads]
    ar = [jnp.concatenate([a, r], axis=0).astype(BF16) for a, r in zip(at, rt)]
    mb = [_mm_nt(x, d) for x, d in zip(ar, bd_b)]
    mk = [_mm_nt(x, d) for x, d in zip(ar, bd_k)]
    a_ab = [jnp.where(strict, m[:CHUNK], 0.0) for m in mb]
    a_rb = [jnp.where(incl, m[CHUNK:], 0.0) for m in mb]
    akrk = [jnp.concatenate([jnp.where(strict, m[:CHUNK], 0.0), jnp.where(incl, m[CHUNK:], 0.0)], axis=0)
            for m in mk]
    avv = [_quad_mm(x, y, n) for x, y in zip(akrk, v)]
    tinv = _unit_lower_inverse(a_ab, row, col)
    a_new = [_quad_mm(t, a, n) for t, a in zip(tinv, at)]
    u0 = [_quad_mm(t, w[:CHUNK], n) for t, w in zip(tinv, avv)]
    for j, s in enumerate(quads):
        rr_ref[:, s] = rt[j] + _quad_mm(a_rb[j], a_new[j], n)
        o0_ref[:, s] = _quad_mm(a_rb[j], u0[j], n) + avv[j][CHUNK:]
    eye = _iota((n, n), 0) == _iota((n, n), 1)
    gz, kv = [], []
    for h in range(heads):
        j, s = h // QUAD, slice((h % QUAD) * n, (h % QUAD + 1) * n)
        hs = slice(h * n, (h + 1) * n)
        z = jnp.concatenate([a_new[j][:, s], u0[j][:, s]], axis=1).astype(BF16)
        gz.append(_mm_tn(bh_ref[:, hs].astype(BF16), z))
        kv.append(_mm_tn(kh_ref[:, hs].astype(BF16), v[j][:, s].astype(BF16)))
    for h in range(heads):
        hs = slice(h * n, (h + 1) * n)
        dmat = jnp.where(eye, jnp.broadcast_to(dc_row[:, hs], (n, n)), 0.0)
        gh_ref[0, h] = gz[h] + jnp.concatenate([dmat, kv[h]], axis=1)


def _rwkv_scan_kernel(rr_ref, o0_ref, gh_ref, o_ref, state_ref, *, heads):
    n = RWKV_HEAD_DIM

    @pl.when(pl.program_id(1) == 0)
    def _():
        state_ref[...] = jnp.zeros_like(state_ref)

    sls = [slice(h * n, (h + 1) * n) for h in range(heads)]
    h0 = [_operand(state_ref[h], PASSES_SCAN) for h in range(heads)]
    outs = [_prod(_operand(rr_ref[:, s], PASSES_SCAN), x) + o0_ref[:, s] for s, x in zip(sls, h0)]
    new = [_prod(_operand(gh_ref[0, h, :, 0:n], PASSES_SCAN), h0[h]) + gh_ref[0, h, :, n:2 * n]
           for h in range(heads)]
    for h, s in enumerate(sls):
        o_ref[:, s] = outs[h]
        state_ref[h] = new[h]


def _rwkv_chunk(rt, at, kt, bt, kh, bh, v, dc, *, batch, seq, prep_ts):
    t, w = rt.shape
    n = RWKV_HEAD_DIM
    heads = w // n
    nchunk = seq // CHUNK
    cpt = prep_ts // CHUNK
    tok = pl.BlockSpec((CHUNK, w), lambda i: (i, 0))
    tok_shape = jax.ShapeDtypeStruct((t, w), F32)
    rr, o0, gh = pl.pallas_call(
        functools.partial(_rwkv_intra_kernel, heads=heads, chunks_per_tile=cpt),
        out_shape=[tok_shape, tok_shape, jax.ShapeDtypeStruct((t // CHUNK, heads, n, 2 * n), F32)],
        grid=(t // CHUNK,),
        in_specs=[tok] * 7 + [pl.BlockSpec((1, 8, w), lambda i: (i // cpt, 0, 0))],
        out_specs=[tok, tok, pl.BlockSpec((1, heads, n, 2 * n), lambda i: (i, 0, 0, 0))],
        compiler_params=_cparams(("parallel",)),
        name="rwkv_intra",
    )(rt, at, kt, bt, kh, bh, v, dc)
    tok2 = pl.BlockSpec((CHUNK, w), lambda b, c: (b * nchunk + c, 0))
    return pl.pallas_call(
        functools.partial(_rwkv_scan_kernel, heads=heads),
        out_shape=tok_shape,
        grid=(batch, nchunk),
        in_specs=[tok2, tok2, pl.BlockSpec((1, heads, n, 2 * n), lambda b, c: (b * nchunk + c, 0, 0, 0))],
        out_specs=tok2,
        scratch_shapes=[pltpu.VMEM((heads, n, n), F32)],
        compiler_params=_cparams(("parallel", "arbitrary")),
        name="rwkv_scan",
    )(rr, o0, gh)


def _rope_lanes(x, cos_t, sin_a, sin_b):
    width = x.shape[1]
    up = pltpu.roll(x, width - ROPE_HALF, axis=1)
    dn = pltpu.roll(x, ROPE_HALF, axis=1)
    return x * cos_t + up * sin_a + dn * sin_b


def _tile_lanes(tab, width):
    return jnp.concatenate([tab] * (width // tab.shape[1]), axis=1)


def _nsa_prep_kernel(p_ref, cos_ref, sa_ref, sb_ref, qn_ref, kn_ref, seg_ref,
                     q_ref, ks_ref, vs_ref, kw_ref, vw_ref, g_ref, *, seq_tiles):
    dh = NSA_HEAD_DIM
    qw = q_ref.shape[1]
    kvw = NSA_KV_HEADS * dh
    seg = seg_ref[...]
    cos_t, sin_a, sin_b = cos_ref[...], sa_ref[...], sb_ref[...]

    def norm_rope(x, gain):
        wd = x.shape[1]
        ms = _head_mean(x * x, seg[:wd, :wd], dh)
        y = x * lax.rsqrt(ms + NORM_EPS) * gain
        return _rope_lanes(y, _tile_lanes(cos_t, wd), _tile_lanes(sin_a, wd), _tile_lanes(sin_b, wd))

    q = norm_rope(p_ref[:, 0:qw], qn_ref[...])
    q_ref[...] = q * (dh ** -0.5)
    base = qw + 2 * kvw
    ks = norm_rope(p_ref[:, base:base + kvw], kn_ref[1:2, :])
    vs = p_ref[:, base + kvw:base + 2 * kvw].astype(BF16)
    kw = norm_rope(p_ref[:, base + 2 * kvw:base + 3 * kvw], kn_ref[2:3, :]).astype(BF16)
    vw = p_ref[:, base + 3 * kvw:base + 4 * kvw].astype(BF16)
    ts = p_ref.shape[0]
    tpos = (pl.program_id(0) % seq_tiles) * ts + _iota((ts, dh), 0)
    onehot = jnp.where(tpos // SEL_BLOCK == _iota((ts, dh), 1), 1.0, 0.0)
    for h in range(NSA_KV_HEADS):
        sl = slice(h * dh, (h + 1) * dh)
        ks_ref[0, h] = jnp.concatenate([ks[:, sl], onehot], axis=1).astype(BF16)
        vs_ref[0, h] = vs[:, sl]
        kw_ref[0, h] = kw[:, sl]
        vw_ref[0, h] = vw[:, sl]
    sig = _sigmoid(p_ref[:, base + 4 * kvw:base + 4 * kvw + LANE])
    sig_t = sig.T
    per_head = 3 * NSA_GROUP
    for h in range(NSA_KV_HEADS):
        g_ref[h] = sig_t[per_head * h:per_head * h + GATE_ROWS, :]


def _nsa_prep(p_nsa, cos_t, sin_a, sin_b, qn, kn, seg, *, batch, seq, ts=256):
    t, pc = p_nsa.shape
    qw = NSA_HEADS * NSA_HEAD_DIM
    st = seq // ts
    tab = pl.BlockSpec((ts, LANE), lambda i: (i % st, 0))
    hm = pl.BlockSpec((1, NSA_KV_HEADS, ts, NSA_HEAD_DIM), lambda i: (i // st, 0, i % st, 0))
    hm_shape = jax.ShapeDtypeStruct((batch, NSA_KV_HEADS, seq, NSA_HEAD_DIM), BF16)
    assert seq // SEL_BLOCK <= NSA_HEAD_DIM
    aug = pl.BlockSpec((1, NSA_KV_HEADS, ts, 2 * NSA_HEAD_DIM), lambda i: (i // st, 0, i % st, 0))
    aug_shape = jax.ShapeDtypeStruct((batch, NSA_KV_HEADS, seq, 2 * NSA_HEAD_DIM), BF16)
    return pl.pallas_call(
        functools.partial(_nsa_prep_kernel, seq_tiles=st),
        out_shape=[jax.ShapeDtypeStruct((t, qw), F32), aug_shape] + [hm_shape] * 3
        + [jax.ShapeDtypeStruct((NSA_KV_HEADS, GATE_ROWS, t), F32)],
        grid=(t // ts,),
        in_specs=[pl.BlockSpec((ts, pc), lambda i: (i, 0)), tab, tab, tab,
                  _const_spec(qn.shape), _const_spec(kn.shape), _const_spec(seg.shape)],
        out_specs=[pl.BlockSpec((ts, qw), lambda i: (i, 0)), aug] + [hm] * 3
        + [pl.BlockSpec((NSA_KV_HEADS, GATE_ROWS, ts), lambda i: (0, 0, i))],
        compiler_params=_cparams(("parallel",)),
        name="nsa_prep",
    )(p_nsa, cos_t, sin_a, sin_b, qn, kn, seg)


def _gelu_tanh(x):
    return 0.5 * x * (1.0 + jnp.tanh(np.sqrt(2.0 / np.pi).astype(np.float32) * (x + 0.044715 * (x * x * x))))


def _compress_kernel(gk_ref, gv_ref, pk_ref, pv_ref, k1_ref, k2_ref, v1_ref, v2_ref, kn_ref,
                     cos_ref, sin_ref, rot_ref, kc_ref, vc_ref):
    half = k1_ref.shape[0] // 2

    def mlp(g, pos, w1_ref, w2_ref):
        ya = _dot3(g, w1_ref[0:half, :])
        yb = _dot3(g, w1_ref[half:, :])
        bias = _dot3(jnp.broadcast_to(pos, (8, pos.shape[1])), w1_ref[...])[0:1, :]
        n = g.shape[0]
        hid = ya + pltpu.roll(yb, n - 1, axis=0) + bias
        return _dot3(_gelu_tanh(hid), w2_ref[...])

    kc = mlp(gk_ref[0, 0], pk_ref[...], k1_ref, k2_ref)
    ms = jnp.mean(kc * kc, axis=-1, keepdims=True)
    kc = kc * lax.rsqrt(ms + NORM_EPS) * kn_ref[0:1, :]
    kc_ref[0, 0] = kc * cos_ref[...] + _dot_exact_rhs(kc, rot_ref[...]) * sin_ref[...]
    vc_ref[0, 0] = mlp(gv_ref[0, 0], pv_ref[...], v1_ref, v2_ref)


def _nsa_compress(gk, gv, pk, pv, k1, k2, v1, v2, kn, cos_c, sin_c, rot):
    b, hk, ng, gw = gk.shape
    dh = NSA_HEAD_DIM
    grp = pl.BlockSpec((1, 1, ng, gw), lambda i, j: (i, j, 0, 0))
    out = pl.BlockSpec((1, 1, ng, dh), lambda i, j: (i, j, 0, 0))
    shape = jax.ShapeDtypeStruct((b, hk, ng, dh), F32)
    consts = [pk, pv, k1, k2, v1, v2, kn, cos_c, sin_c, rot]
    return pl.pallas_call(
        _compress_kernel,
        out_shape=[shape, shape],
        grid=(b, hk),
        in_specs=[grp, grp] + [_const_spec(c.shape) for c in consts],
        out_specs=[out, out],
        compiler_params=_cparams(("parallel", "parallel")),
        name="nsa_compress",
    )(gk, gv, *consts)


def _nsa_attn_kernel(q_ref, kc_ref, vc_ref, ks_ref, vs_ref, kw_ref, vw_ref, g_ref, ovt_ref, o_ref):
    dh = NSA_HEAD_DIM
    grp = NSA_GROUP
    qi = pl.program_id(2)
    tq = q_ref.shape[0]
    cols = grp * tq
    t0 = qi * tq
    ncmp = kc_ref.shape[2]
    nsel = ovt_ref.shape[0]
    kb = tq
    span = WINDOW + tq

    q = q_ref[...]
    q4 = jnp.concatenate([q[:, g * dh:(g + 1) * dh] for g in range(grp)], axis=0)
    q4b = (q4 * LOG2_E).astype(BF16)

    def per_head(x):
        return jnp.concatenate([x] * grp, axis=1)

    st = _dot3(kc_ref[0, 0], q4, _mm_nt)
    tl = t0 + _iota((ncmp, cols), 1) % tq
    cmask = _iota((ncmp, cols), 0) * CMP_STRIDE + (CMP_BLOCK - 1) <= tl
    sm = jnp.where(cmask, st, MASKED)
    e = jnp.where(cmask, jnp.exp(sm - jnp.max(sm, axis=0, keepdims=True)), 0.0)
    pt = e / jnp.maximum(jnp.sum(e, axis=0, keepdims=True), 1e-30)
    o_cmp = _mm_tn(vc_ref[0, 0].astype(BF16), pt.astype(BF16))
    psum = pt[:, 0:tq]
    for g in range(1, grp):
        psum = psum + pt[:, g * tq:(g + 1) * tq]

    imp = _dot_exact_lhs(ovt_ref[...], psum)
    blk = _iota((nsel, tq), 0)
    cur = (t0 + _iota((nsel, tq), 1)) // SEL_BLOCK
    forced = (blk == 0) | (blk == cur) | (blk == cur - 1)
    imp = jnp.where(forced, jnp.inf, jnp.where(blk > cur, -jnp.inf, imp))
    rank = jnp.zeros((nsel, tq), jnp.int32)
    for m in range(nsel):
        im = imp[m:m + 1, :]
        ahead = (im > imp) | ((im == imp) & (m < blk))
        rank = rank + ahead.astype(jnp.int32)
    sel_bias = jnp.where(rank < min(SEL_TOP, nsel), 0.0, MASKED).astype(BF16)

    eye = jnp.where(_iota((nsel, dh), 0) == _iota((nsel, dh), 1), 1.0, 0.0).astype(BF16)
    bias_q = _mm_tn(sel_bias, eye)
    q_aug = jnp.concatenate([q4 * LOG2_E, jnp.concatenate([bias_q] * grp, axis=0)],
                            axis=1).astype(BF16)

    def online_softmax(s, v_blk, m_run, l_run, acc):
        m_new = jnp.maximum(m_run, jnp.max(s, axis=0, keepdims=True))
        alpha = jnp.exp2(m_run - m_new)
        p = jnp.exp2(s - m_new)
        l_new = alpha * l_run + jnp.sum(p, axis=0, keepdims=True)
        return m_new, l_new, alpha * acc + _mm_tn(v_blk, p.astype(BF16))

    def slc_block(k0, state, bias=None):
        rows = pl.ds(pl.multiple_of(k0, kb), kb)
        s = _mm_nt(ks_ref[0, 0, rows, :], q_aug)
        return online_softmax(s if bias is None else s + bias, vs_ref[0, 0, rows, :], *state)

    init = (jnp.full((1, cols), MASKED, F32), jnp.zeros((1, cols), F32), jnp.zeros((dh, cols), F32))
    state = lax.fori_loop(0, qi, lambda j, st: slc_block(j * kb, st), init)
    causal = jnp.where(_iota((kb, tq), 0) <= _iota((kb, tq), 1), 0.0, MASKED)
    _, l_slc, acc_slc = slc_block(t0, state, per_head(causal))
    o_slc = acc_slc / l_slc

    w0 = pl.multiple_of(jnp.maximum(t0 - WINDOW, 0), tq)
    kpos = w0 + _iota((span, tq), 0)
    tw = t0 + _iota((span, tq), 1)
    wbias = jnp.where((kpos <= tw) & (kpos > tw - WINDOW), 0.0, MASKED)
    s = _mm_nt(kw_ref[0, 0, pl.ds(w0, span), :], q4b) + per_head(wbias)
    p = jnp.exp2(s - jnp.max(s, axis=0, keepdims=True))
    o_win = (_mm_tn(vw_ref[0, 0, pl.ds(w0, span), :], p.astype(BF16))
             / jnp.sum(p, axis=0, keepdims=True))

    gates = g_ref[0]
    grow = [jnp.concatenate([gates[3 * g + br:3 * g + br + 1, :] for g in range(grp)], axis=1)
            for br in range(3)]
    o4 = grow[0] * o_cmp + grow[1] * o_slc + grow[2] * o_win
    o_ref[...] = jnp.concatenate([o4[:, g * tq:(g + 1) * tq] for g in range(grp)], axis=0)


def _nsa_attn(q, kc, vc, ks_aug, vs, kw, vw, gates_t, overlap_t, *, batch, seq, tq=256):
    t, qw = q.shape
    dh = NSA_HEAD_DIM
    gw = NSA_GROUP * dh
    st = seq // tq
    ncmp = kc.shape[2]
    cmp_spec = pl.BlockSpec((1, 1, ncmp, dh), lambda b, h, i: (b, h, 0, 0))
    kv_spec = pl.BlockSpec((1, 1, seq, dh), lambda b, h, i: (b, h, 0, 0))
    aug_spec = pl.BlockSpec((1, 1, seq, 2 * dh), lambda b, h, i: (b, h, 0, 0))
    return pl.pallas_call(
        _nsa_attn_kernel,
        out_shape=jax.ShapeDtypeStruct((qw, t), F32),
        grid=(batch, NSA_KV_HEADS, st),
        in_specs=[
            pl.BlockSpec((tq, gw), lambda b, h, i: (b * st + i, h)),
            cmp_spec, cmp_spec, aug_spec, kv_spec, kv_spec, kv_spec,
            pl.BlockSpec((1, gates_t.shape[1], tq), lambda b, h, i: (h, 0, b * st + i)),
            _const_spec(overlap_t.shape),
        ],
        out_specs=pl.BlockSpec((gw, tq), lambda b, h, i: (h, b * st + i)),
        compiler_params=_cparams(("parallel", "parallel", "arbitrary")),
        name="nsa_attn",
    )(q, kc, vc, ks_aug, vs, kw, vw, gates_t, overlap_t)


def _merge_kernel(x_ref, o_ref, bonus_ref, gate_ref, ybt_ref, pg_ref, gnw_ref, gnb_ref, seg_ref,
                  ua_ref, ub_ref, wo_ref, out_ref):
    d = x_ref.shape[1]
    n = RWKV_HEAD_DIM
    seg = seg_ref[...]
    o = o_ref[...]
    mu = _dot_exact_rhs(o, seg) * (1.0 / n)
    dlt = o - mu
    var = _head_mean(dlt * dlt, seg, n)
    on = dlt * lax.rsqrt(var + GN_EPS) * gnw_ref[...] + gnb_ref[...]
    ya = ((on + bonus_ref[...]) * gate_ref[...]).astype(BF16)
    yb_t = ybt_ref[...].astype(BF16)
    merged = (_sigmoid(pg_ref[:, 0:d]) * _mm(ya, ua_ref[...])
              + _sigmoid(pg_ref[:, d:2 * d]) * _mm_tn(yb_t, ub_ref[...]))
    out_ref[...] = x_ref[...] + _mm(merged.astype(BF16), wo_ref[...])


def _merge(x, o_rwkv, bonus, gate, yb_t, pg, gnw, gnb, seg, ua, ub, wo, *, tm=256):
    t, d = x.shape
    w = o_rwkv.shape[1]
    row = lambda i: (i, 0)
    tokw = pl.BlockSpec((tm, w), row)
    return pl.pallas_call(
        _merge_kernel,
        out_shape=jax.ShapeDtypeStruct((t, d), F32),
        grid=(t // tm,),
        in_specs=[pl.BlockSpec((tm, d), row), tokw, tokw, tokw,
                  pl.BlockSpec((yb_t.shape[0], tm), lambda i: (0, i)),
                  pl.BlockSpec((tm, 2 * d), row),
                  _const_spec((1, w)), _const_spec((1, w)), _const_spec(seg.shape),
                  _const_spec(ua.shape), _const_spec(ub.shape), _const_spec(wo.shape)],
        out_specs=pl.BlockSpec((tm, d), row),
        compiler_params=_cparams(("parallel",)),
        name="merge",
    )(x, o_rwkv, bonus, gate, yb_t, pg, gnw, gnb, seg, ua, ub, wo)


def _block_diag_ones(width, block):
    idx = np.arange(width) // block
    return jnp.asarray(idx[:, None] == idx[None, :], BF16)


def _chunk_lower_ones(ts):
    i = np.arange(ts)
    return jnp.asarray((i[:, None] // CHUNK == i[None, :] // CHUNK) & (i[None, :] <= i[:, None]), BF16)


def _rope_tables(pos):
    inv = ROPE_THETA ** (-jnp.arange(ROPE_HALF, dtype=F32) / ROPE_HALF)
    ang = jnp.asarray(pos).astype(F32)[:, None] * inv[None, :]
    cos, sin = jnp.cos(ang), jnp.sin(ang)
    n = ang.shape[0]
    pad = jnp.zeros((n, NSA_HEAD_DIM - ROPE_DIM), F32)
    zero = jnp.zeros_like(sin)
    cos_h = jnp.concatenate([cos, cos, pad + 1.0], axis=1)
    sa_h = jnp.concatenate([-sin, zero, pad], axis=1)
    sb_h = jnp.concatenate([zero, sin, pad], axis=1)
    return cos_h, sa_h, sb_h


def _rot_half_matrix():
    r = np.zeros((NSA_HEAD_DIM, NSA_HEAD_DIM), np.float32)
    for l in range(ROPE_HALF):
        r[l + ROPE_HALF, l] = -1.0
        r[l, l + ROPE_HALF] = 1.0
    return jnp.asarray(r, BF16)


def _overlap_matrix_t(ncmp_pad, nsel):
    cs = np.arange(ncmp_pad)[None, :] * CMP_STRIDE
    ss = np.arange(nsel)[:, None] * SEL_BLOCK
    ov = np.clip(np.minimum(cs + CMP_BLOCK, ss + SEL_BLOCK) - np.maximum(cs, ss), 0, None) / CMP_BLOCK
    return jnp.asarray(ov, BF16)


def _pad_cols(x, width):
    return jnp.pad(x, ((0, 0), (0, width - x.shape[1])))


def _layer(x, l, ffn1_norm, ffn1_w_gate, ffn1_w_up, ffn1_w_down, mix_norm, w_in,
           rwkv_mix, rwkv_w0, rwkv_w_up, rwkv_a0, rwkv_a_up, rwkv_g_up,
           rwkv_k_k, rwkv_k_a, rwkv_r_k, rwkv_gn_w, rwkv_gn_b,
           nsa_q_norm, nsa_k_norm, cmp_pos_k, cmp_pos_v,
           cmp_k_w1, cmp_k_w2, cmp_v_w1, cmp_v_w2,
           w_branch_rwkv, w_branch_nsa, w_out,
           ffn2_norm, ffn2_w_gate, ffn2_w_up, ffn2_w_down, *, batch, seq):
    t, d = x.shape
    w = rwkv_w0.shape[1]
    dh = NSA_HEAD_DIM
    qw = NSA_HEADS * dh
    kvw = NSA_KV_HEADS * dh
    prep_ts = 256
    row = lambda v: v.reshape(1, -1)

    x = _ffn(x, row(ffn1_norm[l]), ffn1_w_gate[l].astype(BF16), ffn1_w_up[l].astype(BF16),
             ffn1_w_down[l].astype(BF16))

    wi = w_in[l]
    rwkv_cols = 3 * w + DECAY_LORA + ICLR_LORA + GATE_LORA
    rwkv_pad = 3 * w + 3 * LANE
    nsa_cols = qw + 6 * kvw + 3 * NSA_HEADS
    nsa_pad = qw + 6 * kvw + LANE
    g_mix = row(mix_norm[l])
    p_rwkv = _norm_proj(x, g_mix, _pad_cols(wi[:, :rwkv_cols], rwkv_pad).astype(BF16), name="proj_rwkv")
    p_nsa = _norm_proj(x, g_mix, _pad_cols(wi[:, rwkv_cols:rwkv_cols + nsa_cols], nsa_pad).astype(BF16),
                       name="proj_nsa")
    p_gate = _norm_proj(x, g_mix, wi[:, rwkv_cols + nsa_cols:].astype(BF16), name="proj_gate")

    wwa = jnp.zeros((LANE, 2 * w), F32)
    wwa = wwa.at[:DECAY_LORA, :w].set(rwkv_w_up[l]).at[DECAY_LORA:, w:].set(rwkv_a_up[l])
    gup = jnp.pad(rwkv_g_up[l], ((0, 2 * LANE - GATE_LORA), (0, 0)))
    seg_w = _block_diag_ones(w, RWKV_HEAD_DIM)
    (rt, at, kt, bt, kh, bh, v, dc, bonus, gate) = _rwkv_prep(
        p_rwkv, _pad_cols(row(rwkv_mix[l]), rwkv_pad), wwa, gup, row(rwkv_w0[l]), row(rwkv_a0[l]),
        row(rwkv_k_k[l]), row(rwkv_k_a[l]), row(rwkv_r_k[l]), seg_w, _chunk_lower_ones(prep_ts),
        seq=seq, ts=prep_ts)
    o_rwkv = _rwkv_chunk(rt, at, kt, bt, kh, bh, v, dc, batch=batch, seq=seq, prep_ts=prep_ts)

    cos_t, sin_a, sin_b = _rope_tables(np.arange(seq))
    two = lambda tab: jnp.concatenate([tab, tab], axis=1)
    qn = jnp.tile(row(nsa_q_norm[l]), (1, NSA_HEADS))
    kn = jnp.tile(nsa_k_norm[l], (1, NSA_KV_HEADS))
    q, ks, vs, kw, vw, gates = _nsa_prep(p_nsa, two(cos_t), two(sin_a), two(sin_b), qn, kn,
                                         _block_diag_ones(qw, dh), batch=batch, seq=seq)

    ngrp = seq // CMP_STRIDE
    grp_w = CMP_STRIDE * dh

    def groups(cols):
        g = cols.reshape(batch, ngrp, CMP_STRIDE, NSA_KV_HEADS, dh)
        return jnp.transpose(g, (0, 3, 1, 2, 4)).reshape(batch, NSA_KV_HEADS, ngrp, grp_w)

    cend = np.arange(ngrp) * CMP_STRIDE + CMP_BLOCK - 1
    cos_c, sa_c, sb_c = _rope_tables(cend)
    kc, vc = _nsa_compress(
        groups(p_nsa[:, qw:qw + kvw]), groups(p_nsa[:, qw + kvw:qw + 2 * kvw]),
        cmp_pos_k[l].reshape(1, -1), cmp_pos_v[l].reshape(1, -1),
        cmp_k_w1[l], cmp_k_w2[l], cmp_v_w1[l], cmp_v_w2[l], nsa_k_norm[l],
        cos_c, sb_c - sa_c, _rot_half_matrix())
    nsel = seq // SEL_BLOCK
    y_nsa = _nsa_attn(q, kc, vc, ks, vs, kw, vw, gates, _overlap_matrix_t(ngrp, nsel),
                      batch=batch, seq=seq)

    x = _merge(x, o_rwkv, bonus, gate, y_nsa, p_gate, row(rwkv_gn_w[l]), row(rwkv_gn_b[l]), seg_w,
               w_branch_rwkv[l].astype(BF16), w_branch_nsa[l].astype(BF16), w_out[l].astype(BF16))
    return _ffn(x, row(ffn2_norm[l]), ffn2_w_gate[l].astype(BF16), ffn2_w_up[l].astype(BF16),
                ffn2_w_down[l].astype(BF16))


def kernel(x, ffn1_norm, ffn1_w_gate, ffn1_w_up, ffn1_w_down, mix_norm, w_in, rwkv_mix, rwkv_w0, rwkv_w_up, rwkv_a0, rwkv_a_up, rwkv_g_up, rwkv_k_k, rwkv_k_a, rwkv_r_k, rwkv_gn_w, rwkv_gn_b, nsa_q_norm, nsa_k_norm, cmp_pos_k, cmp_pos_v, cmp_k_w1, cmp_k_w2, cmp_v_w1, cmp_v_w2, w_branch_rwkv, w_branch_nsa, w_out, ffn2_norm, ffn2_w_gate, ffn2_w_up, ffn2_w_down):
    batch, seq, d = x.shape
    params = (ffn1_norm, ffn1_w_gate, ffn1_w_up, ffn1_w_down, mix_norm, w_in, rwkv_mix, rwkv_w0,
              rwkv_w_up, rwkv_a0, rwkv_a_up, rwkv_g_up, rwkv_k_k, rwkv_k_a, rwkv_r_k, rwkv_gn_w,
              rwkv_gn_b, nsa_q_norm, nsa_k_norm, cmp_pos_k, cmp_pos_v, cmp_k_w1, cmp_k_w2, cmp_v_w1,
              cmp_v_w2, w_branch_rwkv, w_branch_nsa, w_out, ffn2_norm, ffn2_w_gate, ffn2_w_up,
              ffn2_w_down)
    y = x.reshape(batch * seq, d)
    for l in range(ffn1_norm.shape[0]):
        y = _layer(y, l, *params, batch=batch, seq=seq)
    return y.reshape(batch, seq, d)
```

```python
import functools

import numpy as np
import jax
import jax.numpy as jnp
from jax import lax
from jax.experimental import pallas as pl
from jax.experimental.pallas import tpu as pltpu

F32 = jnp.float32
BF16 = jnp.bfloat16

RWKV_HEAD_DIM = 64
DECAY_LORA = 64
ICLR_LORA = 64
GATE_LORA = 160
GN_EPS = 64e-5
NSA_HEADS = 16
NSA_KV_HEADS = 4
NSA_GROUP = NSA_HEADS // NSA_KV_HEADS
NSA_HEAD_DIM = 64
ROPE_DIM = NSA_HEAD_DIM // 4
ROPE_HALF = ROPE_DIM // 2
ROPE_THETA = 500000.0
CMP_BLOCK = 32
CMP_STRIDE = 16
SEL_BLOCK = 64
SEL_TOP = 16
WINDOW = 512
NORM_EPS = 1e-6

LANE = 128
CHUNK = 64
VMEM_LIMIT = 56 * 1024 * 1024
MASKED = -1e30
LOG2_E = 1.4426950408889634
GATE_ROWS = 16
ATTN_KEY_BLOCK = 256
ATTN_KV_PER_STEP = 4


def _cparams(sem):
    return pltpu.CompilerParams(dimension_semantics=sem, vmem_limit_bytes=VMEM_LIMIT)


def _const_spec(shape):
    nd = len(shape)
    return pl.BlockSpec(shape, lambda *_: (0,) * nd, pipeline_mode=pl.Buffered(1))


def _mm(a, b):
    return lax.dot_general(a, b, (((1,), (0,)), ((), ())), preferred_element_type=F32)


def _mm_nt(a, b):
    return lax.dot_general(a, b, (((1,), (1,)), ((), ())), preferred_element_type=F32)


def _mm_tn(a, b):
    return lax.dot_general(a, b, (((0,), (0,)), ((), ())), preferred_element_type=F32)


def _split2(x):
    hi = x.astype(BF16)
    lo = (x - hi.astype(F32)).astype(BF16)
    return hi, lo


def _split3(x):
    h1 = x.astype(BF16)
    r1 = x - h1.astype(F32)
    h2 = r1.astype(BF16)
    h3 = (r1 - h2.astype(F32)).astype(BF16)
    return h1, h2, h3


def _dot3(a, b, mm=_mm):
    a1, a2 = _split2(a)
    b1, b2 = _split2(b)
    return mm(a1, b1) + (mm(a1, b2) + mm(a2, b1))


def _dot_exact_rhs(a, b_bf16):
    a1, a2 = _split2(a)
    return _mm(a1, b_bf16) + _mm(a2, b_bf16)


def _head_mean(x, seg_bf16, width):
    return _mm(x.astype(BF16), seg_bf16) * (1.0 / width)


def _dot_exact_lhs(a_bf16, b):
    b1, b2, b3 = _split3(b)
    return _mm(a_bf16, b1) + (_mm(a_bf16, b2) + _mm(a_bf16, b3))


def _sigmoid(x):
    return 1.0 / (1.0 + jnp.exp(-x))


def _iota(shape, dim):
    return lax.broadcasted_iota(jnp.int32, shape, dim)


def _ffn_kernel(x_ref, g_ref, wg_ref, wu_ref, wd_ref, o_ref, h_ref, acc_ref):
    j = pl.program_id(1)

    @pl.when(j == 0)
    def _():
        x = x_ref[...]
        ms = jnp.mean(x * x, axis=-1, keepdims=True)
        h_ref[...] = (x * lax.rsqrt(ms + NORM_EPS) * g_ref[...]).astype(BF16)
        acc_ref[...] = jnp.zeros_like(acc_ref)

    h = h_ref[...]
    gate = _mm(h, wg_ref[...])
    up = _mm(h, wu_ref[...])
    act = (gate * _sigmoid(gate) * up).astype(BF16)
    acc_ref[...] += _mm(act, wd_ref[...])

    @pl.when(j == pl.num_programs(1) - 1)
    def _():
        o_ref[...] = x_ref[...] + 0.5 * acc_ref[...]


def _ffn(x, g, wg, wu, wd, *, tm=512, tf=512):
    t, d = x.shape
    f = wg.shape[1]
    return pl.pallas_call(
        _ffn_kernel,
        out_shape=jax.ShapeDtypeStruct((t, d), F32),
        grid=(t // tm, f // tf),
        in_specs=[
            pl.BlockSpec((tm, d), lambda i, j: (i, 0)),
            pl.BlockSpec((1, d), lambda i, j: (0, 0)),
            pl.BlockSpec((d, tf), lambda i, j: (0, j)),
            pl.BlockSpec((d, tf), lambda i, j: (0, j)),
            pl.BlockSpec((tf, d), lambda i, j: (j, 0)),
        ],
        out_specs=pl.BlockSpec((tm, d), lambda i, j: (i, 0)),
        scratch_shapes=[pltpu.VMEM((tm, d), BF16), pltpu.VMEM((tm, d), F32)],
        compiler_params=_cparams(("parallel", "arbitrary")),
        name="ffn",
    )(x, g, wg, wu, wd)


def _norm_proj_kernel(x_ref, g_ref, w_ref, o_ref):
    x = x_ref[...]
    ms = jnp.mean(x * x, axis=-1, keepdims=True)
    h = (x * lax.rsqrt(ms + NORM_EPS) * g_ref[...]).astype(BF16)
    o_ref[...] = _mm(h, w_ref[...]).astype(o_ref.dtype)


def _norm_proj(x, g, w, *, tm=256, name="norm_proj"):
    t, d = x.shape
    n = w.shape[1]
    return pl.pallas_call(
        _norm_proj_kernel,
        out_shape=jax.ShapeDtypeStruct((t, n), F32),
        grid=(t // tm,),
        in_specs=[
            pl.BlockSpec((tm, d), lambda i: (i, 0)),
            _const_spec((1, d)),
            _const_spec((d, n)),
        ],
        out_specs=pl.BlockSpec((tm, n), lambda i: (i, 0)),
        compiler_params=_cparams(("parallel",)),
        name=name,
    )(x, g, w)


def _rwkv_prep_kernel(p_ref, prev_ref, mix_ref, wwa_ref, gup_ref, w0_ref, a0_ref, kk_ref, ka_ref,
                      rk_ref, seg_ref, tri_ref,
                      rt_ref, at_ref, kt_ref, bt_ref, kh_ref, bh_ref, v_ref, dc_ref, bonus_ref,
                      gate_ref, *, seq_tiles):
    i = pl.program_id(0)
    ts = p_ref.shape[0]
    w = rt_ref.shape[1]
    p = p_ref[...]
    prev = prev_ref[7:8, :]
    prev = jnp.where(i % seq_tiles == 0, jnp.zeros_like(prev), prev)
    shifted = pltpu.roll(p, 1, axis=0)
    shifted = jnp.where(_iota(p.shape, 0) == 0, prev, shifted)
    xs = p + mix_ref[...] * (shifted - p)

    r = xs[:, 0:w]
    k = xs[:, w:2 * w]
    v = xs[:, 2 * w:3 * w]
    lo = 3 * w
    pwa = xs[:, lo:lo + LANE]
    pg = xs[:, lo + LANE:lo + 3 * LANE]
    lane = _iota(pwa.shape, 1)
    z = jnp.where(lane < DECAY_LORA, jnp.tanh(pwa), pwa)
    wa = _dot3(z, wwa_ref[...])
    wl = w0_ref[...] + wa[:, :w]
    neg = -wl
    softplus = jnp.maximum(neg, 0.0) + jnp.log(1.0 + jnp.exp(-jnp.abs(neg)))
    lw = -jnp.exp(-softplus - 0.5)
    a = _sigmoid(a0_ref[...] + wa[:, w:])
    gate_ref[...] = _dot3(_sigmoid(pg), gup_ref[...])

    seg = seg_ref[...]
    kk = k * kk_ref[...]
    ss = _dot_exact_rhs(kk * kk, seg)
    kk = kk * lax.rsqrt(jnp.maximum(ss, 1e-24))
    k2 = k * (1.0 + (a - 1.0) * ka_ref[...])
    bonus_ref[...] = _dot_exact_rhs(r * k2 * rk_ref[...], seg) * v

    gc = _dot_exact_lhs(tri_ref[...], lw)
    nc = ts // CHUNK
    ends = [gc[(q + 1) * CHUNK - 1:(q + 1) * CHUNK, :] for q in range(nc)]
    gend = jnp.concatenate([jnp.broadcast_to(e, (CHUNK, w)) for e in ends], axis=0)
    to_end = jnp.exp(gend - gc)
    e_in = jnp.exp(gc)
    e_out = jnp.exp(-gc)
    b = kk * a
    rt_ref[...] = r * e_in
    at_ref[...] = -kk * jnp.exp(gc - lw)
    kt_ref[...] = k2 * e_out
    bt_ref[...] = b * e_out
    kh_ref[...] = k2 * to_end
    bh_ref[...] = b * to_end
    v_ref[...] = v
    dc_ref[0] = jnp.concatenate([jnp.exp(e) for e in ends] + [jnp.zeros((8 - nc, w), F32)], axis=0)


def _rwkv_prep(p_rwkv, mix, wwa, gup, w0, a0, k_k, k_a, r_k, seg, tri, *, seq, ts=256):
    t, pc = p_rwkv.shape
    w = w0.shape[1]
    nt = t // ts
    row = lambda i: (i, 0)
    tok = pl.BlockSpec((ts, w), row)
    tok_shape = jax.ShapeDtypeStruct((t, w), F32)
    return pl.pallas_call(
        functools.partial(_rwkv_prep_kernel, seq_tiles=seq // ts),
        out_shape=[tok_shape] * 7 + [jax.ShapeDtypeStruct((nt, 8, w), F32), tok_shape, tok_shape],
        grid=(nt,),
        in_specs=[
            pl.BlockSpec((ts, pc), row),
            pl.BlockSpec((8, pc), lambda i: (jnp.maximum(i * (ts // 8) - 1, 0), 0)),
            _const_spec((1, pc)),
            _const_spec(wwa.shape),
            _const_spec(gup.shape),
            _const_spec((1, w)), _const_spec((1, w)), _const_spec((1, w)), _const_spec((1, w)),
            _const_spec((1, w)),
            _const_spec(seg.shape),
            _const_spec(tri.shape),
        ],
        out_specs=[tok] * 7 + [pl.BlockSpec((1, 8, w), lambda i: (i, 0, 0)), tok, tok],
        compiler_params=_cparams(("parallel",)),
        name="rwkv_prep",
    )(p_rwkv, p_rwkv, mix, wwa, gup, w0, a0, k_k, k_a, r_k, seg, tri)


PASSES_SCAN = 1
QUAD = 4


def _operand(x, passes):
    return _split2(x) if passes == 3 else (x.astype(BF16),)


def _prod(a, b, mm=_mm):
    if len(a) == 2 and len(b) == 2:
        return mm(a[0], b[0]) + (mm(a[0], b[1]) + mm(a[1], b[0]))
    return mm(a[0], b[0])


def _block_diag(y, n):
    c = y.shape[0]
    tiled = jnp.concatenate([y] * (y.shape[1] // n), axis=0)
    keep = (_iota(tiled.shape, 0) // c) == (_iota(tiled.shape, 1) // n)
    return jnp.where(keep, tiled, jnp.zeros_like(tiled))


def _quad_mm(x, y, n, mm=_mm):
    return mm(x.astype(BF16), _block_diag(y.astype(BF16), n))


def _unit_lower_inverse(a_list, row, col):
    n = CHUNK
    eye = (row == col).astype(F32)
    same8 = (row // 8) == (col // 8)
    a8 = [jnp.where(same8, a, 0.0) for a in a_list]
    d8 = [_block_diag(x.astype(BF16), n) for x in a8]
    a8_2 = [_mm(x.astype(BF16), d) for x, d in zip(a8, d8)]
    d8_2 = [_block_diag(x.astype(BF16), n) for x in a8_2]
    a8_4 = [_mm(x.astype(BF16), d) for x, d in zip(a8_2, d8_2)]
    p = [eye + x + x2 + _mm(x.astype(BF16), d2) for x, x2, d2 in zip(a8, a8_2, d8_2)]
    t = [pp + _quad_mm(pp, x4, n) for pp, x4 in zip(p, a8_4)]
    m = 16
    while m <= CHUNK:
        sel = ((row // m) == (col // m)) & ((row // (m // 2)) != (col // (m // 2)))
        mid = [_quad_mm(x, jnp.where(sel, a, 0.0), n) for x, a in zip(t, a_list)]
        t = [x + _quad_mm(md, x, n) for x, md in zip(t, mid)]
        m *= 2
    return t


def _rwkv_intra_kernel(rt_ref, at_ref, kt_ref, bt_ref, kh_ref, bh_ref, v_ref, dc_ref,
                       rr_ref, o0_ref, gh_ref, *, heads, chunks_per_tile):
    n = RWKV_HEAD_DIM
    qw = QUAD * n
    i = pl.program_id(0)
    row = _iota((CHUNK, qw), 0)
    col = _iota((CHUNK, qw), 1) % n
    strict = col < row
    incl = col <= row
    dc_all = dc_ref[0]
    dc_row = dc_all[0:1, :]
    for q in range(1, chunks_per_tile):
        dc_row = jnp.where(i % chunks_per_tile == q, dc_all[q:q + 1, :], dc_row)

    quads = [slice(j * qw, (j + 1) * qw) for j in range(heads // QUAD)]
    at = [at_ref[:, s] for s in quads]
    rt = [rt_ref[:, s] for s in quads]
    v = [v_ref[:, s] for s in quads]
    bd_b = [_block_diag(bt_ref[:, s].astype(BF16), n) for s in quads]
    bd_k = [_block_diag(kt_ref[:, s].astype(BF16), n) for s in quads]
    ar = [jnp.concatenate([a, r], axis=0).astype(BF16) for a, r in zip(at, rt)]
    mb = [_mm_nt(x, d) for x, d in zip(ar, bd_b)]
    mk = [_mm_nt(x, d) for x, d in zip(ar, bd_k)]
    a_ab = [jnp.where(strict, m[:CHUNK], 0.0) for m in mb]
    a_rb = [jnp.where(incl, m[CHUNK:], 0.0) for m in mb]
    akrk = [jnp.concatenate([jnp.where(strict, m[:CHUNK], 0.0), jnp.where(incl, m[CHUNK:], 0.0)], axis=0)
            for m in mk]
    avv = [_quad_mm(x, y, n) for x, y in zip(akrk, v)]
    tinv = _unit_lower_inverse(a_ab, row, col)
    a_new = [_quad_mm(t, a, n) for t, a in zip(tinv, at)]
    u0 = [_quad_mm(t, w[:CHUNK], n) for t, w in zip(tinv, avv)]
    for j, s in enumerate(quads):
        rr_ref[:, s] = rt[j] + _quad_mm(a_rb[j], a_new[j], n)
        o0_ref[:, s] = _quad_mm(a_rb[j], u0[j], n) + avv[j][CHUNK:]
    eye = _iota((n, n), 0) == _iota((n, n), 1)
    gz, kv = [], []
    for h in range(heads):
        j, s = h // QUAD, slice((h % QUAD) * n, (h % QUAD + 1) * n)
        hs = slice(h * n, (h + 1) * n)
        z = jnp.concatenate([a_new[j][:, s], u0[j][:, s]], axis=1).astype(BF16)
        gz.append(_mm_tn(bh_ref[:, hs].astype(BF16), z))
        kv.append(_mm_tn(kh_ref[:, hs].astype(BF16), v[j][:, s].astype(BF16)))
    for h in range(heads):
        hs = slice(h * n, (h + 1) * n)
        dmat = jnp.where(eye, jnp.broadcast_to(dc_row[:, hs], (n, n)), 0.0)
        gh_ref[0, h] = gz[h] + jnp.concatenate([dmat, kv[h]], axis=1)


def _rwkv_scan_kernel(rr_ref, o0_ref, gh_ref, o_ref, state_ref, *, heads):
    n = RWKV_HEAD_DIM

    @pl.when(pl.program_id(1) == 0)
    def _():
        state_ref[...] = jnp.zeros_like(state_ref)

    sls = [slice(h * n, (h + 1) * n) for h in range(heads)]
    h0 = [_operand(state_ref[h], PASSES_SCAN) for h in range(heads)]
    outs = [_prod(_operand(rr_ref[:, s], PASSES_SCAN), x) + o0_ref[:, s] for s, x in zip(sls, h0)]
    new = [_prod(_operand(gh_ref[0, h, :, 0:n], PASSES_SCAN), h0[h]) + gh_ref[0, h, :, n:2 * n]
           for h in range(heads)]
    for h, s in enumerate(sls):
        o_ref[:, s] = outs[h]
        state_ref[h] = new[h]


def _rwkv_chunk(rt, at, kt, bt, kh, bh, v, dc, *, batch, seq, prep_ts):
    t, w = rt.shape
    n = RWKV_HEAD_DIM
    heads = w // n
    nchunk = seq // CHUNK
    cpt = prep_ts // CHUNK
    tok = pl.BlockSpec((CHUNK, w), lambda i: (i, 0))
    tok_shape = jax.ShapeDtypeStruct((t, w), F32)
    rr, o0, gh = pl.pallas_call(
        functools.partial(_rwkv_intra_kernel, heads=heads, chunks_per_tile=cpt),
        out_shape=[tok_shape, tok_shape, jax.ShapeDtypeStruct((t // CHUNK, heads, n, 2 * n), F32)],
        grid=(t // CHUNK,),
        in_specs=[tok] * 7 + [pl.BlockSpec((1, 8, w), lambda i: (i // cpt, 0, 0))],
        out_specs=[tok, tok, pl.BlockSpec((1, heads, n, 2 * n), lambda i: (i, 0, 0, 0))],
        compiler_params=_cparams(("parallel",)),
        name="rwkv_intra",
    )(rt, at, kt, bt, kh, bh, v, dc)
    tok2 = pl.BlockSpec((CHUNK, w), lambda b, c: (b * nchunk + c, 0))
    return pl.pallas_call(
        functools.partial(_rwkv_scan_kernel, heads=heads),
        out_shape=tok_shape,
        grid=(batch, nchunk),
        in_specs=[tok2, tok2, pl.BlockSpec((1, heads, n, 2 * n), lambda b, c: (b * nchunk + c, 0, 0, 0))],
        out_specs=tok2,
        scratch_shapes=[pltpu.VMEM((heads, n, n), F32)],
        compiler_params=_cparams(("parallel", "arbitrary")),
        name="rwkv_scan",
    )(rr, o0, gh)


def _rope_lanes(x, cos_t, sin_a, sin_b):
    width = x.shape[1]
    up = pltpu.roll(x, width - ROPE_HALF, axis=1)
    dn = pltpu.roll(x, ROPE_HALF, axis=1)
    return x * cos_t + up * sin_a + dn * sin_b


def _tile_lanes(tab, width):
    return jnp.concatenate([tab] * (width // tab.shape[1]), axis=1)


def _nsa_prep_kernel(p_ref, cos_ref, sa_ref, sb_ref, qn_ref, kn_ref, seg_ref,
                     q_ref, ks_ref, vs_ref, kw_ref, vw_ref, g_ref, *, seq_tiles):
    dh = NSA_HEAD_DIM
    qw = q_ref.shape[1]
    kvw = NSA_KV_HEADS * dh
    seg = seg_ref[...]
    cos_t, sin_a, sin_b = cos_ref[...], sa_ref[...], sb_ref[...]

    def norm_rope(x, gain):
        wd = x.shape[1]
        ms = _head_mean(x * x, seg[:wd, :wd], dh)
        y = x * lax.rsqrt(ms + NORM_EPS) * gain
        return _rope_lanes(y, _tile_lanes(cos_t, wd), _tile_lanes(sin_a, wd), _tile_lanes(sin_b, wd))

    q = norm_rope(p_ref[:, 0:qw], qn_ref[...])
    q_ref[...] = q * (dh ** -0.5)
    base = qw + 2 * kvw
    ks = norm_rope(p_ref[:, base:base + kvw], kn_ref[1:2, :])
    vs = p_ref[:, base + kvw:base + 2 * kvw].astype(BF16)
    kw = norm_rope(p_ref[:, base + 2 * kvw:base + 3 * kvw], kn_ref[2:3, :]).astype(BF16)
    vw = p_ref[:, base + 3 * kvw:base + 4 * kvw].astype(BF16)
    ts = p_ref.shape[0]
    tpos = (pl.program_id(0) % seq_tiles) * ts + _iota((ts, dh), 0)
    onehot = jnp.where(tpos // SEL_BLOCK == _iota((ts, dh), 1), 1.0, 0.0)
    for h in range(NSA_KV_HEADS):
        sl = slice(h * dh, (h + 1) * dh)
        ks_ref[0, h] = jnp.concatenate([ks[:, sl], onehot], axis=1).astype(BF16)
        vs_ref[0, h] = vs[:, sl]
        kw_ref[0, h] = kw[:, sl]
        vw_ref[0, h] = vw[:, sl]
    sig = _sigmoid(p_ref[:, base + 4 * kvw:base + 4 * kvw + LANE])
    sig_t = sig.T
    per_head = 3 * NSA_GROUP
    for h in range(NSA_KV_HEADS):
        g_ref[h] = sig_t[per_head * h:per_head * h + GATE_ROWS, :]


def _nsa_prep(p_nsa, cos_t, sin_a, sin_b, qn, kn, seg, *, batch, seq, ts=256):
    t, pc = p_nsa.shape
    qw = NSA_HEADS * NSA_HEAD_DIM
    st = seq // ts
    tab = pl.BlockSpec((ts, LANE), lambda i: (i % st, 0))
    hm = pl.BlockSpec((1, NSA_KV_HEADS, ts, NSA_HEAD_DIM), lambda i: (i // st, 0, i % st, 0))
    hm_shape = jax.ShapeDtypeStruct((batch, NSA_KV_HEADS, seq, NSA_HEAD_DIM), BF16)
    assert seq // SEL_BLOCK <= NSA_HEAD_DIM
    aug = pl.BlockSpec((1, NSA_KV_HEADS, ts, 2 * NSA_HEAD_DIM), lambda i: (i // st, 0, i % st, 0))
    aug_shape = jax.ShapeDtypeStruct((batch, NSA_KV_HEADS, seq, 2 * NSA_HEAD_DIM), BF16)
    return pl.pallas_call(
        functools.partial(_nsa_prep_kernel, seq_tiles=st),
        out_shape=[jax.ShapeDtypeStruct((t, qw), F32), aug_shape] + [hm_shape] * 3
        + [jax.ShapeDtypeStruct((NSA_KV_HEADS, GATE_ROWS, t), F32)],
        grid=(t // ts,),
        in_specs=[pl.BlockSpec((ts, pc), lambda i: (i, 0)), tab, tab, tab,
                  _const_spec(qn.shape), _const_spec(kn.shape), _const_spec(seg.shape)],
        out_specs=[pl.BlockSpec((ts, qw), lambda i: (i, 0)), aug] + [hm] * 3
        + [pl.BlockSpec((NSA_KV_HEADS, GATE_ROWS, ts), lambda i: (0, 0, i))],
        compiler_params=_cparams(("parallel",)),
        name="nsa_prep",
    )(p_nsa, cos_t, sin_a, sin_b, qn, kn, seg)


def _gelu_tanh(x):
    return 0.5 * x * (1.0 + jnp.tanh(np.sqrt(2.0 / np.pi).astype(np.float32) * (x + 0.044715 * (x * x * x))))


def _compress_kernel(gk_ref, gv_ref, pk_ref, pv_ref, k1_ref, k2_ref, v1_ref, v2_ref, kn_ref,
                     cos_ref, sin_ref, rot_ref, kc_ref, vc_ref):
    half = k1_ref.shape[0] // 2

    def mlp(g, pos, w1_ref, w2_ref):
        ya = _dot3(g, w1_ref[0:half, :])
        yb = _dot3(g, w1_ref[half:, :])
        bias = _dot3(jnp.broadcast_to(pos, (8, pos.shape[1])), w1_ref[...])[0:1, :]
        n = g.shape[0]
        hid = ya + pltpu.roll(yb, n - 1, axis=0) + bias
        return _dot3(_gelu_tanh(hid), w2_ref[...])

    kc = mlp(gk_ref[0, 0], pk_ref[...], k1_ref, k2_ref)
    ms = jnp.mean(kc * kc, axis=-1, keepdims=True)
    kc = kc * lax.rsqrt(ms + NORM_EPS) * kn_ref[0:1, :]
    kc_ref[0, 0] = kc * cos_ref[...] + _dot_exact_rhs(kc, rot_ref[...]) * sin_ref[...]
    vc_ref[0, 0] = mlp(gv_ref[0, 0], pv_ref[...], v1_ref, v2_ref)


def _nsa_compress(gk, gv, pk, pv, k1, k2, v1, v2, kn, cos_c, sin_c, rot):
    b, hk, ng, gw = gk.shape
    dh = NSA_HEAD_DIM
    grp = pl.BlockSpec((1, 1, ng, gw), lambda i, j: (i, j, 0, 0))
    out = pl.BlockSpec((1, 1, ng, dh), lambda i, j: (i, j, 0, 0))
    shape = jax.ShapeDtypeStruct((b, hk, ng, dh), F32)
    consts = [pk, pv, k1, k2, v1, v2, kn, cos_c, sin_c, rot]
    return pl.pallas_call(
        _compress_kernel,
        out_shape=[shape, shape],
        grid=(b, hk),
        in_specs=[grp, grp] + [_const_spec(c.shape) for c in consts],
        out_specs=[out, out],
        compiler_params=_cparams(("parallel", "parallel")),
        name="nsa_compress",
    )(gk, gv, *consts)


def _nsa_attn_kernel(q_ref, kc_ref, vc_ref, ks_ref, vs_ref, kw_ref, vw_ref, g_ref, ovt_ref, o_ref):
    dh = NSA_HEAD_DIM
    grp = NSA_GROUP
    nh = kc_ref.shape[1]
    qi = pl.program_id(2)
    tq = q_ref.shape[0]
    hc = grp * tq
    cols = nh * hc
    t0 = qi * tq
    ncmp = kc_ref.shape[2]
    nsel = ovt_ref.shape[0]
    kb = ATTN_KEY_BLOCK
    span = WINDOW + tq
    heads = range(nh)

    def lanes(xs):
        return jnp.concatenate(xs, axis=1)

    def every_head(x):
        return lanes([x] * (nh * grp))

    q = q_ref[...]
    q4 = [jnp.concatenate([q[:, (h * grp + g) * dh:(h * grp + g + 1) * dh] for g in range(grp)], axis=0)
          for h in heads]
    q4b = [(x * LOG2_E).astype(BF16) for x in q4]

    st = lanes([_dot3(kc_ref[0, h], q4[h], _mm_nt) for h in heads])
    tl = t0 + _iota((ncmp, cols), 1) % tq
    cmask = _iota((ncmp, cols), 0) * CMP_STRIDE + (CMP_BLOCK - 1) <= tl
    sm = jnp.where(cmask, st, MASKED)
    e = jnp.where(cmask, jnp.exp(sm - jnp.max(sm, axis=0, keepdims=True)), 0.0)
    pt = e / jnp.maximum(jnp.sum(e, axis=0, keepdims=True), 1e-30)
    ptb = pt.astype(BF16)
    o_cmp = lanes([_mm_tn(vc_ref[0, h].astype(BF16), ptb[:, h * hc:(h + 1) * hc]) for h in heads])
    psum = []
    for h in heads:
        acc = pt[:, h * hc:h * hc + tq]
        for g in range(1, grp):
            acc = acc + pt[:, h * hc + g * tq:h * hc + (g + 1) * tq]
        psum.append(acc)
    psum = lanes(psum)

    imp = _dot_exact_lhs(ovt_ref[...], psum)
    blk = _iota(imp.shape, 0)
    cur = (t0 + _iota(imp.shape, 1) % tq) // SEL_BLOCK
    forced = (blk == 0) | (blk == cur) | (blk == cur - 1)
    imp = jnp.where(forced, jnp.inf, jnp.where(blk > cur, -jnp.inf, imp))
    rank = jnp.zeros(imp.shape, jnp.int32)
    for m in range(nsel):
        im = imp[m:m + 1, :]
        ahead = (im > imp) | ((im == imp) & (m < blk))
        rank = rank + ahead.astype(jnp.int32)
    sel_bias = jnp.where(rank < min(SEL_TOP, nsel), 0.0, MASKED).astype(BF16)

    eye = jnp.where(_iota((nsel, dh), 0) == _iota((nsel, dh), 1), 1.0, 0.0).astype(BF16)
    bias_q = _mm_tn(sel_bias, eye)
    q_aug = [jnp.concatenate([q4[h] * LOG2_E, jnp.concatenate([bias_q[h * tq:(h + 1) * tq]] * grp, axis=0)],
                             axis=1).astype(BF16) for h in heads]

    def values(v_ref, rows, pb):
        return lanes([_mm_tn(v_ref[0, h, rows, :], pb[:, h * hc:(h + 1) * hc]) for h in heads])

    def online_softmax(s, rows, m_run, l_run, acc):
        m_new = jnp.maximum(m_run, jnp.max(s, axis=0, keepdims=True))
        alpha = jnp.exp2(m_run - m_new)
        p = jnp.exp2(s - m_new)
        l_new = alpha * l_run + jnp.sum(p, axis=0, keepdims=True)
        return m_new, l_new, alpha * acc + values(vs_ref, rows, p.astype(BF16))

    def slc_block(k0, state, bias=None):
        rows = pl.ds(pl.multiple_of(k0, kb), kb)
        s = lanes([_mm_nt(ks_ref[0, h, rows, :], q_aug[h]) for h in heads])
        return online_softmax(s if bias is None else s + bias, rows, *state)

    init = (jnp.full((1, cols), MASKED, F32), jnp.zeros((1, cols), F32), jnp.zeros((dh, cols), F32))
    state = lax.fori_loop(0, t0 // kb, lambda j, st: slc_block(j * kb, st), init)
    for d in range(tq // kb):
        causal = jnp.where(d * kb + _iota((kb, tq), 0) <= _iota((kb, tq), 1), 0.0, MASKED)
        state = slc_block(t0 + d * kb, state, every_head(causal))
    o_slc = state[2] / state[1]

    w0 = pl.multiple_of(jnp.maximum(t0 - WINDOW, 0), tq)
    wrows = pl.ds(w0, span)
    kpos = w0 + _iota((span, tq), 0)
    tw = t0 + _iota((span, tq), 1)
    wbias = jnp.where((kpos <= tw) & (kpos > tw - WINDOW), 0.0, MASKED)
    s = lanes([_mm_nt(kw_ref[0, h, wrows, :], q4b[h]) for h in heads]) + every_head(wbias)
    p = jnp.exp2(s - jnp.max(s, axis=0, keepdims=True))
    o_win = values(vw_ref, wrows, p.astype(BF16)) / jnp.sum(p, axis=0, keepdims=True)

    gates = g_ref[...]
    grow = [lanes([gates[h, 3 * g + br:3 * g + br + 1, :] for h in heads for g in range(grp)])
            for br in range(3)]
    o4 = grow[0] * o_cmp + grow[1] * o_slc + grow[2] * o_win
    o_ref[...] = jnp.concatenate([o4[:, c * tq:(c + 1) * tq] for c in range(nh * grp)], axis=0)


def _nsa_attn(q, kc, vc, ks_aug, vs, kw, vw, gates_t, overlap_t, *, batch, seq, tq=256,
              nh=ATTN_KV_PER_STEP):
    t, qw = q.shape
    dh = NSA_HEAD_DIM
    gw = nh * NSA_GROUP * dh
    st = seq // tq
    ncmp = kc.shape[2]
    cmp_spec = pl.BlockSpec((1, nh, ncmp, dh), lambda b, h, i: (b, h, 0, 0))
    kv_spec = pl.BlockSpec((1, nh, seq, dh), lambda b, h, i: (b, h, 0, 0))
    aug_spec = pl.BlockSpec((1, nh, seq, 2 * dh), lambda b, h, i: (b, h, 0, 0))
    return pl.pallas_call(
        _nsa_attn_kernel,
        out_shape=jax.ShapeDtypeStruct((qw, t), F32),
        grid=(batch, NSA_KV_HEADS // nh, st),
        in_specs=[
            pl.BlockSpec((tq, gw), lambda b, h, i: (b * st + i, h)),
            cmp_spec, cmp_spec, aug_spec, kv_spec, kv_spec, kv_spec,
            pl.BlockSpec((nh, gates_t.shape[1], tq), lambda b, h, i: (h, 0, b * st + i)),
            _const_spec(overlap_t.shape),
        ],
        out_specs=pl.BlockSpec((gw, tq), lambda b, h, i: (h, b * st + i)),
        compiler_params=_cparams(("parallel", "parallel", "arbitrary")),
        name="nsa_attn",
    )(q, kc, vc, ks_aug, vs, kw, vw, gates_t, overlap_t)


def _merge_kernel(x_ref, o_ref, bonus_ref, gate_ref, ybt_ref, pg_ref, gnw_ref, gnb_ref, seg_ref,
                  ua_ref, ub_ref, wo_ref, out_ref):
    d = x_ref.shape[1]
    n = RWKV_HEAD_DIM
    seg = seg_ref[...]
    o = o_ref[...]
    mu = _dot_exact_rhs(o, seg) * (1.0 / n)
    dlt = o - mu
    var = _head_mean(dlt * dlt, seg, n)
    on = dlt * lax.rsqrt(var + GN_EPS) * gnw_ref[...] + gnb_ref[...]
    ya = ((on + bonus_ref[...]) * gate_ref[...]).astype(BF16)
    yb_t = ybt_ref[...].astype(BF16)
    merged = (_sigmoid(pg_ref[:, 0:d]) * _mm(ya, ua_ref[...])
              + _sigmoid(pg_ref[:, d:2 * d]) * _mm_tn(yb_t, ub_ref[...]))
    out_ref[...] = x_ref[...] + _mm(merged.astype(BF16), wo_ref[...])


def _merge(x, o_rwkv, bonus, gate, yb_t, pg, gnw, gnb, seg, ua, ub, wo, *, tm=256):
    t, d = x.shape
    w = o_rwkv.shape[1]
    row = lambda i: (i, 0)
    tokw = pl.BlockSpec((tm, w), row)
    return pl.pallas_call(
        _merge_kernel,
        out_shape=jax.ShapeDtypeStruct((t, d), F32),
        grid=(t // tm,),
        in_specs=[pl.BlockSpec((tm, d), row), tokw, tokw, tokw,
                  pl.BlockSpec((yb_t.shape[0], tm), lambda i: (0, i)),
                  pl.BlockSpec((tm, 2 * d), row),
                  _const_spec((1, w)), _const_spec((1, w)), _const_spec(seg.shape),
                  _const_spec(ua.shape), _const_spec(ub.shape), _const_spec(wo.shape)],
        out_specs=pl.BlockSpec((tm, d), row),
        compiler_params=_cparams(("parallel",)),
        name="merge",
    )(x, o_rwkv, bonus, gate, yb_t, pg, gnw, gnb, seg, ua, ub, wo)


def _block_diag_ones(width, block):
    idx = np.arange(width) // block
    return jnp.asarray(idx[:, None] == idx[None, :], BF16)


def _chunk_lower_ones(ts):
    i = np.arange(ts)
    return jnp.asarray((i[:, None] // CHUNK == i[None, :] // CHUNK) & (i[None, :] <= i[:, None]), BF16)


def _rope_tables(pos):
    inv = ROPE_THETA ** (-jnp.arange(ROPE_HALF, dtype=F32) / ROPE_HALF)
    ang = jnp.asarray(pos).astype(F32)[:, None] * inv[None, :]
    cos, sin = jnp.cos(ang), jnp.sin(ang)
    n = ang.shape[0]
    pad = jnp.zeros((n, NSA_HEAD_DIM - ROPE_DIM), F32)
    zero = jnp.zeros_like(sin)
    cos_h = jnp.concatenate([cos, cos, pad + 1.0], axis=1)
    sa_h = jnp.concatenate([-sin, zero, pad], axis=1)
    sb_h = jnp.concatenate([zero, sin, pad], axis=1)
    return cos_h, sa_h, sb_h


def _rot_half_matrix():
    r = np.zeros((NSA_HEAD_DIM, NSA_HEAD_DIM), np.float32)
    for l in range(ROPE_HALF):
        r[l + ROPE_HALF, l] = -1.0
        r[l, l + ROPE_HALF] = 1.0
    return jnp.asarray(r, BF16)


def _overlap_matrix_t(ncmp_pad, nsel):
    cs = np.arange(ncmp_pad)[None, :] * CMP_STRIDE
    ss = np.arange(nsel)[:, None] * SEL_BLOCK
    ov = np.clip(np.minimum(cs + CMP_BLOCK, ss + SEL_BLOCK) - np.maximum(cs, ss), 0, None) / CMP_BLOCK
    return jnp.asarray(ov, BF16)


def _pad_cols(x, width):
    return jnp.pad(x, ((0, 0), (0, width - x.shape[1])))


def _layer(x, l, ffn1_norm, ffn1_w_gate, ffn1_w_up, ffn1_w_down, mix_norm, w_in,
           rwkv_mix, rwkv_w0, rwkv_w_up, rwkv_a0, rwkv_a_up, rwkv_g_up,
           rwkv_k_k, rwkv_k_a, rwkv_r_k, rwkv_gn_w, rwkv_gn_b,
           nsa_q_norm, nsa_k_norm, cmp_pos_k, cmp_pos_v,
           cmp_k_w1, cmp_k_w2, cmp_v_w1, cmp_v_w2,
           w_branch_rwkv, w_branch_nsa, w_out,
           ffn2_norm, ffn2_w_gate, ffn2_w_up, ffn2_w_down, *, batch, seq):
    t, d = x.shape
    w = rwkv_w0.shape[1]
    dh = NSA_HEAD_DIM
    qw = NSA_HEADS * dh
    kvw = NSA_KV_HEADS * dh
    prep_ts = 256
    row = lambda v: v.reshape(1, -1)

    x = _ffn(x, row(ffn1_norm[l]), ffn1_w_gate[l].astype(BF16), ffn1_w_up[l].astype(BF16),
             ffn1_w_down[l].astype(BF16))

    wi = w_in[l]
    rwkv_cols = 3 * w + DECAY_LORA + ICLR_LORA + GATE_LORA
    rwkv_pad = 3 * w + 3 * LANE
    nsa_cols = qw + 6 * kvw + 3 * NSA_HEADS
    nsa_pad = qw + 6 * kvw + LANE
    g_mix = row(mix_norm[l])
    p_rwkv = _norm_proj(x, g_mix, _pad_cols(wi[:, :rwkv_cols], rwkv_pad).astype(BF16), name="proj_rwkv")
    p_nsa = _norm_proj(x, g_mix, _pad_cols(wi[:, rwkv_cols:rwkv_cols + nsa_cols], nsa_pad).astype(BF16),
                       name="proj_nsa")
    p_gate = _norm_proj(x, g_mix, wi[:, rwkv_cols + nsa_cols:].astype(BF16), name="proj_gate")

    wwa = jnp.zeros((LANE, 2 * w), F32)
    wwa = wwa.at[:DECAY_LORA, :w].set(rwkv_w_up[l]).at[DECAY_LORA:, w:].set(rwkv_a_up[l])
    gup = jnp.pad(rwkv_g_up[l], ((0, 2 * LANE - GATE_LORA), (0, 0)))
    seg_w = _block_diag_ones(w, RWKV_HEAD_DIM)
    (rt, at, kt, bt, kh, bh, v, dc, bonus, gate) = _rwkv_prep(
        p_rwkv, _pad_cols(row(rwkv_mix[l]), rwkv_pad), wwa, gup, row(rwkv_w0[l]), row(rwkv_a0[l]),
        row(rwkv_k_k[l]), row(rwkv_k_a[l]), row(rwkv_r_k[l]), seg_w, _chunk_lower_ones(prep_ts),
        seq=seq, ts=prep_ts)
    o_rwkv = _rwkv_chunk(rt, at, kt, bt, kh, bh, v, dc, batch=batch, seq=seq, prep_ts=prep_ts)

    cos_t, sin_a, sin_b = _rope_tables(np.arange(seq))
    two = lambda tab: jnp.concatenate([tab, tab], axis=1)
    qn = jnp.tile(row(nsa_q_norm[l]), (1, NSA_HEADS))
    kn = jnp.tile(nsa_k_norm[l], (1, NSA_KV_HEADS))
    q, ks, vs, kw, vw, gates = _nsa_prep(p_nsa, two(cos_t), two(sin_a), two(sin_b), qn, kn,
                                         _block_diag_ones(qw, dh), batch=batch, seq=seq)

    ngrp = seq // CMP_STRIDE
    grp_w = CMP_STRIDE * dh

    def groups(cols):
        g = cols.reshape(batch, ngrp, CMP_STRIDE, NSA_KV_HEADS, dh)
        return jnp.transpose(g, (0, 3, 1, 2, 4)).reshape(batch, NSA_KV_HEADS, ngrp, grp_w)

    cend = np.arange(ngrp) * CMP_STRIDE + CMP_BLOCK - 1
    cos_c, sa_c, sb_c = _rope_tables(cend)
    kc, vc = _nsa_compress(
        groups(p_nsa[:, qw:qw + kvw]), groups(p_nsa[:, qw + kvw:qw + 2 * kvw]),
        cmp_pos_k[l].reshape(1, -1), cmp_pos_v[l].reshape(1, -1),
        cmp_k_w1[l], cmp_k_w2[l], cmp_v_w1[l], cmp_v_w2[l], nsa_k_norm[l],
        cos_c, sb_c - sa_c, _rot_half_matrix())
    nsel = seq // SEL_BLOCK
    y_nsa = _nsa_attn(q, kc, vc, ks, vs, kw, vw, gates, _overlap_matrix_t(ngrp, nsel),
                      batch=batch, seq=seq)

    x = _merge(x, o_rwkv, bonus, gate, y_nsa, p_gate, row(rwkv_gn_w[l]), row(rwkv_gn_b[l]), seg_w,
               w_branch_rwkv[l].astype(BF16), w_branch_nsa[l].astype(BF16), w_out[l].astype(BF16))
    return _ffn(x, row(ffn2_norm[l]), ffn2_w_gate[l].astype(BF16), ffn2_w_up[l].astype(BF16),
                ffn2_w_down[l].astype(BF16))


def kernel(x, ffn1_norm, ffn1_w_gate, ffn1_w_up, ffn1_w_down, mix_norm, w_in, rwkv_mix, rwkv_w0, rwkv_w_up, rwkv_a0, rwkv_a_up, rwkv_g_up, rwkv_k_k, rwkv_k_a, rwkv_r_k, rwkv_gn_w, rwkv_gn_b, nsa_q_norm, nsa_k_norm, cmp_pos_k, cmp_pos_v, cmp_k_w1, cmp_k_w2, cmp_v_w1, cmp_v_w2, w_branch_rwkv, w_branch_nsa, w_out, ffn2_norm, ffn2_w_gate, ffn2_w_up, ffn2_w_down):
    batch, seq, d = x.shape
    params = (ffn1_norm, ffn1_w_gate, ffn1_w_up, ffn1_w_down, mix_norm, w_in, rwkv_mix, rwkv_w0,
              rwkv_w_up, rwkv_a0, rwkv_a_up, rwkv_g_up, rwkv_k_k, rwkv_k_a, rwkv_r_k, rwkv_gn_w,
              rwkv_gn_b, nsa_q_norm, nsa_k_norm, cmp_pos_k, cmp_pos_v, cmp_k_w1, cmp_k_w2, cmp_v_w1,
              cmp_v_w2, w_branch_rwkv, w_branch_nsa, w_out, ffn2_norm, ffn2_w_gate, ffn2_w_up,
              ffn2_w_down)
    y = x.reshape(batch * seq, d)
    for l in range(ffn1_norm.shape[0]):
        y = _layer(y, l, *params, batch=batch, seq=seq)
    return y.reshape(batch, seq, d)
```

```python
import functools

import numpy as np
import jax
import jax.numpy as jnp
from jax import lax
from jax.experimental import pallas as pl
from jax.experimental.pallas import tpu as pltpu

F32 = jnp.float32
BF16 = jnp.bfloat16

RWKV_HEAD_DIM = 64
DECAY_LORA = 64
ICLR_LORA = 64
GATE_LORA = 160
GN_EPS = 64e-5
NSA_HEADS = 16
NSA_KV_HEADS = 4
NSA_GROUP = NSA_HEADS // NSA_KV_HEADS
NSA_HEAD_DIM = 64
ROPE_DIM = NSA_HEAD_DIM // 4
ROPE_HALF = ROPE_DIM // 2
ROPE_THETA = 500000.0
CMP_BLOCK = 32
CMP_STRIDE = 16
SEL_BLOCK = 64
SEL_TOP = 16
WINDOW = 512
NORM_EPS = 1e-6

LANE = 128
CHUNK = 64
VMEM_LIMIT = 56 * 1024 * 1024
MASKED = -1e30
LOG2_E = 1.4426950408889634
GATE_ROWS = 16
ATTN_KEY_BLOCK = 256
ATTN_KV_PER_STEP = 4


def _cparams(sem):
    return pltpu.CompilerParams(dimension_semantics=sem, vmem_limit_bytes=VMEM_LIMIT)


def _const_spec(shape):
    nd = len(shape)
    return pl.BlockSpec(shape, lambda *_: (0,) * nd, pipeline_mode=pl.Buffered(1))


def _mm(a, b):
    return lax.dot_general(a, b, (((1,), (0,)), ((), ())), preferred_element_type=F32)


def _mm_nt(a, b):
    return lax.dot_general(a, b, (((1,), (1,)), ((), ())), preferred_element_type=F32)


def _mm_tn(a, b):
    return lax.dot_general(a, b, (((0,), (0,)), ((), ())), preferred_element_type=F32)


def _split2(x):
    hi = x.astype(BF16)
    lo = (x - hi.astype(F32)).astype(BF16)
    return hi, lo


def _split3(x):
    h1 = x.astype(BF16)
    r1 = x - h1.astype(F32)
    h2 = r1.astype(BF16)
    h3 = (r1 - h2.astype(F32)).astype(BF16)
    return h1, h2, h3


def _dot3(a, b, mm=_mm):
    a1, a2 = _split2(a)
    b1, b2 = _split2(b)
    return mm(a1, b1) + (mm(a1, b2) + mm(a2, b1))


def _dot_exact_rhs(a, b_bf16):
    a1, a2 = _split2(a)
    return _mm(a1, b_bf16) + _mm(a2, b_bf16)


def _head_mean(x, seg_bf16, width):
    return _mm(x.astype(BF16), seg_bf16) * (1.0 / width)


def _dot_exact_lhs(a_bf16, b):
    b1, b2, b3 = _split3(b)
    return _mm(a_bf16, b1) + (_mm(a_bf16, b2) + _mm(a_bf16, b3))


def _sigmoid(x):
    return 1.0 / (1.0 + jnp.exp(-x))


def _iota(shape, dim):
    return lax.broadcasted_iota(jnp.int32, shape, dim)


def _ffn_kernel(x_ref, g_ref, wg_ref, wu_ref, wd_ref, o_ref, h_ref):
    j = pl.program_id(1)

    @pl.when(j == 0)
    def _():
        x = x_ref[...]
        ms = jnp.mean(x * x, axis=-1, keepdims=True)
        h_ref[...] = (x * lax.rsqrt(ms + NORM_EPS) * g_ref[...]).astype(BF16)
        o_ref[...] = jnp.zeros_like(o_ref)

    h = h_ref[...]
    gate = _mm(h, wg_ref[...])
    up = _mm(h, wu_ref[...])
    act = (gate * _sigmoid(gate) * up).astype(BF16)
    o_ref[...] += _mm(act, wd_ref[...])

    @pl.when(j == pl.num_programs(1) - 1)
    def _():
        o_ref[...] = x_ref[...] + 0.5 * o_ref[...]


def _ffn(x, g, wg, wu, wd, *, tm=1024, tf=512):
    t, d = x.shape
    f = wg.shape[1]
    return pl.pallas_call(
        _ffn_kernel,
        out_shape=jax.ShapeDtypeStruct((t, d), F32),
        grid=(t // tm, f // tf),
        in_specs=[
            pl.BlockSpec((tm, d), lambda i, j: (i, 0)),
            pl.BlockSpec((1, d), lambda i, j: (0, 0)),
            pl.BlockSpec((d, tf), lambda i, j: (0, j)),
            pl.BlockSpec((d, tf), lambda i, j: (0, j)),
            pl.BlockSpec((tf, d), lambda i, j: (j, 0)),
        ],
        out_specs=pl.BlockSpec((tm, d), lambda i, j: (i, 0)),
        scratch_shapes=[pltpu.VMEM((tm, d), BF16)],
        compiler_params=_cparams(("parallel", "arbitrary")),
        name="ffn",
    )(x, g, wg, wu, wd)


def _norm_proj_kernel(x_ref, g_ref, w_ref, o_ref):
    x = x_ref[...]
    ms = jnp.mean(x * x, axis=-1, keepdims=True)
    h = (x * lax.rsqrt(ms + NORM_EPS) * g_ref[...]).astype(BF16)
    o_ref[...] = _mm(h, w_ref[...]).astype(o_ref.dtype)


def _norm_proj(x, g, w, *, tm=256, name="norm_proj"):
    t, d = x.shape
    n = w.shape[1]
    return pl.pallas_call(
        _norm_proj_kernel,
        out_shape=jax.ShapeDtypeStruct((t, n), F32),
        grid=(t // tm,),
        in_specs=[
            pl.BlockSpec((tm, d), lambda i: (i, 0)),
            _const_spec((1, d)),
            _const_spec((d, n)),
        ],
        out_specs=pl.BlockSpec((tm, n), lambda i: (i, 0)),
        compiler_params=_cparams(("parallel",)),
        name=name,
    )(x, g, w)


def _rwkv_prep_kernel(p_ref, prev_ref, mix_ref, wwa_ref, gup_ref, w0_ref, a0_ref, kk_ref, ka_ref,
                      rk_ref, seg_ref, tri_ref,
                      rt_ref, at_ref, kt_ref, bt_ref, kh_ref, bh_ref, v_ref, dc_ref, bonus_ref,
                      gate_ref, *, seq_tiles):
    i = pl.program_id(0)
    ts = p_ref.shape[0]
    w = rt_ref.shape[1]
    p = p_ref[...]
    prev = prev_ref[7:8, :]
    prev = jnp.where(i % seq_tiles == 0, jnp.zeros_like(prev), prev)
    shifted = pltpu.roll(p, 1, axis=0)
    shifted = jnp.where(_iota(p.shape, 0) == 0, prev, shifted)
    xs = p + mix_ref[...] * (shifted - p)

    r = xs[:, 0:w]
    k = xs[:, w:2 * w]
    v = xs[:, 2 * w:3 * w]
    lo = 3 * w
    pwa = xs[:, lo:lo + LANE]
    pg = xs[:, lo + LANE:lo + 3 * LANE]
    lane = _iota(pwa.shape, 1)
    z = jnp.where(lane < DECAY_LORA, jnp.tanh(pwa), pwa)
    wa = _dot3(z, wwa_ref[...])
    wl = w0_ref[...] + wa[:, :w]
    neg = -wl
    softplus = jnp.maximum(neg, 0.0) + jnp.log(1.0 + jnp.exp(-jnp.abs(neg)))
    lw = -jnp.exp(-softplus - 0.5)
    a = _sigmoid(a0_ref[...] + wa[:, w:])
    gate_ref[...] = _dot3(_sigmoid(pg), gup_ref[...])

    seg = seg_ref[...]
    kk = k * kk_ref[...]
    ss = _dot_exact_rhs(kk * kk, seg)
    kk = kk * lax.rsqrt(jnp.maximum(ss, 1e-24))
    k2 = k * (1.0 + (a - 1.0) * ka_ref[...])
    bonus_ref[...] = _dot_exact_rhs(r * k2 * rk_ref[...], seg) * v

    gc = _dot_exact_lhs(tri_ref[...], lw)
    nc = ts // CHUNK
    ends = [gc[(q + 1) * CHUNK - 1:(q + 1) * CHUNK, :] for q in range(nc)]
    gend = jnp.concatenate([jnp.broadcast_to(e, (CHUNK, w)) for e in ends], axis=0)
    to_end = jnp.exp(gend - gc)
    e_in = jnp.exp(gc)
    e_out = jnp.exp(-gc)
    b = kk * a
    rt_ref[...] = r * e_in
    at_ref[...] = -kk * jnp.exp(gc - lw)
    kt_ref[...] = k2 * e_out
    bt_ref[...] = b * e_out
    kh_ref[...] = k2 * to_end
    bh_ref[...] = b * to_end
    v_ref[...] = v
    dc_ref[0] = jnp.concatenate([jnp.exp(e) for e in ends] + [jnp.zeros((8 - nc, w), F32)], axis=0)


def _rwkv_prep(p_rwkv, mix, wwa, gup, w0, a0, k_k, k_a, r_k, seg, tri, *, seq, ts=256):
    t, pc = p_rwkv.shape
    w = w0.shape[1]
    nt = t // ts
    row = lambda i: (i, 0)
    tok = pl.BlockSpec((ts, w), row)
    tok_shape = jax.ShapeDtypeStruct((t, w), F32)
    return pl.pallas_call(
        functools.partial(_rwkv_prep_kernel, seq_tiles=seq // ts),
        out_shape=[tok_shape] * 7 + [jax.ShapeDtypeStruct((nt, 8, w), F32), tok_shape, tok_shape],
        grid=(nt,),
        in_specs=[
            pl.BlockSpec((ts, pc), row),
            pl.BlockSpec((8, pc), lambda i: (jnp.maximum(i * (ts // 8) - 1, 0), 0)),
            _const_spec((1, pc)),
            _const_spec(wwa.shape),
            _const_spec(gup.shape),
            _const_spec((1, w)), _const_spec((1, w)), _const_spec((1, w)), _const_spec((1, w)),
            _const_spec((1, w)),
            _const_spec(seg.shape),
            _const_spec(tri.shape),
        ],
        out_specs=[tok] * 7 + [pl.BlockSpec((1, 8, w), lambda i: (i, 0, 0)), tok, tok],
        compiler_params=_cparams(("parallel",)),
        name="rwkv_prep",
    )(p_rwkv, p_rwkv, mix, wwa, gup, w0, a0, k_k, k_a, r_k, seg, tri)


PASSES_SCAN = 1
QUAD = 4


def _operand(x, passes):
    return _split2(x) if passes == 3 else (x.astype(BF16),)


def _prod(a, b, mm=_mm):
    if len(a) == 2 and len(b) == 2:
        return mm(a[0], b[0]) + (mm(a[0], b[1]) + mm(a[1], b[0]))
    return mm(a[0], b[0])


def _block_diag(y, n):
    c = y.shape[0]
    tiled = jnp.concatenate([y] * (y.shape[1] // n), axis=0)
    keep = (_iota(tiled.shape, 0) // c) == (_iota(tiled.shape, 1) // n)
    return jnp.where(keep, tiled, jnp.zeros_like(tiled))


def _quad_mm(x, y, n, mm=_mm):
    return mm(x.astype(BF16), _block_diag(y.astype(BF16), n))


def _unit_lower_inverse(a_list, row, col):
    n = CHUNK
    eye = (row == col).astype(F32)
    same8 = (row // 8) == (col // 8)
    a8 = [jnp.where(same8, a, 0.0) for a in a_list]
    d8 = [_block_diag(x.astype(BF16), n) for x in a8]
    a8_2 = [_mm(x.astype(BF16), d) for x, d in zip(a8, d8)]
    d8_2 = [_block_diag(x.astype(BF16), n) for x in a8_2]
    a8_4 = [_mm(x.astype(BF16), d) for x, d in zip(a8_2, d8_2)]
    p = [eye + x + x2 + _mm(x.astype(BF16), d2) for x, x2, d2 in zip(a8, a8_2, d8_2)]
    t = [pp + _quad_mm(pp, x4, n) for pp, x4 in zip(p, a8_4)]
    m = 16
    while m <= CHUNK:
        sel = ((row // m) == (col // m)) & ((row // (m // 2)) != (col // (m // 2)))
        mid = [_quad_mm(x, jnp.where(sel, a, 0.0), n) for x, a in zip(t, a_list)]
        t = [x + _quad_mm(md, x, n) for x, md in zip(t, mid)]
        m *= 2
    return t


def _rwkv_intra_kernel(rt_ref, at_ref, kt_ref, bt_ref, kh_ref, bh_ref, v_ref, dc_ref,
                       rr_ref, o0_ref, gh_ref, *, heads, chunks_per_tile):
    n = RWKV_HEAD_DIM
    qw = QUAD * n
    i = pl.program_id(0)
    row = _iota((CHUNK, qw), 0)
    col = _iota((CHUNK, qw), 1) % n
    strict = col < row
    incl = col <= row
    dc_all = dc_ref[0]
    dc_row = dc_all[0:1, :]
    for q in range(1, chunks_per_tile):
        dc_row = jnp.where(i % chunks_per_tile == q, dc_all[q:q + 1, :], dc_row)

    quads = [slice(j * qw, (j + 1) * qw) for j in range(heads // QUAD)]
    at = [at_ref[:, s] for s in quads]
    rt = [rt_ref[:, s] for s in quads]
    v = [v_ref[:, s] for s in quads]
    bd_b = [_block_diag(bt_ref[:, s].astype(BF16), n) for s in quads]
    bd_k = [_block_diag(kt_ref[:, s].astype(BF16), n) for s in quads]
    ar = [jnp.concatenate([a, r], axis=0).astype(BF16) for a, r in zip(at, rt)]
    mb = [_mm_nt(x, d) for x, d in zip(ar, bd_b)]
    mk = [_mm_nt(x, d) for x, d in zip(ar, bd_k)]
    a_ab = [jnp.where(strict, m[:CHUNK], 0.0) for m in mb]
    a_rb = [jnp.where(incl, m[CHUNK:], 0.0) for m in mb]
    akrk = [jnp.concatenate([jnp.where(strict, m[:CHUNK], 0.0), jnp.where(incl, m[CHUNK:], 0.0)], axis=0)
            for m in mk]
    avv = [_quad_mm(x, y, n) for x, y in zip(akrk, v)]
    tinv = _unit_lower_inverse(a_ab, row, col)
    a_new = [_quad_mm(t, a, n) for t, a in zip(tinv, at)]
    u0 = [_quad_mm(t, w[:CHUNK], n) for t, w in zip(tinv, avv)]
    for j, s in enumerate(quads):
        rr_ref[:, s] = rt[j] + _quad_mm(a_rb[j], a_new[j], n)
        o0_ref[:, s] = _quad_mm(a_rb[j], u0[j], n) + avv[j][CHUNK:]
    eye = _iota((n, n), 0) == _iota((n, n), 1)
    gz, kv = [], []
    for h in range(heads):
        j, s = h // QUAD, slice((h % QUAD) * n, (h % QUAD + 1) * n)
        hs = slice(h * n, (h + 1) * n)
        z = jnp.concatenate([a_new[j][:, s], u0[j][:, s]], axis=1).astype(BF16)
        gz.append(_mm_tn(bh_ref[:, hs].astype(BF16), z))
        kv.append(_mm_tn(kh_ref[:, hs].astype(BF16), v[j][:, s].astype(BF16)))
    for h in range(heads):
        hs = slice(h * n, (h + 1) * n)
        dmat = jnp.where(eye, jnp.broadcast_to(dc_row[:, hs], (n, n)), 0.0)
        gh_ref[0, h] = gz[h] + jnp.concatenate([dmat, kv[h]], axis=1)


def _rwkv_scan_kernel(rr_ref, o0_ref, gh_ref, o_ref, state_ref, *, heads):
    n = RWKV_HEAD_DIM

    @pl.when(pl.program_id(1) == 0)
    def _():
        state_ref[...] = jnp.zeros_like(state_ref)

    sls = [slice(h * n, (h + 1) * n) for h in range(heads)]
    h0 = [_operand(state_ref[h], PASSES_SCAN) for h in range(heads)]
    outs = [_prod(_operand(rr_ref[:, s], PASSES_SCAN), x) + o0_ref[:, s] for s, x in zip(sls, h0)]
    new = [_prod(_operand(gh_ref[0, h, :, 0:n], PASSES_SCAN), h0[h]) + gh_ref[0, h, :, n:2 * n]
           for h in range(heads)]
    for h, s in enumerate(sls):
        o_ref[:, s] = outs[h]
        state_ref[h] = new[h]


def _rwkv_chunk(rt, at, kt, bt, kh, bh, v, dc, *, batch, seq, prep_ts):
    t, w = rt.shape
    n = RWKV_HEAD_DIM
    heads = w // n
    nchunk = seq // CHUNK
    cpt = prep_ts // CHUNK
    tok = pl.BlockSpec((CHUNK, w), lambda i: (i, 0))
    tok_shape = jax.ShapeDtypeStruct((t, w), F32)
    rr, o0, gh = pl.pallas_call(
        functools.partial(_rwkv_intra_kernel, heads=heads, chunks_per_tile=cpt),
        out_shape=[tok_shape, tok_shape, jax.ShapeDtypeStruct((t // CHUNK, heads, n, 2 * n), F32)],
        grid=(t // CHUNK,),
        in_specs=[tok] * 7 + [pl.BlockSpec((1, 8, w), lambda i: (i // cpt, 0, 0))],
        out_specs=[tok, tok, pl.BlockSpec((1, heads, n, 2 * n), lambda i: (i, 0, 0, 0))],
        compiler_params=_cparams(("parallel",)),
        name="rwkv_intra",
    )(rt, at, kt, bt, kh, bh, v, dc)
    tok2 = pl.BlockSpec((CHUNK, w), lambda b, c: (b * nchunk + c, 0))
    return pl.pallas_call(
        functools.partial(_rwkv_scan_kernel, heads=heads),
        out_shape=tok_shape,
        grid=(batch, nchunk),
        in_specs=[tok2, tok2, pl.BlockSpec((1, heads, n, 2 * n), lambda b, c: (b * nchunk + c, 0, 0, 0))],
        out_specs=tok2,
        scratch_shapes=[pltpu.VMEM((heads, n, n), F32)],
        compiler_params=_cparams(("parallel", "arbitrary")),
        name="rwkv_scan",
    )(rr, o0, gh)


def _rope_lanes(x, cos_t, sin_a, sin_b):
    width = x.shape[1]
    up = pltpu.roll(x, width - ROPE_HALF, axis=1)
    dn = pltpu.roll(x, ROPE_HALF, axis=1)
    return x * cos_t + up * sin_a + dn * sin_b


def _tile_lanes(tab, width):
    return jnp.concatenate([tab] * (width // tab.shape[1]), axis=1)


def _nsa_prep_kernel(p_ref, cos_ref, sa_ref, sb_ref, qn_ref, kn_ref, seg_ref,
                     q_ref, ks_ref, vs_ref, kw_ref, vw_ref, g_ref, *, seq_tiles):
    dh = NSA_HEAD_DIM
    qw = q_ref.shape[1]
    kvw = NSA_KV_HEADS * dh
    seg = seg_ref[...]
    cos_t, sin_a, sin_b = cos_ref[...], sa_ref[...], sb_ref[...]

    def norm_rope(x, gain):
        wd = x.shape[1]
        ms = _head_mean(x * x, seg[:wd, :wd], dh)
        y = x * lax.rsqrt(ms + NORM_EPS) * gain
        return _rope_lanes(y, _tile_lanes(cos_t, wd), _tile_lanes(sin_a, wd), _tile_lanes(sin_b, wd))

    q = norm_rope(p_ref[:, 0:qw], qn_ref[...])
    q_ref[...] = q * (dh ** -0.5)
    base = qw + 2 * kvw
    ks = norm_rope(p_ref[:, base:base + kvw], kn_ref[1:2, :])
    vs = p_ref[:, base + kvw:base + 2 * kvw].astype(BF16)
    kw = norm_rope(p_ref[:, base + 2 * kvw:base + 3 * kvw], kn_ref[2:3, :]).astype(BF16)
    vw = p_ref[:, base + 3 * kvw:base + 4 * kvw].astype(BF16)
    ts = p_ref.shape[0]
    tpos = (pl.program_id(0) % seq_tiles) * ts + _iota((ts, dh), 0)
    onehot = jnp.where(tpos // SEL_BLOCK == _iota((ts, dh), 1), 1.0, 0.0)
    for h in range(NSA_KV_HEADS):
        sl = slice(h * dh, (h + 1) * dh)
        ks_ref[0, h] = jnp.concatenate([ks[:, sl], onehot], axis=1).astype(BF16)
        vs_ref[0, h] = vs[:, sl]
        kw_ref[0, h] = kw[:, sl]
        vw_ref[0, h] = vw[:, sl]
    sig = _sigmoid(p_ref[:, base + 4 * kvw:base + 4 * kvw + LANE])
    sig_t = sig.T
    per_head = 3 * NSA_GROUP
    for h in range(NSA_KV_HEADS):
        g_ref[h] = sig_t[per_head * h:per_head * h + GATE_ROWS, :]


def _nsa_prep(p_nsa, cos_t, sin_a, sin_b, qn, kn, seg, *, batch, seq, ts=256):
    t, pc = p_nsa.shape
    qw = NSA_HEADS * NSA_HEAD_DIM
    st = seq // ts
    tab = pl.BlockSpec((ts, LANE), lambda i: (i % st, 0))
    hm = pl.BlockSpec((1, NSA_KV_HEADS, ts, NSA_HEAD_DIM), lambda i: (i // st, 0, i % st, 0))
    hm_shape = jax.ShapeDtypeStruct((batch, NSA_KV_HEADS, seq, NSA_HEAD_DIM), BF16)
    assert seq // SEL_BLOCK <= NSA_HEAD_DIM
    aug = pl.BlockSpec((1, NSA_KV_HEADS, ts, 2 * NSA_HEAD_DIM), lambda i: (i // st, 0, i % st, 0))
    aug_shape = jax.ShapeDtypeStruct((batch, NSA_KV_HEADS, seq, 2 * NSA_HEAD_DIM), BF16)
    return pl.pallas_call(
        functools.partial(_nsa_prep_kernel, seq_tiles=st),
        out_shape=[jax.ShapeDtypeStruct((t, qw), F32), aug_shape] + [hm_shape] * 3
        + [jax.ShapeDtypeStruct((NSA_KV_HEADS, GATE_ROWS, t), F32)],
        grid=(t // ts,),
        in_specs=[pl.BlockSpec((ts, pc), lambda i: (i, 0)), tab, tab, tab,
                  _const_spec(qn.shape), _const_spec(kn.shape), _const_spec(seg.shape)],
        out_specs=[pl.BlockSpec((ts, qw), lambda i: (i, 0)), aug] + [hm] * 3
        + [pl.BlockSpec((NSA_KV_HEADS, GATE_ROWS, ts), lambda i: (0, 0, i))],
        compiler_params=_cparams(("parallel",)),
        name="nsa_prep",
    )(p_nsa, cos_t, sin_a, sin_b, qn, kn, seg)


def _gelu_tanh(x):
    return 0.5 * x * (1.0 + jnp.tanh(np.sqrt(2.0 / np.pi).astype(np.float32) * (x + 0.044715 * (x * x * x))))


def _compress_kernel(gk_ref, gv_ref, pk_ref, pv_ref, k1_ref, k2_ref, v1_ref, v2_ref, kn_ref,
                     cos_ref, sin_ref, rot_ref, kc_ref, vc_ref):
    half = k1_ref.shape[0] // 2

    def mlp(g, pos, w1_ref, w2_ref):
        ya = _dot3(g, w1_ref[0:half, :])
        yb = _dot3(g, w1_ref[half:, :])
        bias = _dot3(jnp.broadcast_to(pos, (8, pos.shape[1])), w1_ref[...])[0:1, :]
        n = g.shape[0]
        hid = ya + pltpu.roll(yb, n - 1, axis=0) + bias
        return _dot3(_gelu_tanh(hid), w2_ref[...])

    kc = mlp(gk_ref[0, 0], pk_ref[...], k1_ref, k2_ref)
    ms = jnp.mean(kc * kc, axis=-1, keepdims=True)
    kc = kc * lax.rsqrt(ms + NORM_EPS) * kn_ref[0:1, :]
    kc_ref[0, 0] = kc * cos_ref[...] + _dot_exact_rhs(kc, rot_ref[...]) * sin_ref[...]
    vc_ref[0, 0] = mlp(gv_ref[0, 0], pv_ref[...], v1_ref, v2_ref)


def _nsa_compress(gk, gv, pk, pv, k1, k2, v1, v2, kn, cos_c, sin_c, rot):
    b, hk, ng, gw = gk.shape
    dh = NSA_HEAD_DIM
    grp = pl.BlockSpec((1, 1, ng, gw), lambda i, j: (i, j, 0, 0))
    out = pl.BlockSpec((1, 1, ng, dh), lambda i, j: (i, j, 0, 0))
    shape = jax.ShapeDtypeStruct((b, hk, ng, dh), F32)
    consts = [pk, pv, k1, k2, v1, v2, kn, cos_c, sin_c, rot]
    return pl.pallas_call(
        _compress_kernel,
        out_shape=[shape, shape],
        grid=(b, hk),
        in_specs=[grp, grp] + [_const_spec(c.shape) for c in consts],
        out_specs=[out, out],
        compiler_params=_cparams(("parallel", "parallel")),
        name="nsa_compress",
    )(gk, gv, *consts)


def _nsa_attn_kernel(q_ref, kc_ref, vc_ref, ks_ref, vs_ref, kw_ref, vw_ref, g_ref, ovt_ref, o_ref):
    dh = NSA_HEAD_DIM
    grp = NSA_GROUP
    nh = kc_ref.shape[1]
    qi = pl.program_id(2)
    tq = q_ref.shape[0]
    hc = grp * tq
    cols = nh * hc
    t0 = qi * tq
    ncmp = kc_ref.shape[2]
    nsel = ovt_ref.shape[0]
    kb = ATTN_KEY_BLOCK
    span = WINDOW + tq
    heads = range(nh)

    def lanes(xs):
        return jnp.concatenate(xs, axis=1)

    def every_head(x):
        return lanes([x] * (nh * grp))

    q = q_ref[...]
    q4 = [jnp.concatenate([q[:, (h * grp + g) * dh:(h * grp + g + 1) * dh] for g in range(grp)], axis=0)
          for h in heads]
    q4b = [(x * LOG2_E).astype(BF16) for x in q4]

    st = lanes([_dot3(kc_ref[0, h], q4[h], _mm_nt) for h in heads])
    tl = t0 + _iota((ncmp, cols), 1) % tq
    cmask = _iota((ncmp, cols), 0) * CMP_STRIDE + (CMP_BLOCK - 1) <= tl
    sm = jnp.where(cmask, st, MASKED)
    e = jnp.where(cmask, jnp.exp(sm - jnp.max(sm, axis=0, keepdims=True)), 0.0)
    pt = e / jnp.maximum(jnp.sum(e, axis=0, keepdims=True), 1e-30)
    ptb = pt.astype(BF16)
    o_cmp = lanes([_mm_tn(vc_ref[0, h].astype(BF16), ptb[:, h * hc:(h + 1) * hc]) for h in heads])
    psum = []
    for h in heads:
        acc = pt[:, h * hc:h * hc + tq]
        for g in range(1, grp):
            acc = acc + pt[:, h * hc + g * tq:h * hc + (g + 1) * tq]
        psum.append(acc)
    psum = lanes(psum)

    imp = _dot_exact_lhs(ovt_ref[...], psum)
    blk = _iota(imp.shape, 0)
    cur = (t0 + _iota(imp.shape, 1) % tq) // SEL_BLOCK
    forced = (blk == 0) | (blk == cur) | (blk == cur - 1)
    imp = jnp.where(forced, jnp.inf, jnp.where(blk > cur, -jnp.inf, imp))
    rank = jnp.zeros(imp.shape, jnp.int32)
    for m in range(nsel):
        im = imp[m:m + 1, :]
        ahead = (im > imp) | ((im == imp) & (m < blk))
        rank = rank + ahead.astype(jnp.int32)
    sel_bias = jnp.where(rank < min(SEL_TOP, nsel), 0.0, MASKED).astype(BF16)

    eye = jnp.where(_iota((nsel, dh), 0) == _iota((nsel, dh), 1), 1.0, 0.0).astype(BF16)
    bias_q = _mm_tn(sel_bias, eye)
    q_aug = [jnp.concatenate([q4[h] * LOG2_E, jnp.concatenate([bias_q[h * tq:(h + 1) * tq]] * grp, axis=0)],
                             axis=1).astype(BF16) for h in heads]

    def attend(k_ref, v_ref, qs, k0, state, bias=None):
        m_run, l_run, acc = state
        rows = pl.ds(pl.multiple_of(k0, kb), kb)
        s = lanes([_mm_nt(k_ref[0, h, rows, :], qs[h]) for h in heads])
        if bias is not None:
            s = s + bias
        m_new = jnp.maximum(m_run, jnp.max(s, axis=0, keepdims=True))
        alpha = jnp.exp2(m_run - m_new)
        pb = jnp.exp2(s - m_new)
        l_new = alpha * l_run + jnp.sum(pb, axis=0, keepdims=True)
        pb = pb.astype(BF16)
        pv = lanes([_mm_tn(v_ref[0, h, rows, :], pb[:, h * hc:(h + 1) * hc]) for h in heads])
        return m_new, l_new, alpha * acc + pv

    init = (jnp.full((1, cols), MASKED, F32), jnp.zeros((1, cols), F32), jnp.zeros((dh, cols), F32))
    state = lax.fori_loop(0, t0 // kb, lambda j, st: attend(ks_ref, vs_ref, q_aug, j * kb, st), init)
    for d in range(tq // kb):
        causal = jnp.where(d * kb + _iota((kb, tq), 0) <= _iota((kb, tq), 1), 0.0, MASKED)
        state = attend(ks_ref, vs_ref, q_aug, t0 + d * kb, state, every_head(causal))
    o_slc = state[2] / state[1]

    w0 = pl.multiple_of(jnp.maximum(t0 - WINDOW, 0), tq)
    wrows = pl.ds(w0, span)
    kpos = w0 + _iota((span, tq), 0)
    tw = t0 + _iota((span, tq), 1)
    wbias = jnp.where((kpos <= tw) & (kpos > tw - WINDOW), 0.0, MASKED)
    s = lanes([_mm_nt(kw_ref[0, h, wrows, :], q4b[h]) for h in heads]) + every_head(wbias)
    p = jnp.exp2(s - jnp.max(s, axis=0, keepdims=True))
    pb = p.astype(BF16)
    o_win = (lanes([_mm_tn(vw_ref[0, h, wrows, :], pb[:, h * hc:(h + 1) * hc]) for h in heads])
             / jnp.sum(p, axis=0, keepdims=True))

    gates = g_ref[...]
    grow = [lanes([gates[h, 3 * g + br:3 * g + br + 1, :] for h in heads for g in range(grp)])
            for br in range(3)]
    o4 = grow[0] * o_cmp + grow[1] * o_slc + grow[2] * o_win
    o_ref[...] = jnp.concatenate([o4[:, c * tq:(c + 1) * tq] for c in range(nh * grp)], axis=0)


def _nsa_attn(q, kc, vc, ks_aug, vs, kw, vw, gates_t, overlap_t, *, batch, seq, tq=256,
              nh=ATTN_KV_PER_STEP):
    t, qw = q.shape
    dh = NSA_HEAD_DIM
    gw = nh * NSA_GROUP * dh
    st = seq // tq
    ncmp = kc.shape[2]
    cmp_spec = pl.BlockSpec((1, nh, ncmp, dh), lambda b, h, i: (b, h, 0, 0))
    kv_spec = pl.BlockSpec((1, nh, seq, dh), lambda b, h, i: (b, h, 0, 0))
    aug_spec = pl.BlockSpec((1, nh, seq, 2 * dh), lambda b, h, i: (b, h, 0, 0))
    return pl.pallas_call(
        _nsa_attn_kernel,
        out_shape=jax.ShapeDtypeStruct((qw, t), F32),
        grid=(batch, NSA_KV_HEADS // nh, st),
        in_specs=[
            pl.BlockSpec((tq, gw), lambda b, h, i: (b * st + i, h)),
            cmp_spec, cmp_spec, aug_spec, kv_spec, kv_spec, kv_spec,
            pl.BlockSpec((nh, gates_t.shape[1], tq), lambda b, h, i: (h, 0, b * st + i)),
            _const_spec(overlap_t.shape),
        ],
        out_specs=pl.BlockSpec((gw, tq), lambda b, h, i: (h, b * st + i)),
        compiler_params=_cparams(("parallel", "parallel", "arbitrary")),
        name="nsa_attn",
    )(q, kc, vc, ks_aug, vs, kw, vw, gates_t, overlap_t)


def _merge_kernel(x_ref, o_ref, bonus_ref, gate_ref, ybt_ref, pg_ref, gnw_ref, gnb_ref, seg_ref,
                  ua_ref, ub_ref, wo_ref, out_ref):
    d = x_ref.shape[1]
    n = RWKV_HEAD_DIM
    seg = seg_ref[...]
    o = o_ref[...]
    mu = _dot_exact_rhs(o, seg) * (1.0 / n)
    dlt = o - mu
    var = _head_mean(dlt * dlt, seg, n)
    on = dlt * lax.rsqrt(var + GN_EPS) * gnw_ref[...] + gnb_ref[...]
    ya = ((on + bonus_ref[...]) * gate_ref[...]).astype(BF16)
    yb_t = ybt_ref[...].astype(BF16)
    merged = (_sigmoid(pg_ref[:, 0:d]) * _mm(ya, ua_ref[...])
              + _sigmoid(pg_ref[:, d:2 * d]) * _mm_tn(yb_t, ub_ref[...]))
    out_ref[...] = x_ref[...] + _mm(merged.astype(BF16), wo_ref[...])


def _merge(x, o_rwkv, bonus, gate, yb_t, pg, gnw, gnb, seg, ua, ub, wo, *, tm=256):
    t, d = x.shape
    w = o_rwkv.shape[1]
    row = lambda i: (i, 0)
    tokw = pl.BlockSpec((tm, w), row)
    return pl.pallas_call(
        _merge_kernel,
        out_shape=jax.ShapeDtypeStruct((t, d), F32),
        grid=(t // tm,),
        in_specs=[pl.BlockSpec((tm, d), row), tokw, tokw, tokw,
                  pl.BlockSpec((yb_t.shape[0], tm), lambda i: (0, i)),
                  pl.BlockSpec((tm, 2 * d), row),
                  _const_spec((1, w)), _const_spec((1, w)), _const_spec(seg.shape),
                  _const_spec(ua.shape), _const_spec(ub.shape), _const_spec(wo.shape)],
        out_specs=pl.BlockSpec((tm, d), row),
        compiler_params=_cparams(("parallel",)),
        name="merge",
    )(x, o_rwkv, bonus, gate, yb_t, pg, gnw, gnb, seg, ua, ub, wo)


def _block_diag_ones(width, block):
    idx = np.arange(width) // block
    return jnp.asarray(idx[:, None] == idx[None, :], BF16)


def _chunk_lower_ones(ts):
    i = np.arange(ts)
    return jnp.asarray((i[:, None] // CHUNK == i[None, :] // CHUNK) & (i[None, :] <= i[:, None]), BF16)


def _rope_tables(pos):
    inv = ROPE_THETA ** (-jnp.arange(ROPE_HALF, dtype=F32) / ROPE_HALF)
    ang = jnp.asarray(pos).astype(F32)[:, None] * inv[None, :]
    cos, sin = jnp.cos(ang), jnp.sin(ang)
    n = ang.shape[0]
    pad = jnp.zeros((n, NSA_HEAD_DIM - ROPE_DIM), F32)
    zero = jnp.zeros_like(sin)
    cos_h = jnp.concatenate([cos, cos, pad + 1.0], axis=1)
    sa_h = jnp.concatenate([-sin, zero, pad], axis=1)
    sb_h = jnp.concatenate([zero, sin, pad], axis=1)
    return cos_h, sa_h, sb_h


def _rot_half_matrix():
    r = np.zeros((NSA_HEAD_DIM, NSA_HEAD_DIM), np.float32)
    for l in range(ROPE_HALF):
        r[l + ROPE_HALF, l] = -1.0
        r[l, l + ROPE_HALF] = 1.0
    return jnp.asarray(r, BF16)


def _overlap_matrix_t(ncmp_pad, nsel):
    cs = np.arange(ncmp_pad)[None, :] * CMP_STRIDE
    ss = np.arange(nsel)[:, None] * SEL_BLOCK
    ov = np.clip(np.minimum(cs + CMP_BLOCK, ss + SEL_BLOCK) - np.maximum(cs, ss), 0, None) / CMP_BLOCK
    return jnp.asarray(ov, BF16)


def _pad_cols(x, width):
    return jnp.pad(x, ((0, 0), (0, width - x.shape[1])))


def _layer(x, l, ffn1_norm, ffn1_w_gate, ffn1_w_up, ffn1_w_down, mix_norm, w_in,
           rwkv_mix, rwkv_w0, rwkv_w_up, rwkv_a0, rwkv_a_up, rwkv_g_up,
           rwkv_k_k, rwkv_k_a, rwkv_r_k, rwkv_gn_w, rwkv_gn_b,
           nsa_q_norm, nsa_k_norm, cmp_pos_k, cmp_pos_v,
           cmp_k_w1, cmp_k_w2, cmp_v_w1, cmp_v_w2,
           w_branch_rwkv, w_branch_nsa, w_out,
           ffn2_norm, ffn2_w_gate, ffn2_w_up, ffn2_w_down, *, batch, seq):
    t, d = x.shape
    w = rwkv_w0.shape[1]
    dh = NSA_HEAD_DIM
    qw = NSA_HEADS * dh
    kvw = NSA_KV_HEADS * dh
    prep_ts = 256
    row = lambda v: v.reshape(1, -1)

    x = _ffn(x, row(ffn1_norm[l]), ffn1_w_gate[l].astype(BF16), ffn1_w_up[l].astype(BF16),
             ffn1_w_down[l].astype(BF16))

    wi = w_in[l]
    rwkv_cols = 3 * w + DECAY_LORA + ICLR_LORA + GATE_LORA
    rwkv_pad = 3 * w + 3 * LANE
    nsa_cols = qw + 6 * kvw + 3 * NSA_HEADS
    nsa_pad = qw + 6 * kvw + LANE
    g_mix = row(mix_norm[l])
    p_rwkv = _norm_proj(x, g_mix, _pad_cols(wi[:, :rwkv_cols], rwkv_pad).astype(BF16), name="proj_rwkv")
    p_nsa = _norm_proj(x, g_mix, _pad_cols(wi[:, rwkv_cols:rwkv_cols + nsa_cols], nsa_pad).astype(BF16),
                       name="proj_nsa")
    p_gate = _norm_proj(x, g_mix, wi[:, rwkv_cols + nsa_cols:].astype(BF16), name="proj_gate")

    wwa = jnp.zeros((LANE, 2 * w), F32)
    wwa = wwa.at[:DECAY_LORA, :w].set(rwkv_w_up[l]).at[DECAY_LORA:, w:].set(rwkv_a_up[l])
    gup = jnp.pad(rwkv_g_up[l], ((0, 2 * LANE - GATE_LORA), (0, 0)))
    seg_w = _block_diag_ones(w, RWKV_HEAD_DIM)
    (rt, at, kt, bt, kh, bh, v, dc, bonus, gate) = _rwkv_prep(
        p_rwkv, _pad_cols(row(rwkv_mix[l]), rwkv_pad), wwa, gup, row(rwkv_w0[l]), row(rwkv_a0[l]),
        row(rwkv_k_k[l]), row(rwkv_k_a[l]), row(rwkv_r_k[l]), seg_w, _chunk_lower_ones(prep_ts),
        seq=seq, ts=prep_ts)
    o_rwkv = _rwkv_chunk(rt, at, kt, bt, kh, bh, v, dc, batch=batch, seq=seq, prep_ts=prep_ts)

    cos_t, sin_a, sin_b = _rope_tables(np.arange(seq))
    two = lambda tab: jnp.concatenate([tab, tab], axis=1)
    qn = jnp.tile(row(nsa_q_norm[l]), (1, NSA_HEADS))
    kn = jnp.tile(nsa_k_norm[l], (1, NSA_KV_HEADS))
    q, ks, vs, kw, vw, gates = _nsa_prep(p_nsa, two(cos_t), two(sin_a), two(sin_b), qn, kn,
                                         _block_diag_ones(qw, dh), batch=batch, seq=seq)

    ngrp = seq // CMP_STRIDE
    grp_w = CMP_STRIDE * dh

    def groups(cols):
        g = cols.reshape(batch, ngrp, CMP_STRIDE, NSA_KV_HEADS, dh)
        return jnp.transpose(g, (0, 3, 1, 2, 4)).reshape(batch, NSA_KV_HEADS, ngrp, grp_w)

    cend = np.arange(ngrp) * CMP_STRIDE + CMP_BLOCK - 1
    cos_c, sa_c, sb_c = _rope_tables(cend)
    kc, vc = _nsa_compress(
        groups(p_nsa[:, qw:qw + kvw]), groups(p_nsa[:, qw + kvw:qw + 2 * kvw]),
        cmp_pos_k[l].reshape(1, -1), cmp_pos_v[l].reshape(1, -1),
        cmp_k_w1[l], cmp_k_w2[l], cmp_v_w1[l], cmp_v_w2[l], nsa_k_norm[l],
        cos_c, sb_c - sa_c, _rot_half_matrix())
    nsel = seq // SEL_BLOCK
    y_nsa = _nsa_attn(q, kc, vc, ks, vs, kw, vw, gates, _overlap_matrix_t(ngrp, nsel),
                      batch=batch, seq=seq)

    x = _merge(x, o_rwkv, bonus, gate, y_nsa, p_gate, row(rwkv_gn_w[l]), row(rwkv_gn_b[l]), seg_w,
               w_branch_rwkv[l].astype(BF16), w_branch_nsa[l].astype(BF16), w_out[l].astype(BF16))
    return _ffn(x, row(ffn2_norm[l]), ffn2_w_gate[l].astype(BF16), ffn2_w_up[l].astype(BF16),
                ffn2_w_down[l].astype(BF16))


def kernel(x, ffn1_norm, ffn1_w_gate, ffn1_w_up, ffn1_w_down, mix_norm, w_in, rwkv_mix, rwkv_w0, rwkv_w_up, rwkv_a0, rwkv_a_up, rwkv_g_up, rwkv_k_k, rwkv_k_a, rwkv_r_k, rwkv_gn_w, rwkv_gn_b, nsa_q_norm, nsa_k_norm, cmp_pos_k, cmp_pos_v, cmp_k_w1, cmp_k_w2, cmp_v_w1, cmp_v_w2, w_branch_rwkv, w_branch_nsa, w_out, ffn2_norm, ffn2_w_gate, ffn2_w_up, ffn2_w_down):
    batch, seq, d = x.shape
    params = (ffn1_norm, ffn1_w_gate, ffn1_w_up, ffn1_w_down, mix_norm, w_in, rwkv_mix, rwkv_w0,
              rwkv_w_up, rwkv_a0, rwkv_a_up, rwkv_g_up, rwkv_k_k, rwkv_k_a, rwkv_r_k, rwkv_gn_w,
              rwkv_gn_b, nsa_q_norm, nsa_k_norm, cmp_pos_k, cmp_pos_v, cmp_k_w1, cmp_k_w2, cmp_v_w1,
              cmp_v_w2, w_branch_rwkv, w_branch_nsa, w_out, ffn2_norm, ffn2_w_gate, ffn2_w_up,
              ffn2_w_down)
    y = x.reshape(batch * seq, d)
    for l in range(ffn1_norm.shape[0]):
        y = _layer(y, l, *params, batch=batch, seq=seq)
    return y.reshape(batch, seq, d)
```

```python
import functools

import numpy as np
import jax
import jax.numpy as jnp
from jax import lax
from jax.experimental import pallas as pl
from jax.experimental.pallas import tpu as pltpu

F32 = jnp.float32
BF16 = jnp.bfloat16

RWKV_HEAD_DIM = 64
DECAY_LORA = 64
ICLR_LORA = 64
GATE_LORA = 160
GN_EPS = 64e-5
NSA_HEADS = 16
NSA_KV_HEADS = 4
NSA_GROUP = NSA_HEADS // NSA_KV_HEADS
NSA_HEAD_DIM = 64
ROPE_DIM = NSA_HEAD_DIM // 4
ROPE_HALF = ROPE_DIM // 2
ROPE_THETA = 500000.0
CMP_BLOCK = 32
CMP_STRIDE = 16
SEL_BLOCK = 64
SEL_TOP = 16
WINDOW = 512
NORM_EPS = 1e-6

LANE = 128
CHUNK = 64
VMEM_LIMIT = 56 * 1024 * 1024
MASKED = -1e30
LOG2_E = 1.4426950408889634
GATE_ROWS = 16
ATTN_KEY_BLOCK = 256
ATTN_KV_PER_STEP = 4


def _cparams(sem):
    return pltpu.CompilerParams(dimension_semantics=sem, vmem_limit_bytes=VMEM_LIMIT)


def _const_spec(shape):
    nd = len(shape)
    return pl.BlockSpec(shape, lambda *_: (0,) * nd, pipeline_mode=pl.Buffered(1))


def _mm(a, b):
    return lax.dot_general(a, b, (((1,), (0,)), ((), ())), preferred_element_type=F32)


def _mm_nt(a, b):
    return lax.dot_general(a, b, (((1,), (1,)), ((), ())), preferred_element_type=F32)


def _mm_tn(a, b):
    return lax.dot_general(a, b, (((0,), (0,)), ((), ())), preferred_element_type=F32)


def _split2(x):
    hi = x.astype(BF16)
    lo = (x - hi.astype(F32)).astype(BF16)
    return hi, lo


def _split3(x):
    h1 = x.astype(BF16)
    r1 = x - h1.astype(F32)
    h2 = r1.astype(BF16)
    h3 = (r1 - h2.astype(F32)).astype(BF16)
    return h1, h2, h3


def _dot3(a, b, mm=_mm):
    a1, a2 = _split2(a)
    b1, b2 = _split2(b)
    return mm(a1, b1) + (mm(a1, b2) + mm(a2, b1))


def _dot_exact_rhs(a, b_bf16):
    a1, a2 = _split2(a)
    return _mm(a1, b_bf16) + _mm(a2, b_bf16)


def _head_mean(x, seg_bf16, width):
    return _mm(x.astype(BF16), seg_bf16) * (1.0 / width)


def _dot_exact_lhs(a_bf16, b):
    b1, b2, b3 = _split3(b)
    return _mm(a_bf16, b1) + (_mm(a_bf16, b2) + _mm(a_bf16, b3))


def _sigmoid(x):
    return 1.0 / (1.0 + jnp.exp(-x))


def _iota(shape, dim):
    return lax.broadcasted_iota(jnp.int32, shape, dim)


def _ffn_kernel(x_ref, g_ref, wg_ref, wu_ref, wd_ref, o_ref, h_ref):
    j = pl.program_id(1)

    @pl.when(j == 0)
    def _():
        x = x_ref[...]
        ms = jnp.mean(x * x, axis=-1, keepdims=True)
        h_ref[...] = (x * lax.rsqrt(ms + NORM_EPS) * g_ref[...]).astype(BF16)
        o_ref[...] = jnp.zeros_like(o_ref)

    h = h_ref[...]
    gate = _mm(h, wg_ref[...])
    up = _mm(h, wu_ref[...])
    act = (gate * _sigmoid(gate) * up).astype(BF16)
    o_ref[...] += _mm(act, wd_ref[...])

    @pl.when(j == pl.num_programs(1) - 1)
    def _():
        o_ref[...] = x_ref[...] + 0.5 * o_ref[...]


def _ffn(x, g, wg, wu, wd, *, tm=1024, tf=512):
    t, d = x.shape
    f = wg.shape[1]
    return pl.pallas_call(
        _ffn_kernel,
        out_shape=jax.ShapeDtypeStruct((t, d), F32),
        grid=(t // tm, f // tf),
        in_specs=[
            pl.BlockSpec((tm, d), lambda i, j: (i, 0)),
            pl.BlockSpec((1, d), lambda i, j: (0, 0)),
            pl.BlockSpec((d, tf), lambda i, j: (0, j)),
            pl.BlockSpec((d, tf), lambda i, j: (0, j)),
            pl.BlockSpec((tf, d), lambda i, j: (j, 0)),
        ],
        out_specs=pl.BlockSpec((tm, d), lambda i, j: (i, 0)),
        scratch_shapes=[pltpu.VMEM((tm, d), BF16)],
        compiler_params=_cparams(("parallel", "arbitrary")),
        name="ffn",
    )(x, g, wg, wu, wd)


def _norm_proj_kernel(x_ref, g_ref, w_ref, o_ref):
    x = x_ref[...]
    ms = jnp.mean(x * x, axis=-1, keepdims=True)
    h = (x * lax.rsqrt(ms + NORM_EPS) * g_ref[...]).astype(BF16)
    o_ref[...] = _mm(h, w_ref[...]).astype(o_ref.dtype)


def _norm_proj(x, g, w, *, tm=256, name="norm_proj"):
    t, d = x.shape
    n = w.shape[1]
    return pl.pallas_call(
        _norm_proj_kernel,
        out_shape=jax.ShapeDtypeStruct((t, n), F32),
        grid=(t // tm,),
        in_specs=[
            pl.BlockSpec((tm, d), lambda i: (i, 0)),
            _const_spec((1, d)),
            _const_spec((d, n)),
        ],
        out_specs=pl.BlockSpec((tm, n), lambda i: (i, 0)),
        compiler_params=_cparams(("parallel",)),
        name=name,
    )(x, g, w)


def _rwkv_prep_kernel(p_ref, prev_ref, mix_ref, wwa_ref, gup_ref, w0_ref, a0_ref, kk_ref, ka_ref,
                      rk_ref, seg_ref, tri_ref,
                      rt_ref, at_ref, kt_ref, bt_ref, kh_ref, bh_ref, v_ref, dc_ref, bonus_ref,
                      gate_ref, *, seq_tiles):
    i = pl.program_id(0)
    ts = p_ref.shape[0]
    w = rt_ref.shape[1]
    p = p_ref[...]
    prev = prev_ref[7:8, :]
    prev = jnp.where(i % seq_tiles == 0, jnp.zeros_like(prev), prev)
    shifted = pltpu.roll(p, 1, axis=0)
    shifted = jnp.where(_iota(p.shape, 0) == 0, prev, shifted)
    xs = p + mix_ref[...] * (shifted - p)

    r = xs[:, 0:w]
    k = xs[:, w:2 * w]
    v = xs[:, 2 * w:3 * w]
    lo = 3 * w
    pwa = xs[:, lo:lo + LANE]
    pg = xs[:, lo + LANE:lo + 3 * LANE]
    lane = _iota(pwa.shape, 1)
    z = jnp.where(lane < DECAY_LORA, jnp.tanh(pwa), pwa)
    wa = _dot3(z, wwa_ref[...])
    wl = w0_ref[...] + wa[:, :w]
    neg = -wl
    softplus = jnp.maximum(neg, 0.0) + jnp.log(1.0 + jnp.exp(-jnp.abs(neg)))
    lw = -jnp.exp(-softplus - 0.5)
    a = _sigmoid(a0_ref[...] + wa[:, w:])
    gate_ref[...] = _dot3(_sigmoid(pg), gup_ref[...])

    seg = seg_ref[...]
    kk = k * kk_ref[...]
    ss = _dot_exact_rhs(kk * kk, seg)
    kk = kk * lax.rsqrt(jnp.maximum(ss, 1e-24))
    k2 = k * (1.0 + (a - 1.0) * ka_ref[...])
    bonus_ref[...] = _dot_exact_rhs(r * k2 * rk_ref[...], seg) * v

    gc = _dot_exact_lhs(tri_ref[...], lw)
    nc = ts // CHUNK
    ends = [gc[(q + 1) * CHUNK - 1:(q + 1) * CHUNK, :] for q in range(nc)]
    gend = jnp.concatenate([jnp.broadcast_to(e, (CHUNK, w)) for e in ends], axis=0)
    to_end = jnp.exp(gend - gc)
    e_in = jnp.exp(gc)
    e_out = jnp.exp(-gc)
    b = kk * a
    rt_ref[...] = r * e_in
    at_ref[...] = -kk * jnp.exp(gc - lw)
    kt_ref[...] = k2 * e_out
    bt_ref[...] = b * e_out
    kh_ref[...] = k2 * to_end
    bh_ref[...] = b * to_end
    v_ref[...] = v
    dc_ref[0] = jnp.concatenate([jnp.exp(e) for e in ends] + [jnp.zeros((8 - nc, w), F32)], axis=0)


def _rwkv_prep(p_rwkv, mix, wwa, gup, w0, a0, k_k, k_a, r_k, seg, tri, *, seq, ts=256):
    t, pc = p_rwkv.shape
    w = w0.shape[1]
    nt = t // ts
    row = lambda i: (i, 0)
    tok = pl.BlockSpec((ts, w), row)
    tok_shape = jax.ShapeDtypeStruct((t, w), F32)
    return pl.pallas_call(
        functools.partial(_rwkv_prep_kernel, seq_tiles=seq // ts),
        out_shape=[tok_shape] * 7 + [jax.ShapeDtypeStruct((nt, 8, w), F32), tok_shape, tok_shape],
        grid=(nt,),
        in_specs=[
            pl.BlockSpec((ts, pc), row),
            pl.BlockSpec((8, pc), lambda i: (jnp.maximum(i * (ts // 8) - 1, 0), 0)),
            _const_spec((1, pc)),
            _const_spec(wwa.shape),
            _const_spec(gup.shape),
            _const_spec((1, w)), _const_spec((1, w)), _const_spec((1, w)), _const_spec((1, w)),
            _const_spec((1, w)),
            _const_spec(seg.shape),
            _const_spec(tri.shape),
        ],
        out_specs=[tok] * 7 + [pl.BlockSpec((1, 8, w), lambda i: (i, 0, 0)), tok, tok],
        compiler_params=_cparams(("parallel",)),
        name="rwkv_prep",
    )(p_rwkv, p_rwkv, mix, wwa, gup, w0, a0, k_k, k_a, r_k, seg, tri)


PASSES_SCAN = 1
SCAN_CHUNKS_PER_STEP = 4
INTRA_CHUNKS_PER_STEP = 4
QUAD = 4


def _operand(x, passes):
    return _split2(x) if passes == 3 else (x.astype(BF16),)


def _prod(a, b, mm=_mm):
    if len(a) == 2 and len(b) == 2:
        return mm(a[0], b[0]) + (mm(a[0], b[1]) + mm(a[1], b[0]))
    return mm(a[0], b[0])


def _block_diag(y, n):
    c = y.shape[0]
    tiled = jnp.concatenate([y] * (y.shape[1] // n), axis=0)
    keep = (_iota(tiled.shape, 0) // c) == (_iota(tiled.shape, 1) // n)
    return jnp.where(keep, tiled, jnp.zeros_like(tiled))


def _quad_mm(x, y, n, mm=_mm):
    return mm(x.astype(BF16), _block_diag(y.astype(BF16), n))


def _unit_lower_inverse(a_list, row, col):
    n = CHUNK
    eye = (row == col).astype(F32)
    same8 = (row // 8) == (col // 8)
    a8 = [jnp.where(same8, a, 0.0) for a in a_list]
    d8 = [_block_diag(x.astype(BF16), n) for x in a8]
    a8_2 = [_mm(x.astype(BF16), d) for x, d in zip(a8, d8)]
    d8_2 = [_block_diag(x.astype(BF16), n) for x in a8_2]
    a8_4 = [_mm(x.astype(BF16), d) for x, d in zip(a8_2, d8_2)]
    p = [eye + x + x2 + _mm(x.astype(BF16), d2) for x, x2, d2 in zip(a8, a8_2, d8_2)]
    t = [pp + _quad_mm(pp, x4, n) for pp, x4 in zip(p, a8_4)]
    m = 16
    while m <= CHUNK:
        sel = ((row // m) == (col // m)) & ((row // (m // 2)) != (col // (m // 2)))
        mid = [_quad_mm(x, jnp.where(sel, a, 0.0), n) for x, a in zip(t, a_list)]
        t = [x + _quad_mm(md, x, n) for x, md in zip(t, mid)]
        m *= 2
    return t


def _rwkv_intra_kernel(rt_ref, at_ref, kt_ref, bt_ref, kh_ref, bh_ref, v_ref, dc_ref,
                       rr_ref, o0_ref, gh_ref, *, heads, chunks_per_tile):
    n = RWKV_HEAD_DIM
    qw = QUAD * n
    cps = gh_ref.shape[0]
    i = pl.program_id(0)
    row = _iota((CHUNK, qw), 0)
    col = _iota((CHUNK, qw), 1) % n
    strict = col < row
    incl = col <= row
    dc_all = dc_ref[0]
    dc_rows = []
    for c in range(cps):
        r = dc_all[c:c + 1, :]
        for q in range(1, chunks_per_tile // cps):
            r = jnp.where(i % (chunks_per_tile // cps) == q, dc_all[q * cps + c:q * cps + c + 1, :], r)
        dc_rows.append(r)

    units = [(slice(c * CHUNK, (c + 1) * CHUNK), slice(j * qw, (j + 1) * qw))
             for c in range(cps) for j in range(heads // QUAD)]
    at = [at_ref[r, s] for r, s in units]
    rt = [rt_ref[r, s] for r, s in units]
    v = [v_ref[r, s] for r, s in units]
    bd_b = [_block_diag(bt_ref[r, s].astype(BF16), n) for r, s in units]
    bd_k = [_block_diag(kt_ref[r, s].astype(BF16), n) for r, s in units]
    ar = [jnp.concatenate([a, r], axis=0).astype(BF16) for a, r in zip(at, rt)]
    mb = [_mm_nt(x, d) for x, d in zip(ar, bd_b)]
    mk = [_mm_nt(x, d) for x, d in zip(ar, bd_k)]
    a_ab = [jnp.where(strict, m[:CHUNK], 0.0) for m in mb]
    a_rb = [jnp.where(incl, m[CHUNK:], 0.0) for m in mb]
    akrk = [jnp.concatenate([jnp.where(strict, m[:CHUNK], 0.0), jnp.where(incl, m[CHUNK:], 0.0)], axis=0)
            for m in mk]
    avv = [_quad_mm(x, y, n) for x, y in zip(akrk, v)]
    tinv = _unit_lower_inverse(a_ab, row, col)
    a_new = [_quad_mm(t, a, n) for t, a in zip(tinv, at)]
    u0 = [_quad_mm(t, w[:CHUNK], n) for t, w in zip(tinv, avv)]
    for u, (r, s) in enumerate(units):
        rr_ref[r, s] = rt[u] + _quad_mm(a_rb[u], a_new[u], n)
        o0_ref[r, s] = _quad_mm(a_rb[u], u0[u], n) + avv[u][CHUNK:]
    eye = _iota((n, n), 0) == _iota((n, n), 1)
    nq = heads // QUAD
    gz, kv = [], []
    for c in range(cps):
        r = slice(c * CHUNK, (c + 1) * CHUNK)
        for h in range(heads):
            u, s = c * nq + h // QUAD, slice((h % QUAD) * n, (h % QUAD + 1) * n)
            hs = slice(h * n, (h + 1) * n)
            z = jnp.concatenate([a_new[u][:, s], u0[u][:, s]], axis=1).astype(BF16)
            gz.append(_mm_tn(bh_ref[r, hs].astype(BF16), z))
            kv.append(_mm_tn(kh_ref[r, hs].astype(BF16), v[u][:, s].astype(BF16)))
    for c in range(cps):
        for h in range(heads):
            hs = slice(h * n, (h + 1) * n)
            dmat = jnp.where(eye, jnp.broadcast_to(dc_rows[c][:, hs], (n, n)), 0.0)
            gh_ref[c, h] = gz[c * heads + h] + jnp.concatenate([dmat, kv[c * heads + h]], axis=1)


def _rwkv_scan_kernel(rr_ref, o0_ref, gh_ref, o_ref, state_ref, *, heads):
    n = RWKV_HEAD_DIM

    @pl.when(pl.program_id(1) == 0)
    def _():
        state_ref[...] = jnp.zeros_like(state_ref)

    sls = [slice(h * n, (h + 1) * n) for h in range(heads)]
    state = [state_ref[h] for h in range(heads)]
    for c in range(gh_ref.shape[0]):
        rows = slice(c * CHUNK, (c + 1) * CHUNK)
        h0 = [_operand(x, PASSES_SCAN) for x in state]
        outs = [_prod(_operand(rr_ref[rows, s], PASSES_SCAN), x) + o0_ref[rows, s] for s, x in zip(sls, h0)]
        state = [_prod(_operand(gh_ref[c, h, :, 0:n], PASSES_SCAN), h0[h]) + gh_ref[c, h, :, n:2 * n]
                 for h in range(heads)]
        for h, s in enumerate(sls):
            o_ref[rows, s] = outs[h]
    for h in range(heads):
        state_ref[h] = state[h]


def _rwkv_chunk(rt, at, kt, bt, kh, bh, v, dc, *, batch, seq, prep_ts):
    t, w = rt.shape
    n = RWKV_HEAD_DIM
    heads = w // n
    nchunk = seq // CHUNK
    cpt = prep_ts // CHUNK
    cps = INTRA_CHUNKS_PER_STEP
    tok = pl.BlockSpec((cps * CHUNK, w), lambda i: (i, 0))
    tok_shape = jax.ShapeDtypeStruct((t, w), F32)
    rr, o0, gh = pl.pallas_call(
        functools.partial(_rwkv_intra_kernel, heads=heads, chunks_per_tile=cpt),
        out_shape=[tok_shape, tok_shape, jax.ShapeDtypeStruct((t // CHUNK, heads, n, 2 * n), F32)],
        grid=(t // (cps * CHUNK),),
        in_specs=[tok] * 7 + [pl.BlockSpec((1, 8, w), lambda i: (i * cps // cpt, 0, 0))],
        out_specs=[tok, tok, pl.BlockSpec((cps, heads, n, 2 * n), lambda i: (i, 0, 0, 0))],
        compiler_params=_cparams(("parallel",)),
        name="rwkv_intra",
    )(rt, at, kt, bt, kh, bh, v, dc)
    steps = nchunk // SCAN_CHUNKS_PER_STEP
    tok2 = pl.BlockSpec((SCAN_CHUNKS_PER_STEP * CHUNK, w), lambda b, c: (b * steps + c, 0))
    return pl.pallas_call(
        functools.partial(_rwkv_scan_kernel, heads=heads),
        out_shape=tok_shape,
        grid=(batch, steps),
        in_specs=[tok2, tok2, pl.BlockSpec((SCAN_CHUNKS_PER_STEP, heads, n, 2 * n),
                                           lambda b, c: (b * steps + c, 0, 0, 0))],
        out_specs=tok2,
        scratch_shapes=[pltpu.VMEM((heads, n, n), F32)],
        compiler_params=_cparams(("parallel", "arbitrary")),
        name="rwkv_scan",
    )(rr, o0, gh)


def _rope_lanes(x, cos_t, sin_a, sin_b):
    width = x.shape[1]
    up = pltpu.roll(x, width - ROPE_HALF, axis=1)
    dn = pltpu.roll(x, ROPE_HALF, axis=1)
    return x * cos_t + up * sin_a + dn * sin_b


def _tile_lanes(tab, width):
    return jnp.concatenate([tab] * (width // tab.shape[1]), axis=1)


def _nsa_prep_kernel(p_ref, cos_ref, sa_ref, sb_ref, qn_ref, kn_ref, seg_ref,
                     q_ref, ks_ref, vs_ref, kw_ref, vw_ref, g_ref, gk_ref, gv_ref, *, seq_tiles):
    dh = NSA_HEAD_DIM
    qw = q_ref.shape[1]
    kvw = NSA_KV_HEADS * dh
    seg = seg_ref[...]
    cos_t, sin_a, sin_b = cos_ref[...], sa_ref[...], sb_ref[...]

    def norm_rope(x, gain):
        wd = x.shape[1]
        ms = _head_mean(x * x, seg[:wd, :wd], dh)
        y = x * lax.rsqrt(ms + NORM_EPS) * gain
        return _rope_lanes(y, _tile_lanes(cos_t, wd), _tile_lanes(sin_a, wd), _tile_lanes(sin_b, wd))

    q = norm_rope(p_ref[:, 0:qw], qn_ref[...])
    q_ref[...] = q * (dh ** -0.5)
    base = qw + 2 * kvw
    ks = norm_rope(p_ref[:, base:base + kvw], kn_ref[1:2, :])
    vs = p_ref[:, base + kvw:base + 2 * kvw].astype(BF16)
    kw = norm_rope(p_ref[:, base + 2 * kvw:base + 3 * kvw], kn_ref[2:3, :]).astype(BF16)
    vw = p_ref[:, base + 3 * kvw:base + 4 * kvw].astype(BF16)
    ts = p_ref.shape[0]
    tpos = (pl.program_id(0) % seq_tiles) * ts + _iota((ts, dh), 0)
    onehot = jnp.where(tpos // SEL_BLOCK == _iota((ts, dh), 1), 1.0, 0.0)
    for h in range(NSA_KV_HEADS):
        sl = slice(h * dh, (h + 1) * dh)
        ks_ref[0, h] = jnp.concatenate([ks[:, sl], onehot], axis=1).astype(BF16)
        vs_ref[0, h] = vs[:, sl]
        kw_ref[0, h] = kw[:, sl]
        vw_ref[0, h] = vw[:, sl]
    sig = _sigmoid(p_ref[:, base + 4 * kvw:base + 4 * kvw + LANE])
    sig_t = sig.T
    per_head = 3 * NSA_GROUP
    for h in range(NSA_KV_HEADS):
        g_ref[h] = sig_t[per_head * h:per_head * h + GATE_ROWS, :]
    kc3 = p_ref[:, qw:qw + kvw].reshape(ts // CMP_STRIDE, CMP_STRIDE, kvw)
    vc3 = p_ref[:, qw + kvw:qw + 2 * kvw].reshape(ts // CMP_STRIDE, CMP_STRIDE, kvw)
    for i in range(CMP_STRIDE):
        kci, vci = kc3[:, i, :], vc3[:, i, :]
        for h in range(NSA_KV_HEADS):
            gk_ref[0, h, :, i * dh:(i + 1) * dh] = kci[:, h * dh:(h + 1) * dh]
            gv_ref[0, h, :, i * dh:(i + 1) * dh] = vci[:, h * dh:(h + 1) * dh]


def _nsa_prep(p_nsa, cos_t, sin_a, sin_b, qn, kn, seg, *, batch, seq, ts=256):
    t, pc = p_nsa.shape
    qw = NSA_HEADS * NSA_HEAD_DIM
    st = seq // ts
    tab = pl.BlockSpec((ts, LANE), lambda i: (i % st, 0))
    hm = pl.BlockSpec((1, NSA_KV_HEADS, ts, NSA_HEAD_DIM), lambda i: (i // st, 0, i % st, 0))
    hm_shape = jax.ShapeDtypeStruct((batch, NSA_KV_HEADS, seq, NSA_HEAD_DIM), BF16)
    assert seq // SEL_BLOCK <= NSA_HEAD_DIM
    aug = pl.BlockSpec((1, NSA_KV_HEADS, ts, 2 * NSA_HEAD_DIM), lambda i: (i // st, 0, i % st, 0))
    aug_shape = jax.ShapeDtypeStruct((batch, NSA_KV_HEADS, seq, 2 * NSA_HEAD_DIM), BF16)
    grp_w = CMP_STRIDE * NSA_HEAD_DIM
    grp = pl.BlockSpec((1, NSA_KV_HEADS, ts // CMP_STRIDE, grp_w), lambda i: (i // st, 0, i % st, 0))
    grp_shape = jax.ShapeDtypeStruct((batch, NSA_KV_HEADS, seq // CMP_STRIDE, grp_w), F32)
    return pl.pallas_call(
        functools.partial(_nsa_prep_kernel, seq_tiles=st),
        out_shape=[jax.ShapeDtypeStruct((t, qw), F32), aug_shape] + [hm_shape] * 3
        + [jax.ShapeDtypeStruct((NSA_KV_HEADS, GATE_ROWS, t), F32)] + [grp_shape] * 2,
        grid=(t // ts,),
        in_specs=[pl.BlockSpec((ts, pc), lambda i: (i, 0)), tab, tab, tab,
                  _const_spec(qn.shape), _const_spec(kn.shape), _const_spec(seg.shape)],
        out_specs=[pl.BlockSpec((ts, qw), lambda i: (i, 0)), aug] + [hm] * 3
        + [pl.BlockSpec((NSA_KV_HEADS, GATE_ROWS, ts), lambda i: (0, 0, i))] + [grp] * 2,
        compiler_params=_cparams(("parallel",)),
        name="nsa_prep",
    )(p_nsa, cos_t, sin_a, sin_b, qn, kn, seg)


def _gelu_tanh(x):
    return 0.5 * x * (1.0 + jnp.tanh(np.sqrt(2.0 / np.pi).astype(np.float32) * (x + 0.044715 * (x * x * x))))


def _compress_kernel(gk_ref, gv_ref, pk_ref, pv_ref, k1_ref, k2_ref, v1_ref, v2_ref, kn_ref,
                     cos_ref, sin_ref, rot_ref, kc_ref, vc_ref):
    half = k1_ref.shape[0] // 2

    def mlp(g, pos, w1_ref, w2_ref):
        ya = _dot3(g, w1_ref[0:half, :])
        yb = _dot3(g, w1_ref[half:, :])
        bias = _dot3(jnp.broadcast_to(pos, (8, pos.shape[1])), w1_ref[...])[0:1, :]
        n = g.shape[0]
        hid = ya + pltpu.roll(yb, n - 1, axis=0) + bias
        return _dot3(_gelu_tanh(hid), w2_ref[...])

    kc = mlp(gk_ref[0, 0], pk_ref[...], k1_ref, k2_ref)
    ms = jnp.mean(kc * kc, axis=-1, keepdims=True)
    kc = kc * lax.rsqrt(ms + NORM_EPS) * kn_ref[0:1, :]
    kc_ref[0, 0] = kc * cos_ref[...] + _dot_exact_rhs(kc, rot_ref[...]) * sin_ref[...]
    vc_ref[0, 0] = mlp(gv_ref[0, 0], pv_ref[...], v1_ref, v2_ref)


def _nsa_compress(gk, gv, pk, pv, k1, k2, v1, v2, kn, cos_c, sin_c, rot):
    b, hk, ng, gw = gk.shape
    dh = NSA_HEAD_DIM
    grp = pl.BlockSpec((1, 1, ng, gw), lambda i, j: (i, j, 0, 0))
    out = pl.BlockSpec((1, 1, ng, dh), lambda i, j: (i, j, 0, 0))
    shape = jax.ShapeDtypeStruct((b, hk, ng, dh), F32)
    consts = [pk, pv, k1, k2, v1, v2, kn, cos_c, sin_c, rot]
    return pl.pallas_call(
        _compress_kernel,
        out_shape=[shape, shape],
        grid=(b, hk),
        in_specs=[grp, grp] + [_const_spec(c.shape) for c in consts],
        out_specs=[out, out],
        compiler_params=_cparams(("parallel", "parallel")),
        name="nsa_compress",
    )(gk, gv, *consts)


def _nsa_attn_kernel(q_ref, kc_ref, vc_ref, ks_ref, vs_ref, kw_ref, vw_ref, g_ref, ovt_ref, o_ref):
    dh = NSA_HEAD_DIM
    grp = NSA_GROUP
    nh = kc_ref.shape[1]
    qi = pl.program_id(2)
    tq = q_ref.shape[0]
    hc = grp * tq
    cols = nh * hc
    t0 = qi * tq
    ncmp = kc_ref.shape[2]
    nsel = ovt_ref.shape[0]
    kb = ATTN_KEY_BLOCK
    span = WINDOW + tq
    heads = range(nh)

    def lanes(xs):
        return jnp.concatenate(xs, axis=1)

    def every_head(x):
        return lanes([x] * (nh * grp))

    q = q_ref[...]
    q4 = [jnp.concatenate([q[:, (h * grp + g) * dh:(h * grp + g + 1) * dh] for g in range(grp)], axis=0)
          for h in heads]
    q4b = [(x * LOG2_E).astype(BF16) for x in q4]

    st = lanes([_dot3(kc_ref[0, h], q4[h], _mm_nt) for h in heads])
    tl = t0 + _iota((ncmp, cols), 1) % tq
    cmask = _iota((ncmp, cols), 0) * CMP_STRIDE + (CMP_BLOCK - 1) <= tl
    sm = jnp.where(cmask, st, MASKED)
    e = jnp.where(cmask, jnp.exp(sm - jnp.max(sm, axis=0, keepdims=True)), 0.0)
    pt = e / jnp.maximum(jnp.sum(e, axis=0, keepdims=True), 1e-30)
    ptb = pt.astype(BF16)
    o_cmp = lanes([_mm_tn(vc_ref[0, h].astype(BF16), ptb[:, h * hc:(h + 1) * hc]) for h in heads])
    psum = []
    for h in heads:
        acc = pt[:, h * hc:h * hc + tq]
        for g in range(1, grp):
            acc = acc + pt[:, h * hc + g * tq:h * hc + (g + 1) * tq]
        psum.append(acc)
    psum = lanes(psum)

    imp = _dot_exact_lhs(ovt_ref[...], psum)
    blk = _iota(imp.shape, 0)
    cur = (t0 + _iota(imp.shape, 1) % tq) // SEL_BLOCK
    forced = (blk == 0) | (blk == cur) | (blk == cur - 1)
    imp = jnp.where(forced, jnp.inf, jnp.where(blk > cur, -jnp.inf, imp))
    rank = jnp.zeros(imp.shape, jnp.int32)
    for m in range(nsel):
        im = imp[m:m + 1, :]
        ahead = (im > imp) | ((im == imp) & (m < blk))
        rank = rank + ahead.astype(jnp.int32)
    sel_bias = jnp.where(rank < min(SEL_TOP, nsel), 0.0, MASKED).astype(BF16)

    eye = jnp.where(_iota((nsel, dh), 0) == _iota((nsel, dh), 1), 1.0, 0.0).astype(BF16)
    bias_q = _mm_tn(sel_bias, eye)
    q_aug = [jnp.concatenate([q4[h] * LOG2_E, jnp.concatenate([bias_q[h * tq:(h + 1) * tq]] * grp, axis=0)],
                             axis=1).astype(BF16) for h in heads]

    def attend(k_ref, v_ref, qs, k0, state, bias=None):
        m_run, l_run, acc = state
        rows = pl.ds(pl.multiple_of(k0, kb), kb)
        s = lanes([_mm_nt(k_ref[0, h, rows, :], qs[h]) for h in heads])
        if bias is not None:
            s = s + bias
        m_new = jnp.maximum(m_run, jnp.max(s, axis=0, keepdims=True))
        alpha = jnp.exp2(m_run - m_new)
        pb = jnp.exp2(s - m_new)
        l_new = alpha * l_run + jnp.sum(pb, axis=0, keepdims=True)
        pb = pb.astype(BF16)
        pv = lanes([_mm_tn(v_ref[0, h, rows, :], pb[:, h * hc:(h + 1) * hc]) for h in heads])
        return m_new, l_new, alpha * acc + pv

    init = (jnp.full((1, cols), MASKED, F32), jnp.zeros((1, cols), F32), jnp.zeros((dh, cols), F32))
    state = lax.fori_loop(0, t0 // kb, lambda j, st: attend(ks_ref, vs_ref, q_aug, j * kb, st), init)
    for d in range(tq // kb):
        causal = jnp.where(d * kb + _iota((kb, tq), 0) <= _iota((kb, tq), 1), 0.0, MASKED)
        state = attend(ks_ref, vs_ref, q_aug, t0 + d * kb, state, every_head(causal))
    o_slc = state[2] / state[1]

    w0 = pl.multiple_of(jnp.maximum(t0 - WINDOW, 0), tq)
    wrows = pl.ds(w0, span)
    kpos = w0 + _iota((span, tq), 0)
    tw = t0 + _iota((span, tq), 1)
    wbias = jnp.where((kpos <= tw) & (kpos > tw - WINDOW), 0.0, MASKED)
    s = lanes([_mm_nt(kw_ref[0, h, wrows, :], q4b[h]) for h in heads]) + every_head(wbias)
    p = jnp.exp2(s - jnp.max(s, axis=0, keepdims=True))
    pb = p.astype(BF16)
    o_win = (lanes([_mm_tn(vw_ref[0, h, wrows, :], pb[:, h * hc:(h + 1) * hc]) for h in heads])
             / jnp.sum(p, axis=0, keepdims=True))

    gates = g_ref[...]
    grow = [lanes([gates[h, 3 * g + br:3 * g + br + 1, :] for h in heads for g in range(grp)])
            for br in range(3)]
    o4 = grow[0] * o_cmp + grow[1] * o_slc + grow[2] * o_win
    o_ref[...] = jnp.concatenate([o4[:, c * tq:(c + 1) * tq] for c in range(nh * grp)], axis=0)


def _nsa_attn(q, kc, vc, ks_aug, vs, kw, vw, gates_t, overlap_t, *, batch, seq, tq=256,
              nh=ATTN_KV_PER_STEP):
    t, qw = q.shape
    dh = NSA_HEAD_DIM
    gw = nh * NSA_GROUP * dh
    st = seq // tq
    ncmp = kc.shape[2]
    cmp_spec = pl.BlockSpec((1, nh, ncmp, dh), lambda b, h, i: (b, h, 0, 0))
    kv_spec = pl.BlockSpec((1, nh, seq, dh), lambda b, h, i: (b, h, 0, 0))
    aug_spec = pl.BlockSpec((1, nh, seq, 2 * dh), lambda b, h, i: (b, h, 0, 0))
    return pl.pallas_call(
        _nsa_attn_kernel,
        out_shape=jax.ShapeDtypeStruct((qw, t), F32),
        grid=(batch, NSA_KV_HEADS // nh, st),
        in_specs=[
            pl.BlockSpec((tq, gw), lambda b, h, i: (b * st + i, h)),
            cmp_spec, cmp_spec, aug_spec, kv_spec, kv_spec, kv_spec,
            pl.BlockSpec((nh, gates_t.shape[1], tq), lambda b, h, i: (h, 0, b * st + i)),
            _const_spec(overlap_t.shape),
        ],
        out_specs=pl.BlockSpec((gw, tq), lambda b, h, i: (h, b * st + i)),
        compiler_params=_cparams(("parallel", "parallel", "arbitrary")),
        name="nsa_attn",
    )(q, kc, vc, ks_aug, vs, kw, vw, gates_t, overlap_t)


def _merge_kernel(x_ref, o_ref, bonus_ref, gate_ref, ybt_ref, pg_ref, gnw_ref, gnb_ref, seg_ref,
                  ua_ref, ub_ref, wo_ref, out_ref):
    d = x_ref.shape[1]
    n = RWKV_HEAD_DIM
    seg = seg_ref[...]
    o = o_ref[...]
    mu = _dot_exact_rhs(o, seg) * (1.0 / n)
    dlt = o - mu
    var = _head_mean(dlt * dlt, seg, n)
    on = dlt * lax.rsqrt(var + GN_EPS) * gnw_ref[...] + gnb_ref[...]
    ya = ((on + bonus_ref[...]) * gate_ref[...]).astype(BF16)
    yb_t = ybt_ref[...].astype(BF16)
    merged = (_sigmoid(pg_ref[:, 0:d]) * _mm(ya, ua_ref[...])
              + _sigmoid(pg_ref[:, d:2 * d]) * _mm_tn(yb_t, ub_ref[...]))
    out_ref[...] = x_ref[...] + _mm(merged.astype(BF16), wo_ref[...])


def _merge(x, o_rwkv, bonus, gate, yb_t, pg, gnw, gnb, seg, ua, ub, wo, *, tm=256):
    t, d = x.shape
    w = o_rwkv.shape[1]
    row = lambda i: (i, 0)
    tokw = pl.BlockSpec((tm, w), row)
    return pl.pallas_call(
        _merge_kernel,
        out_shape=jax.ShapeDtypeStruct((t, d), F32),
        grid=(t // tm,),
        in_specs=[pl.BlockSpec((tm, d), row), tokw, tokw, tokw,
                  pl.BlockSpec((yb_t.shape[0], tm), lambda i: (0, i)),
                  pl.BlockSpec((tm, 2 * d), row),
                  _const_spec((1, w)), _const_spec((1, w)), _const_spec(seg.shape),
                  _const_spec(ua.shape), _const_spec(ub.shape), _const_spec(wo.shape)],
        out_specs=pl.BlockSpec((tm, d), row),
        compiler_params=_cparams(("parallel",)),
        name="merge",
    )(x, o_rwkv, bonus, gate, yb_t, pg, gnw, gnb, seg, ua, ub, wo)


def _block_diag_ones(width, block):
    idx = np.arange(width) // block
    return jnp.asarray(idx[:, None] == idx[None, :], BF16)


def _chunk_lower_ones(ts):
    i = np.arange(ts)
    return jnp.asarray((i[:, None] // CHUNK == i[None, :] // CHUNK) & (i[None, :] <= i[:, None]), BF16)


def _rope_tables(pos):
    inv = ROPE_THETA ** (-jnp.arange(ROPE_HALF, dtype=F32) / ROPE_HALF)
    ang = jnp.asarray(pos).astype(F32)[:, None] * inv[None, :]
    cos, sin = jnp.cos(ang), jnp.sin(ang)
    n = ang.shape[0]
    pad = jnp.zeros((n, NSA_HEAD_DIM - ROPE_DIM), F32)
    zero = jnp.zeros_like(sin)
    cos_h = jnp.concatenate([cos, cos, pad + 1.0], axis=1)
    sa_h = jnp.concatenate([-sin, zero, pad], axis=1)
    sb_h = jnp.concatenate([zero, sin, pad], axis=1)
    return cos_h, sa_h, sb_h


def _rot_half_matrix():
    r = np.zeros((NSA_HEAD_DIM, NSA_HEAD_DIM), np.float32)
    for l in range(ROPE_HALF):
        r[l + ROPE_HALF, l] = -1.0
        r[l, l + ROPE_HALF] = 1.0
    return jnp.asarray(r, BF16)


def _overlap_matrix_t(ncmp_pad, nsel):
    cs = np.arange(ncmp_pad)[None, :] * CMP_STRIDE
    ss = np.arange(nsel)[:, None] * SEL_BLOCK
    ov = np.clip(np.minimum(cs + CMP_BLOCK, ss + SEL_BLOCK) - np.maximum(cs, ss), 0, None) / CMP_BLOCK
    return jnp.asarray(ov, BF16)


def _pad_cols(x, width):
    return jnp.pad(x, ((0, 0), (0, width - x.shape[1])))


def _layer(x, l, ffn1_norm, ffn1_w_gate, ffn1_w_up, ffn1_w_down, mix_norm, w_in,
           rwkv_mix, rwkv_w0, rwkv_w_up, rwkv_a0, rwkv_a_up, rwkv_g_up,
           rwkv_k_k, rwkv_k_a, rwkv_r_k, rwkv_gn_w, rwkv_gn_b,
           nsa_q_norm, nsa_k_norm, cmp_pos_k, cmp_pos_v,
           cmp_k_w1, cmp_k_w2, cmp_v_w1, cmp_v_w2,
           w_branch_rwkv, w_branch_nsa, w_out,
           ffn2_norm, ffn2_w_gate, ffn2_w_up, ffn2_w_down, *, batch, seq):
    t, d = x.shape
    w = rwkv_w0.shape[1]
    dh = NSA_HEAD_DIM
    qw = NSA_HEADS * dh
    kvw = NSA_KV_HEADS * dh
    prep_ts = 256
    row = lambda v: v.reshape(1, -1)

    x = _ffn(x, row(ffn1_norm[l]), ffn1_w_gate[l].astype(BF16), ffn1_w_up[l].astype(BF16),
             ffn1_w_down[l].astype(BF16))

    wi = w_in[l]
    rwkv_cols = 3 * w + DECAY_LORA + ICLR_LORA + GATE_LORA
    rwkv_pad = 3 * w + 3 * LANE
    nsa_cols = qw + 6 * kvw + 3 * NSA_HEADS
    nsa_pad = qw + 6 * kvw + LANE
    g_mix = row(mix_norm[l])
    p_rwkv = _norm_proj(x, g_mix, _pad_cols(wi[:, :rwkv_cols], rwkv_pad).astype(BF16), name="proj_rwkv")
    p_nsa = _norm_proj(x, g_mix, _pad_cols(wi[:, rwkv_cols:rwkv_cols + nsa_cols], nsa_pad).astype(BF16),
                       name="proj_nsa")
    p_gate = _norm_proj(x, g_mix, wi[:, rwkv_cols + nsa_cols:].astype(BF16), name="proj_gate")

    wwa = jnp.zeros((LANE, 2 * w), F32)
    wwa = wwa.at[:DECAY_LORA, :w].set(rwkv_w_up[l]).at[DECAY_LORA:, w:].set(rwkv_a_up[l])
    gup = jnp.pad(rwkv_g_up[l], ((0, 2 * LANE - GATE_LORA), (0, 0)))
    seg_w = _block_diag_ones(w, RWKV_HEAD_DIM)
    (rt, at, kt, bt, kh, bh, v, dc, bonus, gate) = _rwkv_prep(
        p_rwkv, _pad_cols(row(rwkv_mix[l]), rwkv_pad), wwa, gup, row(rwkv_w0[l]), row(rwkv_a0[l]),
        row(rwkv_k_k[l]), row(rwkv_k_a[l]), row(rwkv_r_k[l]), seg_w, _chunk_lower_ones(prep_ts),
        seq=seq, ts=prep_ts)
    o_rwkv = _rwkv_chunk(rt, at, kt, bt, kh, bh, v, dc, batch=batch, seq=seq, prep_ts=prep_ts)

    cos_t, sin_a, sin_b = _rope_tables(np.arange(seq))
    two = lambda tab: jnp.concatenate([tab, tab], axis=1)
    qn = jnp.tile(row(nsa_q_norm[l]), (1, NSA_HEADS))
    kn = jnp.tile(nsa_k_norm[l], (1, NSA_KV_HEADS))
    q, ks, vs, kw, vw, gates, grp_k, grp_v = _nsa_prep(
        p_nsa, two(cos_t), two(sin_a), two(sin_b), qn, kn, _block_diag_ones(qw, dh), batch=batch, seq=seq)

    ngrp = seq // CMP_STRIDE
    cend = np.arange(ngrp) * CMP_STRIDE + CMP_BLOCK - 1
    cos_c, sa_c, sb_c = _rope_tables(cend)
    kc, vc = _nsa_compress(
        grp_k, grp_v,
        cmp_pos_k[l].reshape(1, -1), cmp_pos_v[l].reshape(1, -1),
        cmp_k_w1[l], cmp_k_w2[l], cmp_v_w1[l], cmp_v_w2[l], nsa_k_norm[l],
        cos_c, sb_c - sa_c, _rot_half_matrix())
    nsel = seq // SEL_BLOCK
    y_nsa = _nsa_attn(q, kc, vc, ks, vs, kw, vw, gates, _overlap_matrix_t(ngrp, nsel),
                      batch=batch, seq=seq)

    x = _merge(x, o_rwkv, bonus, gate, y_nsa, p_gate, row(rwkv_gn_w[l]), row(rwkv_gn_b[l]), seg_w,
               w_branch_rwkv[l].astype(BF16), w_branch_nsa[l].astype(BF16), w_out[l].astype(BF16))
    return _ffn(x, row(ffn2_norm[l]), ffn2_w_gate[l].astype(BF16), ffn2_w_up[l].astype(BF16),
                ffn2_w_down[l].astype(BF16))


def kernel(x, ffn1_norm, ffn1_w_gate, ffn1_w_up, ffn1_w_down, mix_norm, w_in, rwkv_mix, rwkv_w0, rwkv_w_up, rwkv_a0, rwkv_a_up, rwkv_g_up, rwkv_k_k, rwkv_k_a, rwkv_r_k, rwkv_gn_w, rwkv_gn_b, nsa_q_norm, nsa_k_norm, cmp_pos_k, cmp_pos_v, cmp_k_w1, cmp_k_w2, cmp_v_w1, cmp_v_w2, w_branch_rwkv, w_branch_nsa, w_out, ffn2_norm, ffn2_w_gate, ffn2_w_up, ffn2_w_down):
    batch, seq, d = x.shape
    params = (ffn1_norm, ffn1_w_gate, ffn1_w_up, ffn1_w_down, mix_norm, w_in, rwkv_mix, rwkv_w0,
              rwkv_w_up, rwkv_a0, rwkv_a_up, rwkv_g_up, rwkv_k_k, rwkv_k_a, rwkv_r_k, rwkv_gn_w,
              rwkv_gn_b, nsa_q_norm, nsa_k_norm, cmp_pos_k, cmp_pos_v, cmp_k_w1, cmp_k_w2, cmp_v_w1,
              cmp_v_w2, w_branch_rwkv, w_branch_nsa, w_out, ffn2_norm, ffn2_w_gate, ffn2_w_up,
              ffn2_w_down)
    y = x.reshape(batch * seq, d)
    for l in range(ffn1_norm.shape[0]):
        y = _layer(y, l, *params, batch=batch, seq=seq)
    return y.reshape(batch, seq, d)
```

```python
import functools

import numpy as np
import jax
import jax.numpy as jnp
from jax import lax
from jax.experimental import pallas as pl
from jax.experimental.pallas import tpu as pltpu

F32 = jnp.float32
BF16 = jnp.bfloat16

RWKV_HEAD_DIM = 64
DECAY_LORA = 64
ICLR_LORA = 64
GATE_LORA = 160
GN_EPS = 64e-5
NSA_HEADS = 16
NSA_KV_HEADS = 4
NSA_GROUP = NSA_HEADS // NSA_KV_HEADS
NSA_HEAD_DIM = 64
ROPE_DIM = NSA_HEAD_DIM // 4
ROPE_HALF = ROPE_DIM // 2
ROPE_THETA = 500000.0
CMP_BLOCK = 32
CMP_STRIDE = 16
SEL_BLOCK = 64
SEL_TOP = 16
WINDOW = 512
NORM_EPS = 1e-6

LANE = 128
CHUNK = 64
VMEM_LIMIT = 56 * 1024 * 1024
MASKED = -1e30
LOG2_E = 1.4426950408889634
GATE_ROWS = 16
ATTN_KEY_BLOCK = 256
ATTN_KV_PER_STEP = 4


def _cparams(sem):
    return pltpu.CompilerParams(dimension_semantics=sem, vmem_limit_bytes=VMEM_LIMIT)


def _const_spec(shape):
    nd = len(shape)
    return pl.BlockSpec(shape, lambda *_: (0,) * nd, pipeline_mode=pl.Buffered(1))


def _mm(a, b):
    return lax.dot_general(a, b, (((1,), (0,)), ((), ())), preferred_element_type=F32)


def _mm_nt(a, b):
    return lax.dot_general(a, b, (((1,), (1,)), ((), ())), preferred_element_type=F32)


def _mm_tn(a, b):
    return lax.dot_general(a, b, (((0,), (0,)), ((), ())), preferred_element_type=F32)


def _split2(x):
    hi = x.astype(BF16)
    lo = (x - hi.astype(F32)).astype(BF16)
    return hi, lo


def _split3(x):
    h1 = x.astype(BF16)
    r1 = x - h1.astype(F32)
    h2 = r1.astype(BF16)
    h3 = (r1 - h2.astype(F32)).astype(BF16)
    return h1, h2, h3


def _dot3(a, b, mm=_mm):
    a1, a2 = _split2(a)
    b1, b2 = _split2(b)
    return mm(a1, b1) + (mm(a1, b2) + mm(a2, b1))


def _dot_exact_rhs(a, b_bf16):
    a1, a2 = _split2(a)
    return _mm(a1, b_bf16) + _mm(a2, b_bf16)


def _head_mean(x, seg_bf16, width):
    return _mm(x.astype(BF16), seg_bf16) * (1.0 / width)


def _dot_exact_lhs(a_bf16, b):
    b1, b2, b3 = _split3(b)
    return _mm(a_bf16, b1) + (_mm(a_bf16, b2) + _mm(a_bf16, b3))


def _sigmoid(x):
    return 1.0 / (1.0 + jnp.exp(-x))


def _iota(shape, dim):
    return lax.broadcasted_iota(jnp.int32, shape, dim)


def _ffn_kernel(x_ref, g_ref, wg_ref, wu_ref, wd_ref, o_ref, h_ref):
    j = pl.program_id(1)

    @pl.when(j == 0)
    def _():
        x = x_ref[...]
        ms = jnp.mean(x * x, axis=-1, keepdims=True)
        h_ref[...] = (x * lax.rsqrt(ms + NORM_EPS) * g_ref[...]).astype(BF16)
        o_ref[...] = jnp.zeros_like(o_ref)

    h = h_ref[...]
    gate = _mm(h, wg_ref[...])
    up = _mm(h, wu_ref[...])
    act = (gate * _sigmoid(gate) * up).astype(BF16)
    o_ref[...] += _mm(act, wd_ref[...])

    @pl.when(j == pl.num_programs(1) - 1)
    def _():
        o_ref[...] = x_ref[...] + 0.5 * o_ref[...]


def _ffn(x, g, wg, wu, wd, *, tm=1024, tf=512):
    t, d = x.shape
    f = wg.shape[1]
    return pl.pallas_call(
        _ffn_kernel,
        out_shape=jax.ShapeDtypeStruct((t, d), F32),
        grid=(t // tm, f // tf),
        in_specs=[
            pl.BlockSpec((tm, d), lambda i, j: (i, 0)),
            pl.BlockSpec((1, d), lambda i, j: (0, 0)),
            pl.BlockSpec((d, tf), lambda i, j: (0, j)),
            pl.BlockSpec((d, tf), lambda i, j: (0, j)),
            pl.BlockSpec((tf, d), lambda i, j: (j, 0)),
        ],
        out_specs=pl.BlockSpec((tm, d), lambda i, j: (i, 0)),
        scratch_shapes=[pltpu.VMEM((tm, d), BF16)],
        compiler_params=_cparams(("parallel", "arbitrary")),
        name="ffn",
    )(x, g, wg, wu, wd)


def _norm_proj_kernel(x_ref, g_ref, w_ref, o_ref):
    x = x_ref[...]
    ms = jnp.mean(x * x, axis=-1, keepdims=True)
    h = (x * lax.rsqrt(ms + NORM_EPS) * g_ref[...]).astype(BF16)
    o_ref[...] = _mm(h, w_ref[...]).astype(o_ref.dtype)


def _norm_proj(x, g, w, *, tm=256, name="norm_proj"):
    t, d = x.shape
    n = w.shape[1]
    return pl.pallas_call(
        _norm_proj_kernel,
        out_shape=jax.ShapeDtypeStruct((t, n), F32),
        grid=(t // tm,),
        in_specs=[
            pl.BlockSpec((tm, d), lambda i: (i, 0)),
            _const_spec((1, d)),
            _const_spec((d, n)),
        ],
        out_specs=pl.BlockSpec((tm, n), lambda i: (i, 0)),
        compiler_params=_cparams(("parallel",)),
        name=name,
    )(x, g, w)


def _rwkv_prep_kernel(x_ref, gm_ref, win_ref, mix_ref, wwa_ref, gup_ref, w0_ref, a0_ref, kk_ref, ka_ref,
                      rk_ref, seg_ref, tri_ref,
                      rt_ref, at_ref, kt_ref, bt_ref, kh_ref, bh_ref, v_ref, dc_ref, bonus_ref,
                      gate_ref, last_ref, *, seq_tiles):
    i = pl.program_id(0)
    ts = x_ref.shape[0]
    w = rt_ref.shape[1]
    x = x_ref[...]
    ms = jnp.mean(x * x, axis=-1, keepdims=True)
    p = _mm((x * lax.rsqrt(ms + NORM_EPS) * gm_ref[...]).astype(BF16), win_ref[...])

    @pl.when(i == 0)
    def _():
        last_ref[...] = jnp.zeros_like(last_ref)

    prev = last_ref[7:8, :]
    prev = jnp.where(i % seq_tiles == 0, jnp.zeros_like(prev), prev)
    last_ref[...] = p[ts - 8:ts, :]
    shifted = pltpu.roll(p, 1, axis=0)
    shifted = jnp.where(_iota(p.shape, 0) == 0, prev, shifted)
    xs = p + mix_ref[...] * (shifted - p)

    r = xs[:, 0:w]
    k = xs[:, w:2 * w]
    v = xs[:, 2 * w:3 * w]
    lo = 3 * w
    pwa = xs[:, lo:lo + LANE]
    pg = xs[:, lo + LANE:lo + 3 * LANE]
    lane = _iota(pwa.shape, 1)
    z = jnp.where(lane < DECAY_LORA, jnp.tanh(pwa), pwa)
    wa = _dot3(z, wwa_ref[...])
    wl = w0_ref[...] + wa[:, :w]
    neg = -wl
    softplus = jnp.maximum(neg, 0.0) + jnp.log(1.0 + jnp.exp(-jnp.abs(neg)))
    lw = -jnp.exp(-softplus - 0.5)
    a = _sigmoid(a0_ref[...] + wa[:, w:])
    gate_ref[...] = _dot3(_sigmoid(pg), gup_ref[...])

    seg = seg_ref[...]
    kk = k * kk_ref[...]
    ss = _dot_exact_rhs(kk * kk, seg)
    kk = kk * lax.rsqrt(jnp.maximum(ss, 1e-24))
    k2 = k * (1.0 + (a - 1.0) * ka_ref[...])
    bonus_ref[...] = _dot_exact_rhs(r * k2 * rk_ref[...], seg) * v

    gc = _dot_exact_lhs(tri_ref[...], lw)
    nc = ts // CHUNK
    ends = [gc[(q + 1) * CHUNK - 1:(q + 1) * CHUNK, :] for q in range(nc)]
    gend = jnp.concatenate([jnp.broadcast_to(e, (CHUNK, w)) for e in ends], axis=0)
    to_end = jnp.exp(gend - gc)
    e_in = jnp.exp(gc)
    e_out = jnp.exp(-gc)
    b = kk * a
    rt_ref[...] = r * e_in
    at_ref[...] = -kk * jnp.exp(gc - lw)
    kt_ref[...] = k2 * e_out
    bt_ref[...] = b * e_out
    kh_ref[...] = k2 * to_end
    bh_ref[...] = b * to_end
    v_ref[...] = v
    dc_ref[0] = jnp.concatenate([jnp.exp(e) for e in ends] + [jnp.zeros((8 - nc, w), F32)], axis=0)


def _rwkv_prep(x, g_mix, w_in, mix, wwa, gup, w0, a0, k_k, k_a, r_k, seg, tri, *, seq, ts=256):
    t, d = x.shape
    pc = w_in.shape[1]
    w = w0.shape[1]
    nt = t // ts
    row = lambda i: (i, 0)
    tok = pl.BlockSpec((ts, w), row)
    tok_shape = jax.ShapeDtypeStruct((t, w), F32)
    return pl.pallas_call(
        functools.partial(_rwkv_prep_kernel, seq_tiles=seq // ts),
        out_shape=[tok_shape] * 7 + [jax.ShapeDtypeStruct((nt, 8, w), F32), tok_shape, tok_shape],
        grid=(nt,),
        in_specs=[
            pl.BlockSpec((ts, d), row),
            _const_spec((1, d)),
            _const_spec(w_in.shape),
            _const_spec((1, pc)),
            _const_spec(wwa.shape),
            _const_spec(gup.shape),
            _const_spec((1, w)), _const_spec((1, w)), _const_spec((1, w)), _const_spec((1, w)),
            _const_spec((1, w)),
            _const_spec(seg.shape),
            _const_spec(tri.shape),
        ],
        out_specs=[tok] * 7 + [pl.BlockSpec((1, 8, w), lambda i: (i, 0, 0)), tok, tok],
        scratch_shapes=[pltpu.VMEM((8, pc), F32)],
        compiler_params=_cparams(("arbitrary",)),
        name="rwkv_prep",
    )(x, g_mix, w_in, mix, wwa, gup, w0, a0, k_k, k_a, r_k, seg, tri)


PASSES_SCAN = 1
SCAN_CHUNKS_PER_STEP = 4
INTRA_CHUNKS_PER_STEP = 4
QUAD = 4


def _operand(x, passes):
    return _split2(x) if passes == 3 else (x.astype(BF16),)


def _prod(a, b, mm=_mm):
    if len(a) == 2 and len(b) == 2:
        return mm(a[0], b[0]) + (mm(a[0], b[1]) + mm(a[1], b[0]))
    return mm(a[0], b[0])


def _block_diag(y, n):
    c = y.shape[0]
    tiled = jnp.concatenate([y] * (y.shape[1] // n), axis=0)
    keep = (_iota(tiled.shape, 0) // c) == (_iota(tiled.shape, 1) // n)
    return jnp.where(keep, tiled, jnp.zeros_like(tiled))


def _quad_mm(x, y, n, mm=_mm):
    return mm(x.astype(BF16), _block_diag(y.astype(BF16), n))


def _unit_lower_inverse(a_list, row, col):
    n = CHUNK
    eye = (row == col).astype(F32)
    same8 = (row // 8) == (col // 8)
    a8 = [jnp.where(same8, a, 0.0) for a in a_list]
    d8 = [_block_diag(x.astype(BF16), n) for x in a8]
    a8_2 = [_mm(x.astype(BF16), d) for x, d in zip(a8, d8)]
    d8_2 = [_block_diag(x.astype(BF16), n) for x in a8_2]
    a8_4 = [_mm(x.astype(BF16), d) for x, d in zip(a8_2, d8_2)]
    p = [eye + x + x2 + _mm(x.astype(BF16), d2) for x, x2, d2 in zip(a8, a8_2, d8_2)]
    t = [pp + _quad_mm(pp, x4, n) for pp, x4 in zip(p, a8_4)]
    m = 16
    while m <= CHUNK:
        sel = ((row // m) == (col // m)) & ((row // (m // 2)) != (col // (m // 2)))
        mid = [_quad_mm(x, jnp.where(sel, a, 0.0), n) for x, a in zip(t, a_list)]
        t = [x + _quad_mm(md, x, n) for x, md in zip(t, mid)]
        m *= 2
    return t


def _rwkv_intra_kernel(rt_ref, at_ref, kt_ref, bt_ref, kh_ref, bh_ref, v_ref, dc_ref,
                       rr_ref, o0_ref, gh_ref, *, heads, chunks_per_tile):
    n = RWKV_HEAD_DIM
    qw = QUAD * n
    cps = gh_ref.shape[0]
    i = pl.program_id(0)
    row = _iota((CHUNK, qw), 0)
    col = _iota((CHUNK, qw), 1) % n
    strict = col < row
    incl = col <= row
    dc_all = dc_ref[0]
    dc_rows = []
    for c in range(cps):
        r = dc_all[c:c + 1, :]
        for q in range(1, chunks_per_tile // cps):
            r = jnp.where(i % (chunks_per_tile // cps) == q, dc_all[q * cps + c:q * cps + c + 1, :], r)
        dc_rows.append(r)

    units = [(slice(c * CHUNK, (c + 1) * CHUNK), slice(j * qw, (j + 1) * qw))
             for c in range(cps) for j in range(heads // QUAD)]
    at = [at_ref[r, s] for r, s in units]
    rt = [rt_ref[r, s] for r, s in units]
    v = [v_ref[r, s] for r, s in units]
    bd_b = [_block_diag(bt_ref[r, s].astype(BF16), n) for r, s in units]
    bd_k = [_block_diag(kt_ref[r, s].astype(BF16), n) for r, s in units]
    ar = [jnp.concatenate([a, r], axis=0).astype(BF16) for a, r in zip(at, rt)]
    mb = [_mm_nt(x, d) for x, d in zip(ar, bd_b)]
    mk = [_mm_nt(x, d) for x, d in zip(ar, bd_k)]
    a_ab = [jnp.where(strict, m[:CHUNK], 0.0) for m in mb]
    a_rb = [jnp.where(incl, m[CHUNK:], 0.0) for m in mb]
    akrk = [jnp.concatenate([jnp.where(strict, m[:CHUNK], 0.0), jnp.where(incl, m[CHUNK:], 0.0)], axis=0)
            for m in mk]
    avv = [_quad_mm(x, y, n) for x, y in zip(akrk, v)]
    tinv = _unit_lower_inverse(a_ab, row, col)
    a_new = [_quad_mm(t, a, n) for t, a in zip(tinv, at)]
    u0 = [_quad_mm(t, w[:CHUNK], n) for t, w in zip(tinv, avv)]
    for u, (r, s) in enumerate(units):
        rr_ref[r, s] = rt[u] + _quad_mm(a_rb[u], a_new[u], n)
        o0_ref[r, s] = _quad_mm(a_rb[u], u0[u], n) + avv[u][CHUNK:]
    eye = _iota((n, n), 0) == _iota((n, n), 1)
    nq = heads // QUAD
    gz, kv = [], []
    for c in range(cps):
        r = slice(c * CHUNK, (c + 1) * CHUNK)
        for h in range(heads):
            u, s = c * nq + h // QUAD, slice((h % QUAD) * n, (h % QUAD + 1) * n)
            hs = slice(h * n, (h + 1) * n)
            z = jnp.concatenate([a_new[u][:, s], u0[u][:, s]], axis=1).astype(BF16)
            gz.append(_mm_tn(bh_ref[r, hs].astype(BF16), z))
            kv.append(_mm_tn(kh_ref[r, hs].astype(BF16), v[u][:, s].astype(BF16)))
    for c in range(cps):
        for h in range(heads):
            hs = slice(h * n, (h + 1) * n)
            dmat = jnp.where(eye, jnp.broadcast_to(dc_rows[c][:, hs], (n, n)), 0.0)
            gh_ref[c, h] = gz[c * heads + h] + jnp.concatenate([dmat, kv[c * heads + h]], axis=1)


def _rwkv_scan_kernel(rr_ref, o0_ref, gh_ref, o_ref, state_ref, *, heads):
    n = RWKV_HEAD_DIM

    @pl.when(pl.program_id(1) == 0)
    def _():
        state_ref[...] = jnp.zeros_like(state_ref)

    sls = [slice(h * n, (h + 1) * n) for h in range(heads)]
    state = [state_ref[h] for h in range(heads)]
    for c in range(gh_ref.shape[0]):
        rows = slice(c * CHUNK, (c + 1) * CHUNK)
        h0 = [_operand(x, PASSES_SCAN) for x in state]
        outs = [_prod(_operand(rr_ref[rows, s], PASSES_SCAN), x) + o0_ref[rows, s] for s, x in zip(sls, h0)]
        state = [_prod(_operand(gh_ref[c, h, :, 0:n], PASSES_SCAN), h0[h]) + gh_ref[c, h, :, n:2 * n]
                 for h in range(heads)]
        for h, s in enumerate(sls):
            o_ref[rows, s] = outs[h]
    for h in range(heads):
        state_ref[h] = state[h]


def _rwkv_chunk(rt, at, kt, bt, kh, bh, v, dc, *, batch, seq, prep_ts):
    t, w = rt.shape
    n = RWKV_HEAD_DIM
    heads = w // n
    nchunk = seq // CHUNK
    cpt = prep_ts // CHUNK
    cps = INTRA_CHUNKS_PER_STEP
    tok = pl.BlockSpec((cps * CHUNK, w), lambda i: (i, 0))
    tok_shape = jax.ShapeDtypeStruct((t, w), F32)
    rr, o0, gh = pl.pallas_call(
        functools.partial(_rwkv_intra_kernel, heads=heads, chunks_per_tile=cpt),
        out_shape=[tok_shape, tok_shape, jax.ShapeDtypeStruct((t // CHUNK, heads, n, 2 * n), F32)],
        grid=(t // (cps * CHUNK),),
        in_specs=[tok] * 7 + [pl.BlockSpec((1, 8, w), lambda i: (i * cps // cpt, 0, 0))],
        out_specs=[tok, tok, pl.BlockSpec((cps, heads, n, 2 * n), lambda i: (i, 0, 0, 0))],
        compiler_params=_cparams(("parallel",)),
        name="rwkv_intra",
    )(rt, at, kt, bt, kh, bh, v, dc)
    steps = nchunk // SCAN_CHUNKS_PER_STEP
    tok2 = pl.BlockSpec((SCAN_CHUNKS_PER_STEP * CHUNK, w), lambda b, c: (b * steps + c, 0))
    return pl.pallas_call(
        functools.partial(_rwkv_scan_kernel, heads=heads),
        out_shape=tok_shape,
        grid=(batch, steps),
        in_specs=[tok2, tok2, pl.BlockSpec((SCAN_CHUNKS_PER_STEP, heads, n, 2 * n),
                                           lambda b, c: (b * steps + c, 0, 0, 0))],
        out_specs=tok2,
        scratch_shapes=[pltpu.VMEM((heads, n, n), F32)],
        compiler_params=_cparams(("parallel", "arbitrary")),
        name="rwkv_scan",
    )(rr, o0, gh)


def _rope_lanes(x, cos_t, sin_a, sin_b):
    width = x.shape[1]
    up = pltpu.roll(x, width - ROPE_HALF, axis=1)
    dn = pltpu.roll(x, ROPE_HALF, axis=1)
    return x * cos_t + up * sin_a + dn * sin_b


def _tile_lanes(tab, width):
    return jnp.concatenate([tab] * (width // tab.shape[1]), axis=1)


def _nsa_prep_kernel(x_ref, gm_ref, win_ref, cos_ref, sa_ref, sb_ref, qn_ref, kn_ref, seg_ref,
                     q_ref, ks_ref, vs_ref, kw_ref, vw_ref, g_ref, gk_ref, gv_ref, *, seq_tiles):
    dh = NSA_HEAD_DIM
    x = x_ref[...]
    ms = jnp.mean(x * x, axis=-1, keepdims=True)
    p = _mm((x * lax.rsqrt(ms + NORM_EPS) * gm_ref[...]).astype(BF16), win_ref[...])
    qw = q_ref.shape[1]
    kvw = NSA_KV_HEADS * dh
    seg = seg_ref[...]
    cos_t, sin_a, sin_b = cos_ref[...], sa_ref[...], sb_ref[...]

    def norm_rope(x, gain):
        wd = x.shape[1]
        ms = _head_mean(x * x, seg[:wd, :wd], dh)
        y = x * lax.rsqrt(ms + NORM_EPS) * gain
        return _rope_lanes(y, _tile_lanes(cos_t, wd), _tile_lanes(sin_a, wd), _tile_lanes(sin_b, wd))

    q = norm_rope(p[:, 0:qw], qn_ref[...])
    q_ref[...] = q * (dh ** -0.5)
    base = qw + 2 * kvw
    ks = norm_rope(p[:, base:base + kvw], kn_ref[1:2, :])
    vs = p[:, base + kvw:base + 2 * kvw].astype(BF16)
    kw = norm_rope(p[:, base + 2 * kvw:base + 3 * kvw], kn_ref[2:3, :]).astype(BF16)
    vw = p[:, base + 3 * kvw:base + 4 * kvw].astype(BF16)
    ts = x_ref.shape[0]
    tpos = (pl.program_id(0) % seq_tiles) * ts + _iota((ts, dh), 0)
    onehot = jnp.where(tpos // SEL_BLOCK == _iota((ts, dh), 1), 1.0, 0.0)
    for h in range(NSA_KV_HEADS):
        sl = slice(h * dh, (h + 1) * dh)
        ks_ref[0, h] = jnp.concatenate([ks[:, sl], onehot], axis=1).astype(BF16)
        vs_ref[0, h] = vs[:, sl]
        kw_ref[0, h] = kw[:, sl]
        vw_ref[0, h] = vw[:, sl]
    sig = _sigmoid(p[:, base + 4 * kvw:base + 4 * kvw + LANE])
    sig_t = sig.T
    per_head = 3 * NSA_GROUP
    for h in range(NSA_KV_HEADS):
        g_ref[h] = sig_t[per_head * h:per_head * h + GATE_ROWS, :]
    kc3 = p[:, qw:qw + kvw].reshape(ts // CMP_STRIDE, CMP_STRIDE, kvw)
    vc3 = p[:, qw + kvw:qw + 2 * kvw].reshape(ts // CMP_STRIDE, CMP_STRIDE, kvw)
    for i in range(CMP_STRIDE):
        kci, vci = kc3[:, i, :], vc3[:, i, :]
        for h in range(NSA_KV_HEADS):
            gk_ref[0, h, :, i * dh:(i + 1) * dh] = kci[:, h * dh:(h + 1) * dh]
            gv_ref[0, h, :, i * dh:(i + 1) * dh] = vci[:, h * dh:(h + 1) * dh]


def _nsa_prep(x, g_mix, w_in, cos_t, sin_a, sin_b, qn, kn, seg, *, batch, seq, ts=256):
    t, d = x.shape
    qw = NSA_HEADS * NSA_HEAD_DIM
    st = seq // ts
    tab = pl.BlockSpec((ts, LANE), lambda i: (i % st, 0))
    hm = pl.BlockSpec((1, NSA_KV_HEADS, ts, NSA_HEAD_DIM), lambda i: (i // st, 0, i % st, 0))
    hm_shape = jax.ShapeDtypeStruct((batch, NSA_KV_HEADS, seq, NSA_HEAD_DIM), BF16)
    assert seq // SEL_BLOCK <= NSA_HEAD_DIM
    aug = pl.BlockSpec((1, NSA_KV_HEADS, ts, 2 * NSA_HEAD_DIM), lambda i: (i // st, 0, i % st, 0))
    aug_shape = jax.ShapeDtypeStruct((batch, NSA_KV_HEADS, seq, 2 * NSA_HEAD_DIM), BF16)
    grp_w = CMP_STRIDE * NSA_HEAD_DIM
    grp = pl.BlockSpec((1, NSA_KV_HEADS, ts // CMP_STRIDE, grp_w), lambda i: (i // st, 0, i % st, 0))
    grp_shape = jax.ShapeDtypeStruct((batch, NSA_KV_HEADS, seq // CMP_STRIDE, grp_w), F32)
    return pl.pallas_call(
        functools.partial(_nsa_prep_kernel, seq_tiles=st),
        out_shape=[jax.ShapeDtypeStruct((t, qw), F32), aug_shape] + [hm_shape] * 3
        + [jax.ShapeDtypeStruct((NSA_KV_HEADS, GATE_ROWS, t), F32)] + [grp_shape] * 2,
        grid=(t // ts,),
        in_specs=[pl.BlockSpec((ts, d), lambda i: (i, 0)), _const_spec((1, d)), _const_spec(w_in.shape),
                  tab, tab, tab,
                  _const_spec(qn.shape), _const_spec(kn.shape), _const_spec(seg.shape)],
        out_specs=[pl.BlockSpec((ts, qw), lambda i: (i, 0)), aug] + [hm] * 3
        + [pl.BlockSpec((NSA_KV_HEADS, GATE_ROWS, ts), lambda i: (0, 0, i))] + [grp] * 2,
        compiler_params=_cparams(("parallel",)),
        name="nsa_prep",
    )(x, g_mix, w_in, cos_t, sin_a, sin_b, qn, kn, seg)


def _gelu_tanh(x):
    return 0.5 * x * (1.0 + jnp.tanh(np.sqrt(2.0 / np.pi).astype(np.float32) * (x + 0.044715 * (x * x * x))))


def _compress_kernel(gk_ref, gv_ref, pk_ref, pv_ref, k1_ref, k2_ref, v1_ref, v2_ref, kn_ref,
                     cos_ref, sin_ref, rot_ref, kc_ref, vc_ref):
    half = k1_ref.shape[0] // 2

    def mlp(g, pos, w1_ref, w2_ref):
        ya = _dot3(g, w1_ref[0:half, :])
        yb = _dot3(g, w1_ref[half:, :])
        bias = _dot3(jnp.broadcast_to(pos, (8, pos.shape[1])), w1_ref[...])[0:1, :]
        n = g.shape[0]
        hid = ya + pltpu.roll(yb, n - 1, axis=0) + bias
        return _dot3(_gelu_tanh(hid), w2_ref[...])

    kc = mlp(gk_ref[0, 0], pk_ref[...], k1_ref, k2_ref)
    ms = jnp.mean(kc * kc, axis=-1, keepdims=True)
    kc = kc * lax.rsqrt(ms + NORM_EPS) * kn_ref[0:1, :]
    kc_ref[0, 0] = kc * cos_ref[...] + _dot_exact_rhs(kc, rot_ref[...]) * sin_ref[...]
    vc_ref[0, 0] = mlp(gv_ref[0, 0], pv_ref[...], v1_ref, v2_ref)


def _nsa_compress(gk, gv, pk, pv, k1, k2, v1, v2, kn, cos_c, sin_c, rot):
    b, hk, ng, gw = gk.shape
    dh = NSA_HEAD_DIM
    grp = pl.BlockSpec((1, 1, ng, gw), lambda i, j: (i, j, 0, 0))
    out = pl.BlockSpec((1, 1, ng, dh), lambda i, j: (i, j, 0, 0))
    shape = jax.ShapeDtypeStruct((b, hk, ng, dh), F32)
    consts = [pk, pv, k1, k2, v1, v2, kn, cos_c, sin_c, rot]
    return pl.pallas_call(
        _compress_kernel,
        out_shape=[shape, shape],
        grid=(b, hk),
        in_specs=[grp, grp] + [_const_spec(c.shape) for c in consts],
        out_specs=[out, out],
        compiler_params=_cparams(("parallel", "parallel")),
        name="nsa_compress",
    )(gk, gv, *consts)


def _nsa_attn_kernel(q_ref, kc_ref, vc_ref, ks_ref, vs_ref, kw_ref, vw_ref, g_ref, ovt_ref, o_ref):
    dh = NSA_HEAD_DIM
    grp = NSA_GROUP
    nh = kc_ref.shape[1]
    qi = pl.program_id(2)
    tq = q_ref.shape[0]
    hc = grp * tq
    cols = nh * hc
    t0 = qi * tq
    ncmp = kc_ref.shape[2]
    nsel = ovt_ref.shape[0]
    kb = ATTN_KEY_BLOCK
    span = WINDOW + tq
    heads = range(nh)

    def lanes(xs):
        return jnp.concatenate(xs, axis=1)

    def every_head(x):
        return lanes([x] * (nh * grp))

    q = q_ref[...]
    q4 = [jnp.concatenate([q[:, (h * grp + g) * dh:(h * grp + g + 1) * dh] for g in range(grp)], axis=0)
          for h in heads]
    q4b = [(x * LOG2_E).astype(BF16) for x in q4]

    st = lanes([_dot3(kc_ref[0, h], q4[h], _mm_nt) for h in heads])
    tl = t0 + _iota((ncmp, cols), 1) % tq
    cmask = _iota((ncmp, cols), 0) * CMP_STRIDE + (CMP_BLOCK - 1) <= tl
    sm = jnp.where(cmask, st, MASKED)
    e = jnp.where(cmask, jnp.exp(sm - jnp.max(sm, axis=0, keepdims=True)), 0.0)
    pt = e / jnp.maximum(jnp.sum(e, axis=0, keepdims=True), 1e-30)
    ptb = pt.astype(BF16)
    o_cmp = lanes([_mm_tn(vc_ref[0, h].astype(BF16), ptb[:, h * hc:(h + 1) * hc]) for h in heads])
    psum = []
    for h in heads:
        acc = pt[:, h * hc:h * hc + tq]
        for g in range(1, grp):
            acc = acc + pt[:, h * hc + g * tq:h * hc + (g + 1) * tq]
        psum.append(acc)
    psum = lanes(psum)

    imp = _dot_exact_lhs(ovt_ref[...], psum)
    blk = _iota(imp.shape, 0)
    cur = (t0 + _iota(imp.shape, 1) % tq) // SEL_BLOCK
    forced = (blk == 0) | (blk == cur) | (blk == cur - 1)
    imp = jnp.where(forced, jnp.inf, jnp.where(blk > cur, -jnp.inf, imp))
    rank = jnp.zeros(imp.shape, jnp.int32)
    for m in range(nsel):
        im = imp[m:m + 1, :]
        ahead = (im > imp) | ((im == imp) & (m < blk))
        rank = rank + ahead.astype(jnp.int32)
    sel_bias = jnp.where(rank < min(SEL_TOP, nsel), 0.0, MASKED).astype(BF16)

    eye = jnp.where(_iota((nsel, dh), 0) == _iota((nsel, dh), 1), 1.0, 0.0).astype(BF16)
    bias_q = _mm_tn(sel_bias, eye)
    q_aug = [jnp.concatenate([q4[h] * LOG2_E, jnp.concatenate([bias_q[h * tq:(h + 1) * tq]] * grp, axis=0)],
                             axis=1).astype(BF16) for h in heads]

    def attend(k_ref, v_ref, qs, k0, state, bias=None):
        m_run, l_run, acc = state
        rows = pl.ds(pl.multiple_of(k0, kb), kb)
        s = lanes([_mm_nt(k_ref[0, h, rows, :], qs[h]) for h in heads])
        if bias is not None:
            s = s + bias
        m_new = jnp.maximum(m_run, jnp.max(s, axis=0, keepdims=True))
        alpha = jnp.exp2(m_run - m_new)
        pb = jnp.exp2(s - m_new)
        l_new = alpha * l_run + jnp.sum(pb, axis=0, keepdims=True)
        pb = pb.astype(BF16)
        pv = lanes([_mm_tn(v_ref[0, h, rows, :], pb[:, h * hc:(h + 1) * hc]) for h in heads])
        return m_new, l_new, alpha * acc + pv

    init = (jnp.full((1, cols), MASKED, F32), jnp.zeros((1, cols), F32), jnp.zeros((dh, cols), F32))
    state = lax.fori_loop(0, t0 // kb, lambda j, st: attend(ks_ref, vs_ref, q_aug, j * kb, st), init)
    for d in range(tq // kb):
        causal = jnp.where(d * kb + _iota((kb, tq), 0) <= _iota((kb, tq), 1), 0.0, MASKED)
        state = attend(ks_ref, vs_ref, q_aug, t0 + d * kb, state, every_head(causal))
    o_slc = state[2] / state[1]

    w0 = pl.multiple_of(jnp.maximum(t0 - WINDOW, 0), tq)
    wrows = pl.ds(w0, span)
    kpos = w0 + _iota((span, tq), 0)
    tw = t0 + _iota((span, tq), 1)
    wbias = jnp.where((kpos <= tw) & (kpos > tw - WINDOW), 0.0, MASKED)
    s = lanes([_mm_nt(kw_ref[0, h, wrows, :], q4b[h]) for h in heads]) + every_head(wbias)
    p = jnp.exp2(s - jnp.max(s, axis=0, keepdims=True))
    pb = p.astype(BF16)
    o_win = (lanes([_mm_tn(vw_ref[0, h, wrows, :], pb[:, h * hc:(h + 1) * hc]) for h in heads])
             / jnp.sum(p, axis=0, keepdims=True))

    gates = g_ref[...]
    grow = [lanes([gates[h, 3 * g + br:3 * g + br + 1, :] for h in heads for g in range(grp)])
            for br in range(3)]
    o4 = grow[0] * o_cmp + grow[1] * o_slc + grow[2] * o_win
    o_ref[...] = jnp.concatenate([o4[:, c * tq:(c + 1) * tq] for c in range(nh * grp)], axis=0)


def _nsa_attn(q, kc, vc, ks_aug, vs, kw, vw, gates_t, overlap_t, *, batch, seq, tq=256,
              nh=ATTN_KV_PER_STEP):
    t, qw = q.shape
    dh = NSA_HEAD_DIM
    gw = nh * NSA_GROUP * dh
    st = seq // tq
    ncmp = kc.shape[2]
    cmp_spec = pl.BlockSpec((1, nh, ncmp, dh), lambda b, h, i: (b, h, 0, 0))
    kv_spec = pl.BlockSpec((1, nh, seq, dh), lambda b, h, i: (b, h, 0, 0))
    aug_spec = pl.BlockSpec((1, nh, seq, 2 * dh), lambda b, h, i: (b, h, 0, 0))
    return pl.pallas_call(
        _nsa_attn_kernel,
        out_shape=jax.ShapeDtypeStruct((qw, t), F32),
        grid=(batch, NSA_KV_HEADS // nh, st),
        in_specs=[
            pl.BlockSpec((tq, gw), lambda b, h, i: (b * st + i, h)),
            cmp_spec, cmp_spec, aug_spec, kv_spec, kv_spec, kv_spec,
            pl.BlockSpec((nh, gates_t.shape[1], tq), lambda b, h, i: (h, 0, b * st + i)),
            _const_spec(overlap_t.shape),
        ],
        out_specs=pl.BlockSpec((gw, tq), lambda b, h, i: (h, b * st + i)),
        compiler_params=_cparams(("parallel", "parallel", "arbitrary")),
        name="nsa_attn",
    )(q, kc, vc, ks_aug, vs, kw, vw, gates_t, overlap_t)


def _merge_kernel(x_ref, o_ref, bonus_ref, gate_ref, ybt_ref, pg_ref, gnw_ref, gnb_ref, seg_ref,
                  ua_ref, ub_ref, wo_ref, out_ref):
    d = x_ref.shape[1]
    n = RWKV_HEAD_DIM
    seg = seg_ref[...]
    o = o_ref[...]
    mu = _dot_exact_rhs(o, seg) * (1.0 / n)
    dlt = o - mu
    var = _head_mean(dlt * dlt, seg, n)
    on = dlt * lax.rsqrt(var + GN_EPS) * gnw_ref[...] + gnb_ref[...]
    ya = ((on + bonus_ref[...]) * gate_ref[...]).astype(BF16)
    yb_t = ybt_ref[...].astype(BF16)
    merged = (_sigmoid(pg_ref[:, 0:d]) * _mm(ya, ua_ref[...])
              + _sigmoid(pg_ref[:, d:2 * d]) * _mm_tn(yb_t, ub_ref[...]))
    out_ref[...] = x_ref[...] + _mm(merged.astype(BF16), wo_ref[...])


def _merge(x, o_rwkv, bonus, gate, yb_t, pg, gnw, gnb, seg, ua, ub, wo, *, tm=256):
    t, d = x.shape
    w = o_rwkv.shape[1]
    row = lambda i: (i, 0)
    tokw = pl.BlockSpec((tm, w), row)
    return pl.pallas_call(
        _merge_kernel,
        out_shape=jax.ShapeDtypeStruct((t, d), F32),
        grid=(t // tm,),
        in_specs=[pl.BlockSpec((tm, d), row), tokw, tokw, tokw,
                  pl.BlockSpec((yb_t.shape[0], tm), lambda i: (0, i)),
                  pl.BlockSpec((tm, 2 * d), row),
                  _const_spec((1, w)), _const_spec((1, w)), _const_spec(seg.shape),
                  _const_spec(ua.shape), _const_spec(ub.shape), _const_spec(wo.shape)],
        out_specs=pl.BlockSpec((tm, d), row),
        compiler_params=_cparams(("parallel",)),
        name="merge",
    )(x, o_rwkv, bonus, gate, yb_t, pg, gnw, gnb, seg, ua, ub, wo)


def _block_diag_ones(width, block):
    idx = np.arange(width) // block
    return jnp.asarray(idx[:, None] == idx[None, :], BF16)


def _chunk_lower_ones(ts):
    i = np.arange(ts)
    return jnp.asarray((i[:, None] // CHUNK == i[None, :] // CHUNK) & (i[None, :] <= i[:, None]), BF16)


def _rope_tables(pos):
    inv = ROPE_THETA ** (-jnp.arange(ROPE_HALF, dtype=F32) / ROPE_HALF)
    ang = jnp.asarray(pos).astype(F32)[:, None] * inv[None, :]
    cos, sin = jnp.cos(ang), jnp.sin(ang)
    n = ang.shape[0]
    pad = jnp.zeros((n, NSA_HEAD_DIM - ROPE_DIM), F32)
    zero = jnp.zeros_like(sin)
    cos_h = jnp.concatenate([cos, cos, pad + 1.0], axis=1)
    sa_h = jnp.concatenate([-sin, zero, pad], axis=1)
    sb_h = jnp.concatenate([zero, sin, pad], axis=1)
    return cos_h, sa_h, sb_h


def _rot_half_matrix():
    r = np.zeros((NSA_HEAD_DIM, NSA_HEAD_DIM), np.float32)
    for l in range(ROPE_HALF):
        r[l + ROPE_HALF, l] = -1.0
        r[l, l + ROPE_HALF] = 1.0
    return jnp.asarray(r, BF16)


def _overlap_matrix_t(ncmp_pad, nsel):
    cs = np.arange(ncmp_pad)[None, :] * CMP_STRIDE
    ss = np.arange(nsel)[:, None] * SEL_BLOCK
    ov = np.clip(np.minimum(cs + CMP_BLOCK, ss + SEL_BLOCK) - np.maximum(cs, ss), 0, None) / CMP_BLOCK
    return jnp.asarray(ov, BF16)


def _pad_cols(x, width):
    return jnp.pad(x, ((0, 0), (0, width - x.shape[1])))


def _layer(x, l, ffn1_norm, ffn1_w_gate, ffn1_w_up, ffn1_w_down, mix_norm, w_in,
           rwkv_mix, rwkv_w0, rwkv_w_up, rwkv_a0, rwkv_a_up, rwkv_g_up,
           rwkv_k_k, rwkv_k_a, rwkv_r_k, rwkv_gn_w, rwkv_gn_b,
           nsa_q_norm, nsa_k_norm, cmp_pos_k, cmp_pos_v,
           cmp_k_w1, cmp_k_w2, cmp_v_w1, cmp_v_w2,
           w_branch_rwkv, w_branch_nsa, w_out,
           ffn2_norm, ffn2_w_gate, ffn2_w_up, ffn2_w_down, *, batch, seq):
    t, d = x.shape
    w = rwkv_w0.shape[1]
    dh = NSA_HEAD_DIM
    qw = NSA_HEADS * dh
    kvw = NSA_KV_HEADS * dh
    prep_ts = 256
    row = lambda v: v.reshape(1, -1)

    x = _ffn(x, row(ffn1_norm[l]), ffn1_w_gate[l].astype(BF16), ffn1_w_up[l].astype(BF16),
             ffn1_w_down[l].astype(BF16))

    wi = w_in[l]
    rwkv_cols = 3 * w + DECAY_LORA + ICLR_LORA + GATE_LORA
    rwkv_pad = 3 * w + 3 * LANE
    nsa_cols = qw + 6 * kvw + 3 * NSA_HEADS
    nsa_pad = qw + 6 * kvw + LANE
    g_mix = row(mix_norm[l])
    w_rwkv = _pad_cols(wi[:, :rwkv_cols], rwkv_pad).astype(BF16)
    w_nsa = _pad_cols(wi[:, rwkv_cols:rwkv_cols + nsa_cols], nsa_pad).astype(BF16)
    p_gate = _norm_proj(x, g_mix, wi[:, rwkv_cols + nsa_cols:].astype(BF16), name="proj_gate")

    wwa = jnp.zeros((LANE, 2 * w), F32)
    wwa = wwa.at[:DECAY_LORA, :w].set(rwkv_w_up[l]).at[DECAY_LORA:, w:].set(rwkv_a_up[l])
    gup = jnp.pad(rwkv_g_up[l], ((0, 2 * LANE - GATE_LORA), (0, 0)))
    seg_w = _block_diag_ones(w, RWKV_HEAD_DIM)
    (rt, at, kt, bt, kh, bh, v, dc, bonus, gate) = _rwkv_prep(
        x, g_mix, w_rwkv, _pad_cols(row(rwkv_mix[l]), rwkv_pad), wwa, gup, row(rwkv_w0[l]), row(rwkv_a0[l]),
        row(rwkv_k_k[l]), row(rwkv_k_a[l]), row(rwkv_r_k[l]), seg_w, _chunk_lower_ones(prep_ts),
        seq=seq, ts=prep_ts)
    o_rwkv = _rwkv_chunk(rt, at, kt, bt, kh, bh, v, dc, batch=batch, seq=seq, prep_ts=prep_ts)

    cos_t, sin_a, sin_b = _rope_tables(np.arange(seq))
    two = lambda tab: jnp.concatenate([tab, tab], axis=1)
    qn = jnp.tile(row(nsa_q_norm[l]), (1, NSA_HEADS))
    kn = jnp.tile(nsa_k_norm[l], (1, NSA_KV_HEADS))
    q, ks, vs, kw, vw, gates, grp_k, grp_v = _nsa_prep(
        x, g_mix, w_nsa, two(cos_t), two(sin_a), two(sin_b), qn, kn, _block_diag_ones(qw, dh),
        batch=batch, seq=seq)

    ngrp = seq // CMP_STRIDE
    cend = np.arange(ngrp) * CMP_STRIDE + CMP_BLOCK - 1
    cos_c, sa_c, sb_c = _rope_tables(cend)
    kc, vc = _nsa_compress(
        grp_k, grp_v,
        cmp_pos_k[l].reshape(1, -1), cmp_pos_v[l].reshape(1, -1),
        cmp_k_w1[l], cmp_k_w2[l], cmp_v_w1[l], cmp_v_w2[l], nsa_k_norm[l],
        cos_c, sb_c - sa_c, _rot_half_matrix())
    nsel = seq // SEL_BLOCK
    y_nsa = _nsa_attn(q, kc, vc, ks, vs, kw, vw, gates, _overlap_matrix_t(ngrp, nsel),
                      batch=batch, seq=seq)

    x = _merge(x, o_rwkv, bonus, gate, y_nsa, p_gate, row(rwkv_gn_w[l]), row(rwkv_gn_b[l]), seg_w,
               w_branch_rwkv[l].astype(BF16), w_branch_nsa[l].astype(BF16), w_out[l].astype(BF16))
    return _ffn(x, row(ffn2_norm[l]), ffn2_w_gate[l].astype(BF16), ffn2_w_up[l].astype(BF16),
                ffn2_w_down[l].astype(BF16))


def kernel(x, ffn1_norm, ffn1_w_gate, ffn1_w_up, ffn1_w_down, mix_norm, w_in, rwkv_mix, rwkv_w0, rwkv_w_up, rwkv_a0, rwkv_a_up, rwkv_g_up, rwkv_k_k, rwkv_k_a, rwkv_r_k, rwkv_gn_w, rwkv_gn_b, nsa_q_norm, nsa_k_norm, cmp_pos_k, cmp_pos_v, cmp_k_w1, cmp_k_w2, cmp_v_w1, cmp_v_w2, w_branch_rwkv, w_branch_nsa, w_out, ffn2_norm, ffn2_w_gate, ffn2_w_up, ffn2_w_down):
    batch, seq, d = x.shape
    params = (ffn1_norm, ffn1_w_gate, ffn1_w_up, ffn1_w_down, mix_norm, w_in, rwkv_mix, rwkv_w0,
              rwkv_w_up, rwkv_a0, rwkv_a_up, rwkv_g_up, rwkv_k_k, rwkv_k_a, rwkv_r_k, rwkv_gn_w,
              rwkv_gn_b, nsa_q_norm, nsa_k_norm, cmp_pos_k, cmp_pos_v, cmp_k_w1, cmp_k_w2, cmp_v_w1,
              cmp_v_w2, w_branch_rwkv, w_branch_nsa, w_out, ffn2_norm, ffn2_w_gate, ffn2_w_up,
              ffn2_w_down)
    y = x.reshape(batch * seq, d)
    for l in range(ffn1_norm.shape[0]):
        y = _layer(y, l, *params, batch=batch, seq=seq)
    return y.reshape(batch, seq, d)
```

```python
import functools

import numpy as np
import jax
import jax.numpy as jnp
from jax import lax
from jax.experimental import pallas as pl
from jax.experimental.pallas import tpu as pltpu

F32 = jnp.float32
BF16 = jnp.bfloat16

RWKV_HEAD_DIM = 64
DECAY_LORA = 64
ICLR_LORA = 64
GATE_LORA = 160
GN_EPS = 64e-5
NSA_HEADS = 16
NSA_KV_HEADS = 4
NSA_GROUP = NSA_HEADS // NSA_KV_HEADS
NSA_HEAD_DIM = 64
ROPE_DIM = NSA_HEAD_DIM // 4
ROPE_HALF = ROPE_DIM // 2
ROPE_THETA = 500000.0
CMP_BLOCK = 32
CMP_STRIDE = 16
SEL_BLOCK = 64
SEL_TOP = 16
WINDOW = 512
NORM_EPS = 1e-6

LANE = 128
SEG_WIDTH = 256
CHUNK = 64
VMEM_LIMIT = 56 * 1024 * 1024
MASKED = -1e30
LOG2_E = 1.4426950408889634
GATE_ROWS = 16
ATTN_KEY_BLOCK = 256
ATTN_KV_PER_STEP = 4


def _cparams(sem):
    return pltpu.CompilerParams(dimension_semantics=sem, vmem_limit_bytes=VMEM_LIMIT)


def _const_spec(shape):
    nd = len(shape)
    return pl.BlockSpec(shape, lambda *_: (0,) * nd, pipeline_mode=pl.Buffered(1))


def _mm(a, b):
    return lax.dot_general(a, b, (((1,), (0,)), ((), ())), preferred_element_type=F32)


def _mm_nt(a, b):
    return lax.dot_general(a, b, (((1,), (1,)), ((), ())), preferred_element_type=F32)


def _mm_tn(a, b):
    return lax.dot_general(a, b, (((0,), (0,)), ((), ())), preferred_element_type=F32)


def _split2(x):
    hi = x.astype(BF16)
    lo = (x - hi.astype(F32)).astype(BF16)
    return hi, lo


def _split3(x):
    h1 = x.astype(BF16)
    r1 = x - h1.astype(F32)
    h2 = r1.astype(BF16)
    h3 = (r1 - h2.astype(F32)).astype(BF16)
    return h1, h2, h3


def _dot3(a, b, mm=_mm):
    a1, a2 = _split2(a)
    b1, b2 = _split2(b)
    return mm(a1, b1) + (mm(a1, b2) + mm(a2, b1))


def _mm_groups(a, b):
    k = b.shape[0]
    if a.shape[1] == k:
        return _mm(a, b)
    return jnp.concatenate([_mm(a[:, i * k:(i + 1) * k], b) for i in range(a.shape[1] // k)], axis=1)


def _dot_exact_rhs(a, b_bf16):
    a1, a2 = _split2(a)
    return _mm_groups(a1, b_bf16) + _mm_groups(a2, b_bf16)


def _head_mean(x, seg_bf16, width):
    return _mm_groups(x.astype(BF16), seg_bf16) * (1.0 / width)


def _dot_exact_lhs(a_bf16, b):
    b1, b2, b3 = _split3(b)
    return _mm(a_bf16, b1) + (_mm(a_bf16, b2) + _mm(a_bf16, b3))


def _sigmoid(x):
    return 1.0 / (1.0 + jnp.exp(-x))


def _iota(shape, dim):
    return lax.broadcasted_iota(jnp.int32, shape, dim)


def _ffn_kernel(x_ref, g_ref, wg_ref, wu_ref, wd_ref, o_ref, h_ref):
    j = pl.program_id(1)

    @pl.when(j == 0)
    def _():
        x = x_ref[...]
        ms = jnp.mean(x * x, axis=-1, keepdims=True)
        h_ref[...] = (x * lax.rsqrt(ms + NORM_EPS) * g_ref[...]).astype(BF16)
        o_ref[...] = jnp.zeros_like(o_ref)

    h = h_ref[...]
    gate = _mm(h, wg_ref[...])
    up = _mm(h, wu_ref[...])
    act = (gate * _sigmoid(gate) * up).astype(BF16)
    o_ref[...] += _mm(act, wd_ref[...])

    @pl.when(j == pl.num_programs(1) - 1)
    def _():
        o_ref[...] = x_ref[...] + 0.5 * o_ref[...]


def _ffn(x, g, wg, wu, wd, *, tm=1024, tf=512):
    t, d = x.shape
    f = wg.shape[1]
    return pl.pallas_call(
        _ffn_kernel,
        out_shape=jax.ShapeDtypeStruct((t, d), F32),
        grid=(t // tm, f // tf),
        in_specs=[
            pl.BlockSpec((tm, d), lambda i, j: (i, 0)),
            pl.BlockSpec((1, d), lambda i, j: (0, 0)),
            pl.BlockSpec((d, tf), lambda i, j: (0, j)),
            pl.BlockSpec((d, tf), lambda i, j: (0, j)),
            pl.BlockSpec((tf, d), lambda i, j: (j, 0)),
        ],
        out_specs=pl.BlockSpec((tm, d), lambda i, j: (i, 0)),
        scratch_shapes=[pltpu.VMEM((tm, d), BF16)],
        compiler_params=_cparams(("parallel", "arbitrary")),
        name="ffn",
    )(x, g, wg, wu, wd)


def _norm_proj_kernel(x_ref, g_ref, w_ref, o_ref):
    x = x_ref[...]
    ms = jnp.mean(x * x, axis=-1, keepdims=True)
    h = (x * lax.rsqrt(ms + NORM_EPS) * g_ref[...]).astype(BF16)
    o_ref[...] = _mm(h, w_ref[...]).astype(o_ref.dtype)


def _norm_proj(x, g, w, *, tm=256, name="norm_proj"):
    t, d = x.shape
    n = w.shape[1]
    return pl.pallas_call(
        _norm_proj_kernel,
        out_shape=jax.ShapeDtypeStruct((t, n), F32),
        grid=(t // tm,),
        in_specs=[
            pl.BlockSpec((tm, d), lambda i: (i, 0)),
            _const_spec((1, d)),
            _const_spec((d, n)),
        ],
        out_specs=pl.BlockSpec((tm, n), lambda i: (i, 0)),
        compiler_params=_cparams(("parallel",)),
        name=name,
    )(x, g, w)


def _rwkv_prep_kernel(x_ref, gm_ref, win_ref, mix_ref, wwa_ref, gup_ref, w0_ref, a0_ref, kk_ref, ka_ref,
                      rk_ref, seg_ref, tri_ref,
                      rt_ref, at_ref, kt_ref, bt_ref, kh_ref, bh_ref, v_ref, dc_ref, bonus_ref,
                      gate_ref, last_ref, *, seq_tiles):
    i = pl.program_id(0)
    ts = x_ref.shape[0]
    w = rt_ref.shape[1]
    x = x_ref[...]
    ms = jnp.mean(x * x, axis=-1, keepdims=True)
    p = _mm((x * lax.rsqrt(ms + NORM_EPS) * gm_ref[...]).astype(BF16), win_ref[...])

    @pl.when(i == 0)
    def _():
        last_ref[...] = jnp.zeros_like(last_ref)

    prev = last_ref[7:8, :]
    prev = jnp.where(i % seq_tiles == 0, jnp.zeros_like(prev), prev)
    last_ref[...] = p[ts - 8:ts, :]
    shifted = pltpu.roll(p, 1, axis=0)
    shifted = jnp.where(_iota(p.shape, 0) == 0, prev, shifted)
    xs = p + mix_ref[...] * (shifted - p)

    r = xs[:, 0:w]
    k = xs[:, w:2 * w]
    v = xs[:, 2 * w:3 * w]
    lo = 3 * w
    pwa = xs[:, lo:lo + LANE]
    pg = xs[:, lo + LANE:lo + 3 * LANE]
    lane = _iota(pwa.shape, 1)
    z = jnp.where(lane < DECAY_LORA, jnp.tanh(pwa), pwa)
    wa = _dot3(z, wwa_ref[...])
    wl = w0_ref[...] + wa[:, :w]
    neg = -wl
    softplus = jnp.maximum(neg, 0.0) + jnp.log(1.0 + jnp.exp(-jnp.abs(neg)))
    lw = -jnp.exp(-softplus - 0.5)
    a = _sigmoid(a0_ref[...] + wa[:, w:])
    gate_ref[...] = _dot3(_sigmoid(pg), gup_ref[...])

    seg = seg_ref[...]
    kk = k * kk_ref[...]
    ss = _dot_exact_rhs(kk * kk, seg)
    kk = kk * lax.rsqrt(jnp.maximum(ss, 1e-24))
    k2 = k * (1.0 + (a - 1.0) * ka_ref[...])
    bonus_ref[...] = _dot_exact_rhs(r * k2 * rk_ref[...], seg) * v

    gc = _dot_exact_lhs(tri_ref[...], lw)
    nc = ts // CHUNK
    ends = [gc[(q + 1) * CHUNK - 1:(q + 1) * CHUNK, :] for q in range(nc)]
    gend = jnp.concatenate([jnp.broadcast_to(e, (CHUNK, w)) for e in ends], axis=0)
    to_end = jnp.exp(gend - gc)
    e_in = jnp.exp(gc)
    e_out = jnp.exp(-gc)
    b = kk * a
    rt_ref[...] = r * e_in
    at_ref[...] = -kk * jnp.exp(gc - lw)
    kt_ref[...] = k2 * e_out
    bt_ref[...] = b * e_out
    kh_ref[...] = k2 * to_end
    bh_ref[...] = b * to_end
    v_ref[...] = v
    dc_ref[0] = jnp.concatenate([jnp.exp(e) for e in ends] + [jnp.zeros((8 - nc, w), F32)], axis=0)


def _rwkv_prep(x, g_mix, w_in, mix, wwa, gup, w0, a0, k_k, k_a, r_k, seg, tri, *, seq, ts=256):
    t, d = x.shape
    pc = w_in.shape[1]
    w = w0.shape[1]
    nt = t // ts
    row = lambda i: (i, 0)
    tok = pl.BlockSpec((ts, w), row)
    tok_shape = jax.ShapeDtypeStruct((t, w), F32)
    return pl.pallas_call(
        functools.partial(_rwkv_prep_kernel, seq_tiles=seq // ts),
        out_shape=[tok_shape] * 7 + [jax.ShapeDtypeStruct((nt, 8, w), F32), tok_shape, tok_shape],
        grid=(nt,),
        in_specs=[
            pl.BlockSpec((ts, d), row),
            _const_spec((1, d)),
            _const_spec(w_in.shape),
            _const_spec((1, pc)),
            _const_spec(wwa.shape),
            _const_spec(gup.shape),
            _const_spec((1, w)), _const_spec((1, w)), _const_spec((1, w)), _const_spec((1, w)),
            _const_spec((1, w)),
            _const_spec(seg.shape),
            _const_spec(tri.shape),
        ],
        out_specs=[tok] * 7 + [pl.BlockSpec((1, 8, w), lambda i: (i, 0, 0)), tok, tok],
        scratch_shapes=[pltpu.VMEM((8, pc), F32)],
        compiler_params=_cparams(("arbitrary",)),
        name="rwkv_prep",
    )(x, g_mix, w_in, mix, wwa, gup, w0, a0, k_k, k_a, r_k, seg, tri)


PASSES_SCAN = 1
SCAN_CHUNKS_PER_STEP = 4
INTRA_CHUNKS_PER_STEP = 4
QUAD = 4


def _operand(x, passes):
    return _split2(x) if passes == 3 else (x.astype(BF16),)


def _prod(a, b, mm=_mm):
    if len(a) == 2 and len(b) == 2:
        return mm(a[0], b[0]) + (mm(a[0], b[1]) + mm(a[1], b[0]))
    return mm(a[0], b[0])


def _block_diag(y, n):
    c = y.shape[0]
    tiled = jnp.concatenate([y] * (y.shape[1] // n), axis=0)
    keep = (_iota(tiled.shape, 0) // c) == (_iota(tiled.shape, 1) // n)
    return jnp.where(keep, tiled, jnp.zeros_like(tiled))


def _quad_mm(x, y, n, mm=_mm):
    return mm(x.astype(BF16), _block_diag(y.astype(BF16), n))


def _unit_lower_inverse(a_list, row, col):
    n = CHUNK
    eye = (row == col).astype(F32)
    same8 = (row // 8) == (col // 8)
    a8 = [jnp.where(same8, a, 0.0) for a in a_list]
    d8 = [_block_diag(x.astype(BF16), n) for x in a8]
    a8_2 = [_mm(x.astype(BF16), d) for x, d in zip(a8, d8)]
    d8_2 = [_block_diag(x.astype(BF16), n) for x in a8_2]
    a8_4 = [_mm(x.astype(BF16), d) for x, d in zip(a8_2, d8_2)]
    p = [eye + x + x2 + _mm(x.astype(BF16), d2) for x, x2, d2 in zip(a8, a8_2, d8_2)]
    t = [pp + _quad_mm(pp, x4, n) for pp, x4 in zip(p, a8_4)]
    m = 16
    while m <= CHUNK:
        sel = ((row // m) == (col // m)) & ((row // (m // 2)) != (col // (m // 2)))
        mid = [_quad_mm(x, jnp.where(sel, a, 0.0), n) for x, a in zip(t, a_list)]
        t = [x + _quad_mm(md, x, n) for x, md in zip(t, mid)]
        m *= 2
    return t


def _rwkv_intra_kernel(rt_ref, at_ref, kt_ref, bt_ref, kh_ref, bh_ref, v_ref, dc_ref,
                       rr_ref, o0_ref, gh_ref, *, heads, chunks_per_tile):
    n = RWKV_HEAD_DIM
    qw = QUAD * n
    cps = gh_ref.shape[0]
    i = pl.program_id(0)
    row = _iota((CHUNK, qw), 0)
    col = _iota((CHUNK, qw), 1) % n
    strict = col < row
    incl = col <= row
    dc_all = dc_ref[0]
    dc_rows = []
    for c in range(cps):
        r = dc_all[c:c + 1, :]
        for q in range(1, chunks_per_tile // cps):
            r = jnp.where(i % (chunks_per_tile // cps) == q, dc_all[q * cps + c:q * cps + c + 1, :], r)
        dc_rows.append(r)

    units = [(slice(c * CHUNK, (c + 1) * CHUNK), slice(j * qw, (j + 1) * qw))
             for c in range(cps) for j in range(heads // QUAD)]
    at = [at_ref[r, s] for r, s in units]
    rt = [rt_ref[r, s] for r, s in units]
    v = [v_ref[r, s] for r, s in units]
    bd_b = [_block_diag(bt_ref[r, s].astype(BF16), n) for r, s in units]
    bd_k = [_block_diag(kt_ref[r, s].astype(BF16), n) for r, s in units]
    ar = [jnp.concatenate([a, r], axis=0).astype(BF16) for a, r in zip(at, rt)]
    mb = [_mm_nt(x, d) for x, d in zip(ar, bd_b)]
    mk = [_mm_nt(x, d) for x, d in zip(ar, bd_k)]
    a_ab = [jnp.where(strict, m[:CHUNK], 0.0) for m in mb]
    a_rb = [jnp.where(incl, m[CHUNK:], 0.0) for m in mb]
    akrk = [jnp.concatenate([jnp.where(strict, m[:CHUNK], 0.0), jnp.where(incl, m[CHUNK:], 0.0)], axis=0)
            for m in mk]
    avv = [_quad_mm(x, y, n) for x, y in zip(akrk, v)]
    tinv = _unit_lower_inverse(a_ab, row, col)
    a_new = [_quad_mm(t, a, n) for t, a in zip(tinv, at)]
    u0 = [_quad_mm(t, w[:CHUNK], n) for t, w in zip(tinv, avv)]
    for u, (r, s) in enumerate(units):
        rr_ref[r, s] = rt[u] + _quad_mm(a_rb[u], a_new[u], n)
        o0_ref[r, s] = _quad_mm(a_rb[u], u0[u], n) + avv[u][CHUNK:]
    eye = _iota((n, n), 0) == _iota((n, n), 1)
    nq = heads // QUAD
    gz, kv = [], []
    for c in range(cps):
        r = slice(c * CHUNK, (c + 1) * CHUNK)
        for h in range(heads):
            u, s = c * nq + h // QUAD, slice((h % QUAD) * n, (h % QUAD + 1) * n)
            hs = slice(h * n, (h + 1) * n)
            z = jnp.concatenate([a_new[u][:, s], u0[u][:, s]], axis=1).astype(BF16)
            gz.append(_mm_tn(bh_ref[r, hs].astype(BF16), z))
            kv.append(_mm_tn(kh_ref[r, hs].astype(BF16), v[u][:, s].astype(BF16)))
    for c in range(cps):
        for h in range(heads):
            hs = slice(h * n, (h + 1) * n)
            dmat = jnp.where(eye, jnp.broadcast_to(dc_rows[c][:, hs], (n, n)), 0.0)
            gh_ref[c, h] = gz[c * heads + h] + jnp.concatenate([dmat, kv[c * heads + h]], axis=1)


def _rwkv_scan_kernel(rr_ref, o0_ref, gh_ref, o_ref, state_ref, *, heads):
    n = RWKV_HEAD_DIM

    @pl.when(pl.program_id(1) == 0)
    def _():
        state_ref[...] = jnp.zeros_like(state_ref)

    sls = [slice(h * n, (h + 1) * n) for h in range(heads)]
    state = [state_ref[h] for h in range(heads)]
    for c in range(gh_ref.shape[0]):
        rows = slice(c * CHUNK, (c + 1) * CHUNK)
        h0 = [_operand(x, PASSES_SCAN) for x in state]
        outs = [_prod(_operand(rr_ref[rows, s], PASSES_SCAN), x) + o0_ref[rows, s] for s, x in zip(sls, h0)]
        state = [_prod(_operand(gh_ref[c, h, :, 0:n], PASSES_SCAN), h0[h]) + gh_ref[c, h, :, n:2 * n]
                 for h in range(heads)]
        for h, s in enumerate(sls):
            o_ref[rows, s] = outs[h]
    for h in range(heads):
        state_ref[h] = state[h]


def _rwkv_chunk(rt, at, kt, bt, kh, bh, v, dc, *, batch, seq, prep_ts):
    t, w = rt.shape
    n = RWKV_HEAD_DIM
    heads = w // n
    nchunk = seq // CHUNK
    cpt = prep_ts // CHUNK
    cps = INTRA_CHUNKS_PER_STEP
    tok = pl.BlockSpec((cps * CHUNK, w), lambda i: (i, 0))
    tok_shape = jax.ShapeDtypeStruct((t, w), F32)
    rr, o0, gh = pl.pallas_call(
        functools.partial(_rwkv_intra_kernel, heads=heads, chunks_per_tile=cpt),
        out_shape=[tok_shape, tok_shape, jax.ShapeDtypeStruct((t // CHUNK, heads, n, 2 * n), F32)],
        grid=(t // (cps * CHUNK),),
        in_specs=[tok] * 7 + [pl.BlockSpec((1, 8, w), lambda i: (i * cps // cpt, 0, 0))],
        out_specs=[tok, tok, pl.BlockSpec((cps, heads, n, 2 * n), lambda i: (i, 0, 0, 0))],
        compiler_params=_cparams(("parallel",)),
        name="rwkv_intra",
    )(rt, at, kt, bt, kh, bh, v, dc)
    steps = nchunk // SCAN_CHUNKS_PER_STEP
    tok2 = pl.BlockSpec((SCAN_CHUNKS_PER_STEP * CHUNK, w), lambda b, c: (b * steps + c, 0))
    return pl.pallas_call(
        functools.partial(_rwkv_scan_kernel, heads=heads),
        out_shape=tok_shape,
        grid=(batch, steps),
        in_specs=[tok2, tok2, pl.BlockSpec((SCAN_CHUNKS_PER_STEP, heads, n, 2 * n),
                                           lambda b, c: (b * steps + c, 0, 0, 0))],
        out_specs=tok2,
        scratch_shapes=[pltpu.VMEM((heads, n, n), F32)],
        compiler_params=_cparams(("parallel", "arbitrary")),
        name="rwkv_scan",
    )(rr, o0, gh)


def _rope_lanes(x, cos_t, sin_a, sin_b):
    width = x.shape[1]
    up = pltpu.roll(x, width - ROPE_HALF, axis=1)
    dn = pltpu.roll(x, ROPE_HALF, axis=1)
    return x * cos_t + up * sin_a + dn * sin_b


def _tile_lanes(tab, width):
    return jnp.concatenate([tab] * (width // tab.shape[1]), axis=1)


def _nsa_prep_kernel(x_ref, gm_ref, win_ref, cos_ref, sa_ref, sb_ref, qn_ref, kn_ref, seg_ref,
                     q_ref, ks_ref, vs_ref, kw_ref, vw_ref, g_ref, gk_ref, gv_ref, *, seq_tiles):
    dh = NSA_HEAD_DIM
    x = x_ref[...]
    ms = jnp.mean(x * x, axis=-1, keepdims=True)
    p = _mm((x * lax.rsqrt(ms + NORM_EPS) * gm_ref[...]).astype(BF16), win_ref[...])
    qw = q_ref.shape[1]
    kvw = NSA_KV_HEADS * dh
    seg = seg_ref[...]
    cos_t, sin_a, sin_b = cos_ref[...], sa_ref[...], sb_ref[...]

    def norm_rope(x, gain):
        wd = x.shape[1]
        ms = _head_mean(x * x, seg, dh)
        y = x * lax.rsqrt(ms + NORM_EPS) * gain
        return _rope_lanes(y, _tile_lanes(cos_t, wd), _tile_lanes(sin_a, wd), _tile_lanes(sin_b, wd))

    q = norm_rope(p[:, 0:qw], qn_ref[...])
    q_ref[...] = q * (dh ** -0.5)
    base = qw + 2 * kvw
    ks = norm_rope(p[:, base:base + kvw], kn_ref[1:2, :])
    vs = p[:, base + kvw:base + 2 * kvw].astype(BF16)
    kw = norm_rope(p[:, base + 2 * kvw:base + 3 * kvw], kn_ref[2:3, :]).astype(BF16)
    vw = p[:, base + 3 * kvw:base + 4 * kvw].astype(BF16)
    ts = x_ref.shape[0]
    tpos = (pl.program_id(0) % seq_tiles) * ts + _iota((ts, dh), 0)
    onehot = jnp.where(tpos // SEL_BLOCK == _iota((ts, dh), 1), 1.0, 0.0)
    for h in range(NSA_KV_HEADS):
        sl = slice(h * dh, (h + 1) * dh)
        ks_ref[0, h] = jnp.concatenate([ks[:, sl], onehot], axis=1).astype(BF16)
        vs_ref[0, h] = vs[:, sl]
        kw_ref[0, h] = kw[:, sl]
        vw_ref[0, h] = vw[:, sl]
    sig = _sigmoid(p[:, base + 4 * kvw:base + 4 * kvw + LANE])
    sig_t = sig.T
    per_head = 3 * NSA_GROUP
    for h in range(NSA_KV_HEADS):
        g_ref[h] = sig_t[per_head * h:per_head * h + GATE_ROWS, :]
    kc3 = p[:, qw:qw + kvw].reshape(ts // CMP_STRIDE, CMP_STRIDE, kvw)
    vc3 = p[:, qw + kvw:qw + 2 * kvw].reshape(ts // CMP_STRIDE, CMP_STRIDE, kvw)
    for i in range(CMP_STRIDE):
        kci, vci = kc3[:, i, :], vc3[:, i, :]
        for h in range(NSA_KV_HEADS):
            gk_ref[0, h, :, i * dh:(i + 1) * dh] = kci[:, h * dh:(h + 1) * dh]
            gv_ref[0, h, :, i * dh:(i + 1) * dh] = vci[:, h * dh:(h + 1) * dh]


def _nsa_prep(x, g_mix, w_in, cos_t, sin_a, sin_b, qn, kn, seg, *, batch, seq, ts=256):
    t, d = x.shape
    qw = NSA_HEADS * NSA_HEAD_DIM
    st = seq // ts
    tab = pl.BlockSpec((ts, LANE), lambda i: (i % st, 0))
    hm = pl.BlockSpec((1, NSA_KV_HEADS, ts, NSA_HEAD_DIM), lambda i: (i // st, 0, i % st, 0))
    hm_shape = jax.ShapeDtypeStruct((batch, NSA_KV_HEADS, seq, NSA_HEAD_DIM), BF16)
    assert seq // SEL_BLOCK <= NSA_HEAD_DIM
    aug = pl.BlockSpec((1, NSA_KV_HEADS, ts, 2 * NSA_HEAD_DIM), lambda i: (i // st, 0, i % st, 0))
    aug_shape = jax.ShapeDtypeStruct((batch, NSA_KV_HEADS, seq, 2 * NSA_HEAD_DIM), BF16)
    grp_w = CMP_STRIDE * NSA_HEAD_DIM
    grp = pl.BlockSpec((1, NSA_KV_HEADS, ts // CMP_STRIDE, grp_w), lambda i: (i // st, 0, i % st, 0))
    grp_shape = jax.ShapeDtypeStruct((batch, NSA_KV_HEADS, seq // CMP_STRIDE, grp_w), F32)
    return pl.pallas_call(
        functools.partial(_nsa_prep_kernel, seq_tiles=st),
        out_shape=[jax.ShapeDtypeStruct((t, qw), F32), aug_shape] + [hm_shape] * 3
        + [jax.ShapeDtypeStruct((NSA_KV_HEADS, GATE_ROWS, t), F32)] + [grp_shape] * 2,
        grid=(t // ts,),
        in_specs=[pl.BlockSpec((ts, d), lambda i: (i, 0)), _const_spec((1, d)), _const_spec(w_in.shape),
                  tab, tab, tab,
                  _const_spec(qn.shape), _const_spec(kn.shape), _const_spec(seg.shape)],
        out_specs=[pl.BlockSpec((ts, qw), lambda i: (i, 0)), aug] + [hm] * 3
        + [pl.BlockSpec((NSA_KV_HEADS, GATE_ROWS, ts), lambda i: (0, 0, i))] + [grp] * 2,
        compiler_params=_cparams(("parallel",)),
        name="nsa_prep",
    )(x, g_mix, w_in, cos_t, sin_a, sin_b, qn, kn, seg)


def _gelu_tanh(x):
    return 0.5 * x * (1.0 + jnp.tanh(np.sqrt(2.0 / np.pi).astype(np.float32) * (x + 0.044715 * (x * x * x))))


def _compress_kernel(gk_ref, gv_ref, pk_ref, pv_ref, k1_ref, k2_ref, v1_ref, v2_ref, kn_ref,
                     cos_ref, sin_ref, rot_ref, kc_ref, vc_ref):
    half = k1_ref.shape[0] // 2

    def mlp(g, pos, w1_ref, w2_ref):
        ya = _dot3(g, w1_ref[0:half, :])
        yb = _dot3(g, w1_ref[half:, :])
        bias = _dot3(jnp.broadcast_to(pos, (8, pos.shape[1])), w1_ref[...])[0:1, :]
        n = g.shape[0]
        hid = ya + pltpu.roll(yb, n - 1, axis=0) + bias
        return _dot3(_gelu_tanh(hid), w2_ref[...])

    hk, ng, gw = gk_ref.shape[1:]
    tile_rows = lambda tab: jnp.concatenate([tab] * hk, axis=0)
    kc = mlp(gk_ref[0].reshape(hk * ng, gw), pk_ref[...], k1_ref, k2_ref)
    ms = jnp.mean(kc * kc, axis=-1, keepdims=True)
    kc = kc * lax.rsqrt(ms + NORM_EPS) * kn_ref[0:1, :]
    kc = kc * tile_rows(cos_ref[...]) + _dot_exact_rhs(kc, rot_ref[...]) * tile_rows(sin_ref[...])
    kc_ref[0] = kc.reshape(hk, ng, kc.shape[1])
    vc = mlp(gv_ref[0].reshape(hk * ng, gw), pv_ref[...], v1_ref, v2_ref)
    vc_ref[0] = vc.reshape(hk, ng, vc.shape[1])


def _nsa_compress(gk, gv, pk, pv, k1, k2, v1, v2, kn, cos_c, sin_c, rot):
    b, hk, ng, gw = gk.shape
    dh = NSA_HEAD_DIM
    grp = pl.BlockSpec((1, hk, ng, gw), lambda i: (i, 0, 0, 0))
    out = pl.BlockSpec((1, hk, ng, dh), lambda i: (i, 0, 0, 0))
    shape = jax.ShapeDtypeStruct((b, hk, ng, dh), F32)
    consts = [pk, pv, k1, k2, v1, v2, kn, cos_c, sin_c, rot]
    return pl.pallas_call(
        _compress_kernel,
        out_shape=[shape, shape],
        grid=(b,),
        in_specs=[grp, grp] + [_const_spec(c.shape) for c in consts],
        out_specs=[out, out],
        compiler_params=_cparams(("parallel",)),
        name="nsa_compress",
    )(gk, gv, *consts)


def _nsa_attn_kernel(q_ref, kc_ref, vc_ref, ks_ref, vs_ref, kw_ref, vw_ref, g_ref, ovt_ref, o_ref):
    dh = NSA_HEAD_DIM
    grp = NSA_GROUP
    nh = kc_ref.shape[1]
    qi = pl.program_id(2)
    tq = q_ref.shape[0]
    hc = grp * tq
    cols = nh * hc
    t0 = qi * tq
    ncmp = kc_ref.shape[2]
    nsel = ovt_ref.shape[0]
    kb = ATTN_KEY_BLOCK
    span = WINDOW + tq
    heads = range(nh)

    def lanes(xs):
        return jnp.concatenate(xs, axis=1)

    def every_head(x):
        return lanes([x] * (nh * grp))

    q = q_ref[...]
    q4 = [jnp.concatenate([q[:, (h * grp + g) * dh:(h * grp + g + 1) * dh] for g in range(grp)], axis=0)
          for h in heads]
    q4b = [(x * LOG2_E).astype(BF16) for x in q4]

    st = lanes([_dot3(kc_ref[0, h], q4[h], _mm_nt) for h in heads])
    tl = t0 + _iota((ncmp, cols), 1) % tq
    cmask = _iota((ncmp, cols), 0) * CMP_STRIDE + (CMP_BLOCK - 1) <= tl
    sm = jnp.where(cmask, st, MASKED)
    e = jnp.where(cmask, jnp.exp(sm - jnp.max(sm, axis=0, keepdims=True)), 0.0)
    pt = e / jnp.maximum(jnp.sum(e, axis=0, keepdims=True), 1e-30)
    ptb = pt.astype(BF16)
    o_cmp = lanes([_mm_tn(vc_ref[0, h].astype(BF16), ptb[:, h * hc:(h + 1) * hc]) for h in heads])
    psum = []
    for h in heads:
        acc = pt[:, h * hc:h * hc + tq]
        for g in range(1, grp):
            acc = acc + pt[:, h * hc + g * tq:h * hc + (g + 1) * tq]
        psum.append(acc)
    psum = lanes(psum)

    imp = _dot_exact_lhs(ovt_ref[...], psum)
    blk = _iota(imp.shape, 0)
    cur = (t0 + _iota(imp.shape, 1) % tq) // SEL_BLOCK
    forced = (blk == 0) | (blk == cur) | (blk == cur - 1)
    imp = jnp.where(forced, jnp.inf, jnp.where(blk > cur, -jnp.inf, imp))
    rank = jnp.zeros(imp.shape, jnp.int32)
    for m in range(nsel):
        im = imp[m:m + 1, :]
        ahead = (im > imp) | ((im == imp) & (m < blk))
        rank = rank + ahead.astype(jnp.int32)
    sel_bias = jnp.where(rank < min(SEL_TOP, nsel), 0.0, MASKED).astype(BF16)

    eye = jnp.where(_iota((nsel, dh), 0) == _iota((nsel, dh), 1), 1.0, 0.0).astype(BF16)
    bias_q = _mm_tn(sel_bias, eye)
    q_aug = [jnp.concatenate([q4[h] * LOG2_E, jnp.concatenate([bias_q[h * tq:(h + 1) * tq]] * grp, axis=0)],
                             axis=1).astype(BF16) for h in heads]

    def attend(k_ref, v_ref, qs, k0, state, bias=None):
        m_run, l_run, acc = state
        rows = pl.ds(pl.multiple_of(k0, kb), kb)
        s = lanes([_mm_nt(k_ref[0, h, rows, :], qs[h]) for h in heads])
        if bias is not None:
            s = s + bias
        m_new = jnp.maximum(m_run, jnp.max(s, axis=0, keepdims=True))
        alpha = jnp.exp2(m_run - m_new)
        pb = jnp.exp2(s - m_new)
        l_new = alpha * l_run + jnp.sum(pb, axis=0, keepdims=True)
        pb = pb.astype(BF16)
        pv = lanes([_mm_tn(v_ref[0, h, rows, :], pb[:, h * hc:(h + 1) * hc]) for h in heads])
        return m_new, l_new, alpha * acc + pv

    init = (jnp.full((1, cols), MASKED, F32), jnp.zeros((1, cols), F32), jnp.zeros((dh, cols), F32))
    state = lax.fori_loop(0, t0 // kb, lambda j, st: attend(ks_ref, vs_ref, q_aug, j * kb, st), init)
    for d in range(tq // kb):
        causal = jnp.where(d * kb + _iota((kb, tq), 0) <= _iota((kb, tq), 1), 0.0, MASKED)
        state = attend(ks_ref, vs_ref, q_aug, t0 + d * kb, state, every_head(causal))
    o_slc = state[2] / state[1]

    w0 = pl.multiple_of(jnp.maximum(t0 - WINDOW, 0), tq)
    wrows = pl.ds(w0, span)
    kpos = w0 + _iota((span, tq), 0)
    tw = t0 + _iota((span, tq), 1)
    wbias = jnp.where((kpos <= tw) & (kpos > tw - WINDOW), 0.0, MASKED)
    s = lanes([_mm_nt(kw_ref[0, h, wrows, :], q4b[h]) for h in heads]) + every_head(wbias)
    p = jnp.exp2(s - jnp.max(s, axis=0, keepdims=True))
    pb = p.astype(BF16)
    o_win = (lanes([_mm_tn(vw_ref[0, h, wrows, :], pb[:, h * hc:(h + 1) * hc]) for h in heads])
             / jnp.sum(p, axis=0, keepdims=True))

    gates = g_ref[...]
    grow = [lanes([gates[h, 3 * g + br:3 * g + br + 1, :] for h in heads for g in range(grp)])
            for br in range(3)]
    o4 = grow[0] * o_cmp + grow[1] * o_slc + grow[2] * o_win
    o_ref[...] = jnp.concatenate([o4[:, c * tq:(c + 1) * tq] for c in range(nh * grp)], axis=0)


def _nsa_attn(q, kc, vc, ks_aug, vs, kw, vw, gates_t, overlap_t, *, batch, seq, tq=256,
              nh=ATTN_KV_PER_STEP):
    t, qw = q.shape
    dh = NSA_HEAD_DIM
    gw = nh * NSA_GROUP * dh
    st = seq // tq
    ncmp = kc.shape[2]
    cmp_spec = pl.BlockSpec((1, nh, ncmp, dh), lambda b, h, i: (b, h, 0, 0))
    kv_spec = pl.BlockSpec((1, nh, seq, dh), lambda b, h, i: (b, h, 0, 0))
    aug_spec = pl.BlockSpec((1, nh, seq, 2 * dh), lambda b, h, i: (b, h, 0, 0))
    return pl.pallas_call(
        _nsa_attn_kernel,
        out_shape=jax.ShapeDtypeStruct((qw, t), F32),
        grid=(batch, NSA_KV_HEADS // nh, st),
        in_specs=[
            pl.BlockSpec((tq, gw), lambda b, h, i: (b * st + i, h)),
            cmp_spec, cmp_spec, aug_spec, kv_spec, kv_spec, kv_spec,
            pl.BlockSpec((nh, gates_t.shape[1], tq), lambda b, h, i: (h, 0, b * st + i)),
            _const_spec(overlap_t.shape),
        ],
        out_specs=pl.BlockSpec((gw, tq), lambda b, h, i: (h, b * st + i)),
        compiler_params=_cparams(("parallel", "parallel", "arbitrary")),
        name="nsa_attn",
    )(q, kc, vc, ks_aug, vs, kw, vw, gates_t, overlap_t)


def _merge_kernel(x_ref, o_ref, bonus_ref, gate_ref, ybt_ref, pg_ref, gnw_ref, gnb_ref, seg_ref,
                  ua_ref, ub_ref, wo_ref, out_ref):
    d = x_ref.shape[1]
    n = RWKV_HEAD_DIM
    seg = seg_ref[...]
    o = o_ref[...]
    mu = _dot_exact_rhs(o, seg) * (1.0 / n)
    dlt = o - mu
    var = _head_mean(dlt * dlt, seg, n)
    on = dlt * lax.rsqrt(var + GN_EPS) * gnw_ref[...] + gnb_ref[...]
    ya = ((on + bonus_ref[...]) * gate_ref[...]).astype(BF16)
    yb_t = ybt_ref[...].astype(BF16)
    merged = (_sigmoid(pg_ref[:, 0:d]) * _mm(ya, ua_ref[...])
              + _sigmoid(pg_ref[:, d:2 * d]) * _mm_tn(yb_t, ub_ref[...]))
    out_ref[...] = x_ref[...] + _mm(merged.astype(BF16), wo_ref[...])


def _merge(x, o_rwkv, bonus, gate, yb_t, pg, gnw, gnb, seg, ua, ub, wo, *, tm=256):
    t, d = x.shape
    w = o_rwkv.shape[1]
    row = lambda i: (i, 0)
    tokw = pl.BlockSpec((tm, w), row)
    return pl.pallas_call(
        _merge_kernel,
        out_shape=jax.ShapeDtypeStruct((t, d), F32),
        grid=(t // tm,),
        in_specs=[pl.BlockSpec((tm, d), row), tokw, tokw, tokw,
                  pl.BlockSpec((yb_t.shape[0], tm), lambda i: (0, i)),
                  pl.BlockSpec((tm, 2 * d), row),
                  _const_spec((1, w)), _const_spec((1, w)), _const_spec(seg.shape),
                  _const_spec(ua.shape), _const_spec(ub.shape), _const_spec(wo.shape)],
        out_specs=pl.BlockSpec((tm, d), row),
        compiler_params=_cparams(("parallel",)),
        name="merge",
    )(x, o_rwkv, bonus, gate, yb_t, pg, gnw, gnb, seg, ua, ub, wo)


def _block_diag_ones(width, block):
    idx = np.arange(width) // block
    return jnp.asarray(idx[:, None] == idx[None, :], BF16)


def _chunk_lower_ones(ts):
    i = np.arange(ts)
    return jnp.asarray((i[:, None] // CHUNK == i[None, :] // CHUNK) & (i[None, :] <= i[:, None]), BF16)


def _rope_tables(pos):
    inv = ROPE_THETA ** (-jnp.arange(ROPE_HALF, dtype=F32) / ROPE_HALF)
    ang = jnp.asarray(pos).astype(F32)[:, None] * inv[None, :]
    cos, sin = jnp.cos(ang), jnp.sin(ang)
    n = ang.shape[0]
    pad = jnp.zeros((n, NSA_HEAD_DIM - ROPE_DIM), F32)
    zero = jnp.zeros_like(sin)
    cos_h = jnp.concatenate([cos, cos, pad + 1.0], axis=1)
    sa_h = jnp.concatenate([-sin, zero, pad], axis=1)
    sb_h = jnp.concatenate([zero, sin, pad], axis=1)
    return cos_h, sa_h, sb_h


def _rot_half_matrix():
    r = np.zeros((NSA_HEAD_DIM, NSA_HEAD_DIM), np.float32)
    for l in range(ROPE_HALF):
        r[l + ROPE_HALF, l] = -1.0
        r[l, l + ROPE_HALF] = 1.0
    return jnp.asarray(r, BF16)


def _overlap_matrix_t(ncmp_pad, nsel):
    cs = np.arange(ncmp_pad)[None, :] * CMP_STRIDE
    ss = np.arange(nsel)[:, None] * SEL_BLOCK
    ov = np.clip(np.minimum(cs + CMP_BLOCK, ss + SEL_BLOCK) - np.maximum(cs, ss), 0, None) / CMP_BLOCK
    return jnp.asarray(ov, BF16)


def _pad_cols(x, width):
    return jnp.pad(x, ((0, 0), (0, width - x.shape[1])))


def _layer(x, l, ffn1_norm, ffn1_w_gate, ffn1_w_up, ffn1_w_down, mix_norm, w_in,
           rwkv_mix, rwkv_w0, rwkv_w_up, rwkv_a0, rwkv_a_up, rwkv_g_up,
           rwkv_k_k, rwkv_k_a, rwkv_r_k, rwkv_gn_w, rwkv_gn_b,
           nsa_q_norm, nsa_k_norm, cmp_pos_k, cmp_pos_v,
           cmp_k_w1, cmp_k_w2, cmp_v_w1, cmp_v_w2,
           w_branch_rwkv, w_branch_nsa, w_out,
           ffn2_norm, ffn2_w_gate, ffn2_w_up, ffn2_w_down, *, batch, seq):
    t, d = x.shape
    w = rwkv_w0.shape[1]
    dh = NSA_HEAD_DIM
    qw = NSA_HEADS * dh
    kvw = NSA_KV_HEADS * dh
    prep_ts = 256
    row = lambda v: v.reshape(1, -1)

    x = _ffn(x, row(ffn1_norm[l]), ffn1_w_gate[l].astype(BF16), ffn1_w_up[l].astype(BF16),
             ffn1_w_down[l].astype(BF16))

    wi = w_in[l]
    rwkv_cols = 3 * w + DECAY_LORA + ICLR_LORA + GATE_LORA
    rwkv_pad = 3 * w + 3 * LANE
    nsa_cols = qw + 6 * kvw + 3 * NSA_HEADS
    nsa_pad = qw + 6 * kvw + LANE
    g_mix = row(mix_norm[l])
    w_rwkv = _pad_cols(wi[:, :rwkv_cols], rwkv_pad).astype(BF16)
    w_nsa = _pad_cols(wi[:, rwkv_cols:rwkv_cols + nsa_cols], nsa_pad).astype(BF16)
    p_gate = _norm_proj(x, g_mix, wi[:, rwkv_cols + nsa_cols:].astype(BF16), name="proj_gate")

    wwa = jnp.zeros((LANE, 2 * w), F32)
    wwa = wwa.at[:DECAY_LORA, :w].set(rwkv_w_up[l]).at[DECAY_LORA:, w:].set(rwkv_a_up[l])
    gup = jnp.pad(rwkv_g_up[l], ((0, 2 * LANE - GATE_LORA), (0, 0)))
    seg_w = _block_diag_ones(SEG_WIDTH, RWKV_HEAD_DIM)
    (rt, at, kt, bt, kh, bh, v, dc, bonus, gate) = _rwkv_prep(
        x, g_mix, w_rwkv, _pad_cols(row(rwkv_mix[l]), rwkv_pad), wwa, gup, row(rwkv_w0[l]), row(rwkv_a0[l]),
        row(rwkv_k_k[l]), row(rwkv_k_a[l]), row(rwkv_r_k[l]), seg_w, _chunk_lower_ones(prep_ts),
        seq=seq, ts=prep_ts)
    o_rwkv = _rwkv_chunk(rt, at, kt, bt, kh, bh, v, dc, batch=batch, seq=seq, prep_ts=prep_ts)

    cos_t, sin_a, sin_b = _rope_tables(np.arange(seq))
    two = lambda tab: jnp.concatenate([tab, tab], axis=1)
    qn = jnp.tile(row(nsa_q_norm[l]), (1, NSA_HEADS))
    kn = jnp.tile(nsa_k_norm[l], (1, NSA_KV_HEADS))
    q, ks, vs, kw, vw, gates, grp_k, grp_v = _nsa_prep(
        x, g_mix, w_nsa, two(cos_t), two(sin_a), two(sin_b), qn, kn, _block_diag_ones(SEG_WIDTH, dh),
        batch=batch, seq=seq)

    ngrp = seq // CMP_STRIDE
    cend = np.arange(ngrp) * CMP_STRIDE + CMP_BLOCK - 1
    cos_c, sa_c, sb_c = _rope_tables(cend)
    kc, vc = _nsa_compress(
        grp_k, grp_v,
        cmp_pos_k[l].reshape(1, -1), cmp_pos_v[l].reshape(1, -1),
        cmp_k_w1[l], cmp_k_w2[l], cmp_v_w1[l], cmp_v_w2[l], nsa_k_norm[l],
        cos_c, sb_c - sa_c, _rot_half_matrix())
    nsel = seq // SEL_BLOCK
    y_nsa = _nsa_attn(q, kc, vc, ks, vs, kw, vw, gates, _overlap_matrix_t(ngrp, nsel),
                      batch=batch, seq=seq)

    x = _merge(x, o_rwkv, bonus, gate, y_nsa, p_gate, row(rwkv_gn_w[l]), row(rwkv_gn_b[l]), seg_w,
               w_branch_rwkv[l].astype(BF16), w_branch_nsa[l].astype(BF16), w_out[l].astype(BF16))
    return _ffn(x, row(ffn2_norm[l]), ffn2_w_gate[l].astype(BF16), ffn2_w_up[l].astype(BF16),
                ffn2_w_down[l].astype(BF16))


def kernel(x, ffn1_norm, ffn1_w_gate, ffn1_w_up, ffn1_w_down, mix_norm, w_in, rwkv_mix, rwkv_w0, rwkv_w_up, rwkv_a0, rwkv_a_up, rwkv_g_up, rwkv_k_k, rwkv_k_a, rwkv_r_k, rwkv_gn_w, rwkv_gn_b, nsa_q_norm, nsa_k_norm, cmp_pos_k, cmp_pos_v, cmp_k_w1, cmp_k_w2, cmp_v_w1, cmp_v_w2, w_branch_rwkv, w_branch_nsa, w_out, ffn2_norm, ffn2_w_gate, ffn2_w_up, ffn2_w_down):
    batch, seq, d = x.shape
    params = (ffn1_norm, ffn1_w_gate, ffn1_w_up, ffn1_w_down, mix_norm, w_in, rwkv_mix, rwkv_w0,
              rwkv_w_up, rwkv_a0, rwkv_a_up, rwkv_g_up, rwkv_k_k, rwkv_k_a, rwkv_r_k, rwkv_gn_w,
              rwkv_gn_b, nsa_q_norm, nsa_k_norm, cmp_pos_k, cmp_pos_v, cmp_k_w1, cmp_k_w2, cmp_v_w1,
              cmp_v_w2, w_branch_rwkv, w_branch_nsa, w_out, ffn2_norm, ffn2_w_gate, ffn2_w_up,
              ffn2_w_down)
    y = x.reshape(batch * seq, d)
    for l in range(ffn1_norm.shape[0]):
        y = _layer(y, l, *params, batch=batch, seq=seq)
    return y.reshape(batch, seq, d)
```

```python
import functools

import numpy as np
import jax
import jax.numpy as jnp
from jax import lax
from jax.experimental import pallas as pl
from jax.experimental.pallas import tpu as pltpu

F32 = jnp.float32
BF16 = jnp.bfloat16

RWKV_HEAD_DIM = 64
DECAY_LORA = 64
ICLR_LORA = 64
GATE_LORA = 160
GN_EPS = 64e-5
NSA_HEADS = 16
NSA_KV_HEADS = 4
NSA_GROUP = NSA_HEADS // NSA_KV_HEADS
NSA_HEAD_DIM = 64
ROPE_DIM = NSA_HEAD_DIM // 4
ROPE_HALF = ROPE_DIM // 2
ROPE_THETA = 500000.0
CMP_BLOCK = 32
CMP_STRIDE = 16
SEL_BLOCK = 64
SEL_TOP = 16
WINDOW = 512
NORM_EPS = 1e-6

LANE = 128
SEG_WIDTH = 256
CHUNK = 64
VMEM_LIMIT = 56 * 1024 * 1024
MASKED = -1e30
LOG2_E = 1.4426950408889634
GATE_ROWS = 16
ATTN_KEY_BLOCK = 256
ATTN_KV_PER_STEP = 4


def _cparams(sem):
    return pltpu.CompilerParams(dimension_semantics=sem, vmem_limit_bytes=VMEM_LIMIT)


def _const_spec(shape):
    nd = len(shape)
    return pl.BlockSpec(shape, lambda *_: (0,) * nd, pipeline_mode=pl.Buffered(1))


def _mm(a, b):
    return lax.dot_general(a, b, (((1,), (0,)), ((), ())), preferred_element_type=F32)


def _mm_nt(a, b):
    return lax.dot_general(a, b, (((1,), (1,)), ((), ())), preferred_element_type=F32)


def _mm_tn(a, b):
    return lax.dot_general(a, b, (((0,), (0,)), ((), ())), preferred_element_type=F32)


def _split2(x):
    hi = x.astype(BF16)
    lo = (x - hi.astype(F32)).astype(BF16)
    return hi, lo


def _split3(x):
    h1 = x.astype(BF16)
    r1 = x - h1.astype(F32)
    h2 = r1.astype(BF16)
    h3 = (r1 - h2.astype(F32)).astype(BF16)
    return h1, h2, h3


def _dot3(a, b, mm=_mm):
    a1, a2 = _split2(a)
    b1, b2 = _split2(b)
    return mm(a1, b1) + (mm(a1, b2) + mm(a2, b1))


def _mm_groups(a, b):
    k = b.shape[0]
    if a.shape[1] == k:
        return _mm(a, b)
    return jnp.concatenate([_mm(a[:, i * k:(i + 1) * k], b) for i in range(a.shape[1] // k)], axis=1)


def _dot_exact_rhs(a, b_bf16):
    a1, a2 = _split2(a)
    return _mm_groups(a1, b_bf16) + _mm_groups(a2, b_bf16)


def _head_mean(x, seg_bf16, width):
    return _mm_groups(x.astype(BF16), seg_bf16) * (1.0 / width)


def _dot_exact_lhs(a_bf16, b):
    b1, b2, b3 = _split3(b)
    return _mm(a_bf16, b1) + (_mm(a_bf16, b2) + _mm(a_bf16, b3))


def _sigmoid(x):
    return 1.0 / (1.0 + jnp.exp(-x))


def _iota(shape, dim):
    return lax.broadcasted_iota(jnp.int32, shape, dim)


def _ffn_kernel(x_ref, g_ref, wg_ref, wu_ref, wd_ref, o_ref, h_ref):
    j = pl.program_id(1)

    @pl.when(j == 0)
    def _():
        x = x_ref[...]
        ms = jnp.mean(x * x, axis=-1, keepdims=True)
        h_ref[...] = (x * lax.rsqrt(ms + NORM_EPS) * g_ref[...]).astype(BF16)
        o_ref[...] = jnp.zeros_like(o_ref)

    h = h_ref[...]
    gate = _mm(h, wg_ref[...])
    up = _mm(h, wu_ref[...])
    act = (gate * _sigmoid(gate) * up).astype(BF16)
    o_ref[...] += _mm(act, wd_ref[...])

    @pl.when(j == pl.num_programs(1) - 1)
    def _():
        o_ref[...] = x_ref[...] + 0.5 * o_ref[...]


def _ffn(x, g, wg, wu, wd, *, tm=1024, tf=512):
    t, d = x.shape
    f = wg.shape[1]
    assert t % tm == 0 and f % tf == 0
    return pl.pallas_call(
        _ffn_kernel,
        out_shape=jax.ShapeDtypeStruct((t, d), F32),
        grid=(t // tm, f // tf),
        in_specs=[
            pl.BlockSpec((tm, d), lambda i, j: (i, 0)),
            pl.BlockSpec((1, d), lambda i, j: (0, 0)),
            pl.BlockSpec((d, tf), lambda i, j: (0, j)),
            pl.BlockSpec((d, tf), lambda i, j: (0, j)),
            pl.BlockSpec((tf, d), lambda i, j: (j, 0)),
        ],
        out_specs=pl.BlockSpec((tm, d), lambda i, j: (i, 0)),
        scratch_shapes=[pltpu.VMEM((tm, d), BF16)],
        compiler_params=_cparams(("parallel", "arbitrary")),
        name="ffn",
    )(x, g, wg, wu, wd)


def _norm_proj_kernel(x_ref, g_ref, w_ref, o_ref):
    x = x_ref[...]
    ms = jnp.mean(x * x, axis=-1, keepdims=True)
    h = (x * lax.rsqrt(ms + NORM_EPS) * g_ref[...]).astype(BF16)
    o_ref[...] = _mm(h, w_ref[...]).astype(o_ref.dtype)


def _norm_proj(x, g, w, *, tm=256, name="norm_proj"):
    t, d = x.shape
    n = w.shape[1]
    return pl.pallas_call(
        _norm_proj_kernel,
        out_shape=jax.ShapeDtypeStruct((t, n), F32),
        grid=(t // tm,),
        in_specs=[
            pl.BlockSpec((tm, d), lambda i: (i, 0)),
            _const_spec((1, d)),
            _const_spec((d, n)),
        ],
        out_specs=pl.BlockSpec((tm, n), lambda i: (i, 0)),
        compiler_params=_cparams(("parallel",)),
        name=name,
    )(x, g, w)


def _rwkv_prep_kernel(x_ref, gm_ref, win_ref, mix_ref, wwa_ref, gup_ref, w0_ref, a0_ref, kk_ref, ka_ref,
                      rk_ref, seg_ref, tri_ref,
                      rt_ref, at_ref, kt_ref, bt_ref, kh_ref, bh_ref, v_ref, dc_ref, bonus_ref,
                      gate_ref, last_ref, *, seq_tiles):
    i = pl.program_id(0)
    ts = x_ref.shape[0]
    w = rt_ref.shape[1]
    x = x_ref[...]
    ms = jnp.mean(x * x, axis=-1, keepdims=True)
    p = _mm((x * lax.rsqrt(ms + NORM_EPS) * gm_ref[...]).astype(BF16), win_ref[...])

    @pl.when(i == 0)
    def _():
        last_ref[...] = jnp.zeros_like(last_ref)

    prev = last_ref[7:8, :]
    prev = jnp.where(i % seq_tiles == 0, jnp.zeros_like(prev), prev)
    last_ref[...] = p[ts - 8:ts, :]
    shifted = pltpu.roll(p, 1, axis=0)
    shifted = jnp.where(_iota(p.shape, 0) == 0, prev, shifted)
    xs = p + mix_ref[...] * (shifted - p)

    r = xs[:, 0:w]
    k = xs[:, w:2 * w]
    v = xs[:, 2 * w:3 * w]
    lo = 3 * w
    pwa = xs[:, lo:lo + LANE]
    pg = xs[:, lo + LANE:lo + 3 * LANE]
    lane = _iota(pwa.shape, 1)
    z = jnp.where(lane < DECAY_LORA, jnp.tanh(pwa), pwa)
    wa = _dot3(z, wwa_ref[...])
    wl = w0_ref[...] + wa[:, :w]
    neg = -wl
    softplus = jnp.maximum(neg, 0.0) + jnp.log(1.0 + jnp.exp(-jnp.abs(neg)))
    lw = -jnp.exp(-softplus - 0.5)
    a = _sigmoid(a0_ref[...] + wa[:, w:])
    gate_ref[...] = _dot3(_sigmoid(pg), gup_ref[...])

    seg = seg_ref[...]
    kk = k * kk_ref[...]
    ss = _dot_exact_rhs(kk * kk, seg)
    kk = kk * lax.rsqrt(jnp.maximum(ss, 1e-24))
    k2 = k * (1.0 + (a - 1.0) * ka_ref[...])
    bonus_ref[...] = _dot_exact_rhs(r * k2 * rk_ref[...], seg) * v

    gc = _dot_exact_lhs(tri_ref[...], lw)
    nc = ts // CHUNK
    ends = [gc[(q + 1) * CHUNK - 1:(q + 1) * CHUNK, :] for q in range(nc)]
    gend = jnp.concatenate([jnp.broadcast_to(e, (CHUNK, w)) for e in ends], axis=0)
    to_end = jnp.exp(gend - gc)
    e_in = jnp.exp(gc)
    e_out = jnp.exp(-gc)
    b = kk * a
    rt_ref[...] = r * e_in
    at_ref[...] = -kk * jnp.exp(gc - lw)
    kt_ref[...] = k2 * e_out
    bt_ref[...] = b * e_out
    kh_ref[...] = k2 * to_end
    bh_ref[...] = b * to_end
    v_ref[...] = v
    dc_ref[0] = jnp.concatenate([jnp.exp(e) for e in ends] + [jnp.zeros((8 - nc, w), F32)], axis=0)


def _rwkv_prep(x, g_mix, w_in, mix, wwa, gup, w0, a0, k_k, k_a, r_k, seg, tri, *, seq, ts=256):
    t, d = x.shape
    pc = w_in.shape[1]
    w = w0.shape[1]
    nt = t // ts
    row = lambda i: (i, 0)
    tok = pl.BlockSpec((ts, w), row)
    tok_shape = jax.ShapeDtypeStruct((t, w), F32)
    return pl.pallas_call(
        functools.partial(_rwkv_prep_kernel, seq_tiles=seq // ts),
        out_shape=[tok_shape] * 7 + [jax.ShapeDtypeStruct((nt, 8, w), F32), tok_shape, tok_shape],
        grid=(nt,),
        in_specs=[
            pl.BlockSpec((ts, d), row),
            _const_spec((1, d)),
            _const_spec(w_in.shape),
            _const_spec((1, pc)),
            _const_spec(wwa.shape),
            _const_spec(gup.shape),
            _const_spec((1, w)), _const_spec((1, w)), _const_spec((1, w)), _const_spec((1, w)),
            _const_spec((1, w)),
            _const_spec(seg.shape),
            _const_spec(tri.shape),
        ],
        out_specs=[tok] * 7 + [pl.BlockSpec((1, 8, w), lambda i: (i, 0, 0)), tok, tok],
        scratch_shapes=[pltpu.VMEM((8, pc), F32)],
        compiler_params=_cparams(("arbitrary",)),
        name="rwkv_prep",
    )(x, g_mix, w_in, mix, wwa, gup, w0, a0, k_k, k_a, r_k, seg, tri)


PASSES_SCAN = 1
SCAN_CHUNKS_PER_STEP = 4
INTRA_CHUNKS_PER_STEP = 4
QUAD = 4


def _operand(x, passes):
    return _split2(x) if passes == 3 else (x.astype(BF16),)


def _prod(a, b, mm=_mm):
    if len(a) == 2 and len(b) == 2:
        return mm(a[0], b[0]) + (mm(a[0], b[1]) + mm(a[1], b[0]))
    return mm(a[0], b[0])


def _block_diag(y, n):
    c = y.shape[0]
    tiled = jnp.concatenate([y] * (y.shape[1] // n), axis=0)
    keep = (_iota(tiled.shape, 0) // c) == (_iota(tiled.shape, 1) // n)
    return jnp.where(keep, tiled, jnp.zeros_like(tiled))


def _quad_mm(x, y, n, mm=_mm):
    return mm(x.astype(BF16), _block_diag(y.astype(BF16), n))


def _unit_lower_inverse(a_list, row, col):
    n = CHUNK
    eye = (row == col).astype(F32)
    same8 = (row // 8) == (col // 8)
    a8 = [jnp.where(same8, a, 0.0) for a in a_list]
    d8 = [_block_diag(x.astype(BF16), n) for x in a8]
    a8_2 = [_mm(x.astype(BF16), d) for x, d in zip(a8, d8)]
    d8_2 = [_block_diag(x.astype(BF16), n) for x in a8_2]
    a8_4 = [_mm(x.astype(BF16), d) for x, d in zip(a8_2, d8_2)]
    p = [eye + x + x2 + _mm(x.astype(BF16), d2) for x, x2, d2 in zip(a8, a8_2, d8_2)]
    t = [pp + _quad_mm(pp, x4, n) for pp, x4 in zip(p, a8_4)]
    m = 16
    while m <= CHUNK:
        sel = ((row // m) == (col // m)) & ((row // (m // 2)) != (col // (m // 2)))
        mid = [_quad_mm(x, jnp.where(sel, a, 0.0), n) for x, a in zip(t, a_list)]
        t = [x + _quad_mm(md, x, n) for x, md in zip(t, mid)]
        m *= 2
    return t


def _rwkv_intra_kernel(rt_ref, at_ref, kt_ref, bt_ref, kh_ref, bh_ref, v_ref, dc_ref,
                       rr_ref, o0_ref, gh_ref, *, heads, chunks_per_tile):
    n = RWKV_HEAD_DIM
    qw = QUAD * n
    cps = gh_ref.shape[0]
    i = pl.program_id(0)
    row = _iota((CHUNK, qw), 0)
    col = _iota((CHUNK, qw), 1) % n
    strict = col < row
    incl = col <= row
    dc_all = dc_ref[0]
    dc_rows = []
    for c in range(cps):
        r = dc_all[c:c + 1, :]
        for q in range(1, chunks_per_tile // cps):
            r = jnp.where(i % (chunks_per_tile // cps) == q, dc_all[q * cps + c:q * cps + c + 1, :], r)
        dc_rows.append(r)

    units = [(slice(c * CHUNK, (c + 1) * CHUNK), slice(j * qw, (j + 1) * qw))
             for c in range(cps) for j in range(heads // QUAD)]
    at = [at_ref[r, s] for r, s in units]
    rt = [rt_ref[r, s] for r, s in units]
    v = [v_ref[r, s] for r, s in units]
    bd_b = [_block_diag(bt_ref[r, s].astype(BF16), n) for r, s in units]
    bd_k = [_block_diag(kt_ref[r, s].astype(BF16), n) for r, s in units]
    ar = [jnp.concatenate([a, r], axis=0).astype(BF16) for a, r in zip(at, rt)]
    mb = [_mm_nt(x, d) for x, d in zip(ar, bd_b)]
    mk = [_mm_nt(x, d) for x, d in zip(ar, bd_k)]
    a_ab = [jnp.where(strict, m[:CHUNK], 0.0) for m in mb]
    a_rb = [jnp.where(incl, m[CHUNK:], 0.0) for m in mb]
    akrk = [jnp.concatenate([jnp.where(strict, m[:CHUNK], 0.0), jnp.where(incl, m[CHUNK:], 0.0)], axis=0)
            for m in mk]
    avv = [_quad_mm(x, y, n) for x, y in zip(akrk, v)]
    tinv = _unit_lower_inverse(a_ab, row, col)
    a_new = [_quad_mm(t, a, n) for t, a in zip(tinv, at)]
    u0 = [_quad_mm(t, w[:CHUNK], n) for t, w in zip(tinv, avv)]
    for u, (r, s) in enumerate(units):
        rr_ref[r, s] = rt[u] + _quad_mm(a_rb[u], a_new[u], n)
        o0_ref[r, s] = _quad_mm(a_rb[u], u0[u], n) + avv[u][CHUNK:]
    eye = _iota((n, n), 0) == _iota((n, n), 1)
    nq = heads // QUAD
    gz, kv = [], []
    for c in range(cps):
        r = slice(c * CHUNK, (c + 1) * CHUNK)
        for h in range(heads):
            u, s = c * nq + h // QUAD, slice((h % QUAD) * n, (h % QUAD + 1) * n)
            hs = slice(h * n, (h + 1) * n)
            z = jnp.concatenate([a_new[u][:, s], u0[u][:, s]], axis=1).astype(BF16)
            gz.append(_mm_tn(bh_ref[r, hs].astype(BF16), z))
            kv.append(_mm_tn(kh_ref[r, hs].astype(BF16), v[u][:, s].astype(BF16)))
    for c in range(cps):
        for h in range(heads):
            hs = slice(h * n, (h + 1) * n)
            dmat = jnp.where(eye, jnp.broadcast_to(dc_rows[c][:, hs], (n, n)), 0.0)
            gh_ref[c, h] = gz[c * heads + h] + jnp.concatenate([dmat, kv[c * heads + h]], axis=1)


def _rwkv_scan_kernel(rr_ref, o0_ref, gh_ref, o_ref, state_ref, *, heads):
    n = RWKV_HEAD_DIM

    @pl.when(pl.program_id(1) == 0)
    def _():
        state_ref[...] = jnp.zeros_like(state_ref)

    sls = [slice(h * n, (h + 1) * n) for h in range(heads)]
    state = [state_ref[h] for h in range(heads)]
    for c in range(gh_ref.shape[0]):
        rows = slice(c * CHUNK, (c + 1) * CHUNK)
        h0 = [_operand(x, PASSES_SCAN) for x in state]
        outs = [_prod(_operand(rr_ref[rows, s], PASSES_SCAN), x) + o0_ref[rows, s] for s, x in zip(sls, h0)]
        state = [_prod(_operand(gh_ref[c, h, :, 0:n], PASSES_SCAN), h0[h]) + gh_ref[c, h, :, n:2 * n]
                 for h in range(heads)]
        for h, s in enumerate(sls):
            o_ref[rows, s] = outs[h]
    for h in range(heads):
        state_ref[h] = state[h]


def _rwkv_chunk(rt, at, kt, bt, kh, bh, v, dc, *, batch, seq, prep_ts):
    t, w = rt.shape
    n = RWKV_HEAD_DIM
    heads = w // n
    nchunk = seq // CHUNK
    cpt = prep_ts // CHUNK
    cps = INTRA_CHUNKS_PER_STEP
    assert heads % QUAD == 0 and cpt % cps == 0 and nchunk % SCAN_CHUNKS_PER_STEP == 0
    tok = pl.BlockSpec((cps * CHUNK, w), lambda i: (i, 0))
    tok_shape = jax.ShapeDtypeStruct((t, w), F32)
    rr, o0, gh = pl.pallas_call(
        functools.partial(_rwkv_intra_kernel, heads=heads, chunks_per_tile=cpt),
        out_shape=[tok_shape, tok_shape, jax.ShapeDtypeStruct((t // CHUNK, heads, n, 2 * n), F32)],
        grid=(t // (cps * CHUNK),),
        in_specs=[tok] * 7 + [pl.BlockSpec((1, 8, w), lambda i: (i * cps // cpt, 0, 0))],
        out_specs=[tok, tok, pl.BlockSpec((cps, heads, n, 2 * n), lambda i: (i, 0, 0, 0))],
        compiler_params=_cparams(("parallel",)),
        name="rwkv_intra",
    )(rt, at, kt, bt, kh, bh, v, dc)
    steps = nchunk // SCAN_CHUNKS_PER_STEP
    tok2 = pl.BlockSpec((SCAN_CHUNKS_PER_STEP * CHUNK, w), lambda b, c: (b * steps + c, 0))
    return pl.pallas_call(
        functools.partial(_rwkv_scan_kernel, heads=heads),
        out_shape=tok_shape,
        grid=(batch, steps),
        in_specs=[tok2, tok2, pl.BlockSpec((SCAN_CHUNKS_PER_STEP, heads, n, 2 * n),
                                           lambda b, c: (b * steps + c, 0, 0, 0))],
        out_specs=tok2,
        scratch_shapes=[pltpu.VMEM((heads, n, n), F32)],
        compiler_params=_cparams(("parallel", "arbitrary")),
        name="rwkv_scan",
    )(rr, o0, gh)


def _rope_lanes(x, cos_t, sin_a, sin_b):
    width = x.shape[1]
    up = pltpu.roll(x, width - ROPE_HALF, axis=1)
    dn = pltpu.roll(x, ROPE_HALF, axis=1)
    return x * cos_t + up * sin_a + dn * sin_b


def _tile_lanes(tab, width):
    return jnp.concatenate([tab] * (width // tab.shape[1]), axis=1)


def _nsa_prep_kernel(x_ref, gm_ref, win_ref, cos_ref, sa_ref, sb_ref, qn_ref, kn_ref, seg_ref,
                     q_ref, ks_ref, vs_ref, kw_ref, vw_ref, g_ref, gk_ref, gv_ref, *, seq_tiles):
    dh = NSA_HEAD_DIM
    x = x_ref[...]
    ms = jnp.mean(x * x, axis=-1, keepdims=True)
    p = _mm((x * lax.rsqrt(ms + NORM_EPS) * gm_ref[...]).astype(BF16), win_ref[...])
    qw = q_ref.shape[1]
    kvw = NSA_KV_HEADS * dh
    seg = seg_ref[...]
    cos_t, sin_a, sin_b = cos_ref[...], sa_ref[...], sb_ref[...]

    def norm_rope(x, gain):
        wd = x.shape[1]
        ms = _head_mean(x * x, seg, dh)
        y = x * lax.rsqrt(ms + NORM_EPS) * gain
        return _rope_lanes(y, _tile_lanes(cos_t, wd), _tile_lanes(sin_a, wd), _tile_lanes(sin_b, wd))

    q = norm_rope(p[:, 0:qw], qn_ref[...])
    q_ref[...] = q * (dh ** -0.5)
    base = qw + 2 * kvw
    ks = norm_rope(p[:, base:base + kvw], kn_ref[1:2, :])
    vs = p[:, base + kvw:base + 2 * kvw].astype(BF16)
    kw = norm_rope(p[:, base + 2 * kvw:base + 3 * kvw], kn_ref[2:3, :]).astype(BF16)
    vw = p[:, base + 3 * kvw:base + 4 * kvw].astype(BF16)
    ts = x_ref.shape[0]
    tpos = (pl.program_id(0) % seq_tiles) * ts + _iota((ts, dh), 0)
    onehot = jnp.where(tpos // SEL_BLOCK == _iota((ts, dh), 1), 1.0, 0.0)
    for h in range(NSA_KV_HEADS):
        sl = slice(h * dh, (h + 1) * dh)
        ks_ref[0, h] = jnp.concatenate([ks[:, sl], onehot], axis=1).astype(BF16)
        vs_ref[0, h] = vs[:, sl]
        kw_ref[0, h] = kw[:, sl]
        vw_ref[0, h] = vw[:, sl]
    sig = _sigmoid(p[:, base + 4 * kvw:base + 4 * kvw + LANE])
    sig_t = sig.T
    per_head = 3 * NSA_GROUP
    for h in range(NSA_KV_HEADS):
        g_ref[h] = sig_t[per_head * h:per_head * h + GATE_ROWS, :]
    kc3 = p[:, qw:qw + kvw].reshape(ts // CMP_STRIDE, CMP_STRIDE, kvw)
    vc3 = p[:, qw + kvw:qw + 2 * kvw].reshape(ts // CMP_STRIDE, CMP_STRIDE, kvw)
    for i in range(CMP_STRIDE):
        kci, vci = kc3[:, i, :], vc3[:, i, :]
        for h in range(NSA_KV_HEADS):
            gk_ref[0, h, :, i * dh:(i + 1) * dh] = kci[:, h * dh:(h + 1) * dh]
            gv_ref[0, h, :, i * dh:(i + 1) * dh] = vci[:, h * dh:(h + 1) * dh]


def _nsa_prep(x, g_mix, w_in, cos_t, sin_a, sin_b, qn, kn, seg, *, batch, seq, ts=256):
    t, d = x.shape
    qw = NSA_HEADS * NSA_HEAD_DIM
    st = seq // ts
    tab = pl.BlockSpec((ts, LANE), lambda i: (i % st, 0))
    hm = pl.BlockSpec((1, NSA_KV_HEADS, ts, NSA_HEAD_DIM), lambda i: (i // st, 0, i % st, 0))
    hm_shape = jax.ShapeDtypeStruct((batch, NSA_KV_HEADS, seq, NSA_HEAD_DIM), BF16)
    assert seq // SEL_BLOCK <= NSA_HEAD_DIM
    aug = pl.BlockSpec((1, NSA_KV_HEADS, ts, 2 * NSA_HEAD_DIM), lambda i: (i // st, 0, i % st, 0))
    aug_shape = jax.ShapeDtypeStruct((batch, NSA_KV_HEADS, seq, 2 * NSA_HEAD_DIM), BF16)
    grp_w = CMP_STRIDE * NSA_HEAD_DIM
    grp = pl.BlockSpec((1, NSA_KV_HEADS, ts // CMP_STRIDE, grp_w), lambda i: (i // st, 0, i % st, 0))
    grp_shape = jax.ShapeDtypeStruct((batch, NSA_KV_HEADS, seq // CMP_STRIDE, grp_w), F32)
    return pl.pallas_call(
        functools.partial(_nsa_prep_kernel, seq_tiles=st),
        out_shape=[jax.ShapeDtypeStruct((t, qw), F32), aug_shape] + [hm_shape] * 3
        + [jax.ShapeDtypeStruct((NSA_KV_HEADS, GATE_ROWS, t), F32)] + [grp_shape] * 2,
        grid=(t // ts,),
        in_specs=[pl.BlockSpec((ts, d), lambda i: (i, 0)), _const_spec((1, d)), _const_spec(w_in.shape),
                  tab, tab, tab,
                  _const_spec(qn.shape), _const_spec(kn.shape), _const_spec(seg.shape)],
        out_specs=[pl.BlockSpec((ts, qw), lambda i: (i, 0)), aug] + [hm] * 3
        + [pl.BlockSpec((NSA_KV_HEADS, GATE_ROWS, ts), lambda i: (0, 0, i))] + [grp] * 2,
        compiler_params=_cparams(("parallel",)),
        name="nsa_prep",
    )(x, g_mix, w_in, cos_t, sin_a, sin_b, qn, kn, seg)


def _gelu_tanh(x):
    return 0.5 * x * (1.0 + jnp.tanh(np.sqrt(2.0 / np.pi).astype(np.float32) * (x + 0.044715 * (x * x * x))))


def _compress_kernel(gk_ref, gv_ref, pk_ref, pv_ref, k1_ref, k2_ref, v1_ref, v2_ref, kn_ref,
                     cos_ref, sin_ref, rot_ref, kc_ref, vc_ref):
    half = k1_ref.shape[0] // 2

    def mlp(g, pos, w1_ref, w2_ref):
        ya = _dot3(g, w1_ref[0:half, :])
        yb = _dot3(g, w1_ref[half:, :])
        bias = _dot3(jnp.broadcast_to(pos, (8, pos.shape[1])), w1_ref[...])[0:1, :]
        n = g.shape[0]
        hid = ya + pltpu.roll(yb, n - 1, axis=0) + bias
        return _dot3(_gelu_tanh(hid), w2_ref[...])

    hk, ng, gw = gk_ref.shape[1:]
    tile_rows = lambda tab: jnp.concatenate([tab] * hk, axis=0)
    kc = mlp(gk_ref[0].reshape(hk * ng, gw), pk_ref[...], k1_ref, k2_ref)
    ms = jnp.mean(kc * kc, axis=-1, keepdims=True)
    kc = kc * lax.rsqrt(ms + NORM_EPS) * kn_ref[0:1, :]
    kc = kc * tile_rows(cos_ref[...]) + _dot_exact_rhs(kc, rot_ref[...]) * tile_rows(sin_ref[...])
    kc_ref[0] = kc.reshape(hk, ng, kc.shape[1])
    vc = mlp(gv_ref[0].reshape(hk * ng, gw), pv_ref[...], v1_ref, v2_ref)
    vc_ref[0] = vc.reshape(hk, ng, vc.shape[1])


def _nsa_compress(gk, gv, pk, pv, k1, k2, v1, v2, kn, cos_c, sin_c, rot):
    b, hk, ng, gw = gk.shape
    dh = NSA_HEAD_DIM
    grp = pl.BlockSpec((1, hk, ng, gw), lambda i: (i, 0, 0, 0))
    out = pl.BlockSpec((1, hk, ng, dh), lambda i: (i, 0, 0, 0))
    shape = jax.ShapeDtypeStruct((b, hk, ng, dh), F32)
    consts = [pk, pv, k1, k2, v1, v2, kn, cos_c, sin_c, rot]
    return pl.pallas_call(
        _compress_kernel,
        out_shape=[shape, shape],
        grid=(b,),
        in_specs=[grp, grp] + [_const_spec(c.shape) for c in consts],
        out_specs=[out, out],
        compiler_params=_cparams(("parallel",)),
        name="nsa_compress",
    )(gk, gv, *consts)


def _nsa_attn_kernel(q_ref, kc_ref, vc_ref, ks_ref, vs_ref, kw_ref, vw_ref, g_ref, ovt_ref, o_ref):
    dh = NSA_HEAD_DIM
    grp = NSA_GROUP
    nh = kc_ref.shape[1]
    qi = pl.program_id(2)
    tq = q_ref.shape[0]
    hc = grp * tq
    cols = nh * hc
    t0 = qi * tq
    ncmp = kc_ref.shape[2]
    nsel = ovt_ref.shape[0]
    kb = ATTN_KEY_BLOCK
    span = WINDOW + tq
    heads = range(nh)

    def lanes(xs):
        return jnp.concatenate(xs, axis=1)

    def every_head(x):
        return lanes([x] * (nh * grp))

    q = q_ref[...]
    q4 = [jnp.concatenate([q[:, (h * grp + g) * dh:(h * grp + g + 1) * dh] for g in range(grp)], axis=0)
          for h in heads]
    q4b = [(x * LOG2_E).astype(BF16) for x in q4]

    st = lanes([_dot3(kc_ref[0, h], q4[h], _mm_nt) for h in heads])
    tl = t0 + _iota((ncmp, cols), 1) % tq
    cmask = _iota((ncmp, cols), 0) * CMP_STRIDE + (CMP_BLOCK - 1) <= tl
    sm = jnp.where(cmask, st, MASKED)
    e = jnp.where(cmask, jnp.exp(sm - jnp.max(sm, axis=0, keepdims=True)), 0.0)
    pt = e / jnp.maximum(jnp.sum(e, axis=0, keepdims=True), 1e-30)
    ptb = pt.astype(BF16)
    o_cmp = lanes([_mm_tn(vc_ref[0, h].astype(BF16), ptb[:, h * hc:(h + 1) * hc]) for h in heads])
    psum = []
    for h in heads:
        acc = pt[:, h * hc:h * hc + tq]
        for g in range(1, grp):
            acc = acc + pt[:, h * hc + g * tq:h * hc + (g + 1) * tq]
        psum.append(acc)
    psum = lanes(psum)

    imp = _dot_exact_lhs(ovt_ref[...], psum)
    blk = _iota(imp.shape, 0)
    cur = (t0 + _iota(imp.shape, 1) % tq) // SEL_BLOCK
    forced = (blk == 0) | (blk == cur) | (blk == cur - 1)
    imp = jnp.where(forced, jnp.inf, jnp.where(blk > cur, -jnp.inf, imp))
    rank = jnp.zeros(imp.shape, jnp.int32)
    for m in range(nsel):
        im = imp[m:m + 1, :]
        ahead = (im > imp) | ((im == imp) & (m < blk))
        rank = rank + ahead.astype(jnp.int32)
    sel_bias = jnp.where(rank < min(SEL_TOP, nsel), 0.0, MASKED).astype(BF16)

    eye = jnp.where(_iota((nsel, dh), 0) == _iota((nsel, dh), 1), 1.0, 0.0).astype(BF16)
    bias_q = _mm_tn(sel_bias, eye)
    q_aug = [jnp.concatenate([q4[h] * LOG2_E, jnp.concatenate([bias_q[h * tq:(h + 1) * tq]] * grp, axis=0)],
                             axis=1).astype(BF16) for h in heads]

    def attend(k_ref, v_ref, qs, k0, state, bias=None):
        m_run, l_run, acc = state
        rows = pl.ds(pl.multiple_of(k0, kb), kb)
        s = lanes([_mm_nt(k_ref[0, h, rows, :], qs[h]) for h in heads])
        if bias is not None:
            s = s + bias
        m_new = jnp.maximum(m_run, jnp.max(s, axis=0, keepdims=True))
        alpha = jnp.exp2(m_run - m_new)
        pb = jnp.exp2(s - m_new)
        l_new = alpha * l_run + jnp.sum(pb, axis=0, keepdims=True)
        pb = pb.astype(BF16)
        pv = lanes([_mm_tn(v_ref[0, h, rows, :], pb[:, h * hc:(h + 1) * hc]) for h in heads])
        return m_new, l_new, alpha * acc + pv

    init = (jnp.full((1, cols), MASKED, F32), jnp.zeros((1, cols), F32), jnp.zeros((dh, cols), F32))
    state = lax.fori_loop(0, t0 // kb, lambda j, st: attend(ks_ref, vs_ref, q_aug, j * kb, st), init)
    for d in range(tq // kb):
        causal = jnp.where(d * kb + _iota((kb, tq), 0) <= _iota((kb, tq), 1), 0.0, MASKED)
        state = attend(ks_ref, vs_ref, q_aug, t0 + d * kb, state, every_head(causal))
    o_slc = state[2] / state[1]

    w0 = pl.multiple_of(jnp.maximum(t0 - WINDOW, 0), tq)
    wrows = pl.ds(w0, span)
    kpos = w0 + _iota((span, tq), 0)
    tw = t0 + _iota((span, tq), 1)
    wbias = jnp.where((kpos <= tw) & (kpos > tw - WINDOW), 0.0, MASKED)
    s = lanes([_mm_nt(kw_ref[0, h, wrows, :], q4b[h]) for h in heads]) + every_head(wbias)
    p = jnp.exp2(s - jnp.max(s, axis=0, keepdims=True))
    pb = p.astype(BF16)
    o_win = (lanes([_mm_tn(vw_ref[0, h, wrows, :], pb[:, h * hc:(h + 1) * hc]) for h in heads])
             / jnp.sum(p, axis=0, keepdims=True))

    gates = g_ref[...]
    grow = [lanes([gates[h, 3 * g + br:3 * g + br + 1, :] for h in heads for g in range(grp)])
            for br in range(3)]
    o4 = grow[0] * o_cmp + grow[1] * o_slc + grow[2] * o_win
    o_ref[...] = jnp.concatenate([o4[:, c * tq:(c + 1) * tq] for c in range(nh * grp)], axis=0)


def _nsa_attn(q, kc, vc, ks_aug, vs, kw, vw, gates_t, overlap_t, *, batch, seq, tq=256,
              nh=ATTN_KV_PER_STEP):
    t, qw = q.shape
    dh = NSA_HEAD_DIM
    gw = nh * NSA_GROUP * dh
    st = seq // tq
    assert tq % ATTN_KEY_BLOCK == 0 and WINDOW % tq == 0 and seq % tq == 0
    ncmp = kc.shape[2]
    cmp_spec = pl.BlockSpec((1, nh, ncmp, dh), lambda b, h, i: (b, h, 0, 0))
    kv_spec = pl.BlockSpec((1, nh, seq, dh), lambda b, h, i: (b, h, 0, 0))
    aug_spec = pl.BlockSpec((1, nh, seq, 2 * dh), lambda b, h, i: (b, h, 0, 0))
    return pl.pallas_call(
        _nsa_attn_kernel,
        out_shape=jax.ShapeDtypeStruct((qw, t), F32),
        grid=(batch, NSA_KV_HEADS // nh, st),
        in_specs=[
            pl.BlockSpec((tq, gw), lambda b, h, i: (b * st + i, h)),
            cmp_spec, cmp_spec, aug_spec, kv_spec, kv_spec, kv_spec,
            pl.BlockSpec((nh, gates_t.shape[1], tq), lambda b, h, i: (h, 0, b * st + i)),
            _const_spec(overlap_t.shape),
        ],
        out_specs=pl.BlockSpec((gw, tq), lambda b, h, i: (h, b * st + i)),
        compiler_params=_cparams(("parallel", "parallel", "arbitrary")),
        name="nsa_attn",
    )(q, kc, vc, ks_aug, vs, kw, vw, gates_t, overlap_t)


def _merge_kernel(x_ref, o_ref, bonus_ref, gate_ref, ybt_ref, pg_ref, gnw_ref, gnb_ref, seg_ref,
                  ua_ref, ub_ref, wo_ref, out_ref):
    d = x_ref.shape[1]
    n = RWKV_HEAD_DIM
    seg = seg_ref[...]
    o = o_ref[...]
    mu = _dot_exact_rhs(o, seg) * (1.0 / n)
    dlt = o - mu
    var = _head_mean(dlt * dlt, seg, n)
    on = dlt * lax.rsqrt(var + GN_EPS) * gnw_ref[...] + gnb_ref[...]
    ya = ((on + bonus_ref[...]) * gate_ref[...]).astype(BF16)
    yb_t = ybt_ref[...].astype(BF16)
    merged = (_sigmoid(pg_ref[:, 0:d]) * _mm(ya, ua_ref[...])
              + _sigmoid(pg_ref[:, d:2 * d]) * _mm_tn(yb_t, ub_ref[...]))
    out_ref[...] = x_ref[...] + _mm(merged.astype(BF16), wo_ref[...])


def _merge(x, o_rwkv, bonus, gate, yb_t, pg, gnw, gnb, seg, ua, ub, wo, *, tm=256):
    t, d = x.shape
    w = o_rwkv.shape[1]
    row = lambda i: (i, 0)
    tokw = pl.BlockSpec((tm, w), row)
    return pl.pallas_call(
        _merge_kernel,
        out_shape=jax.ShapeDtypeStruct((t, d), F32),
        grid=(t // tm,),
        in_specs=[pl.BlockSpec((tm, d), row), tokw, tokw, tokw,
                  pl.BlockSpec((yb_t.shape[0], tm), lambda i: (0, i)),
                  pl.BlockSpec((tm, 2 * d), row),
                  _const_spec((1, w)), _const_spec((1, w)), _const_spec(seg.shape),
                  _const_spec(ua.shape), _const_spec(ub.shape), _const_spec(wo.shape)],
        out_specs=pl.BlockSpec((tm, d), row),
        compiler_params=_cparams(("parallel",)),
        name="merge",
    )(x, o_rwkv, bonus, gate, yb_t, pg, gnw, gnb, seg, ua, ub, wo)


def _block_diag_ones(width, block):
    idx = np.arange(width) // block
    return jnp.asarray(idx[:, None] == idx[None, :], BF16)


def _chunk_lower_ones(ts):
    i = np.arange(ts)
    return jnp.asarray((i[:, None] // CHUNK == i[None, :] // CHUNK) & (i[None, :] <= i[:, None]), BF16)


def _rope_tables(pos):
    inv = ROPE_THETA ** (-jnp.arange(ROPE_HALF, dtype=F32) / ROPE_HALF)
    ang = jnp.asarray(pos).astype(F32)[:, None] * inv[None, :]
    cos, sin = jnp.cos(ang), jnp.sin(ang)
    n = ang.shape[0]
    pad = jnp.zeros((n, NSA_HEAD_DIM - ROPE_DIM), F32)
    zero = jnp.zeros_like(sin)
    cos_h = jnp.concatenate([cos, cos, pad + 1.0], axis=1)
    sa_h = jnp.concatenate([-sin, zero, pad], axis=1)
    sb_h = jnp.concatenate([zero, sin, pad], axis=1)
    return cos_h, sa_h, sb_h


def _rot_half_matrix():
    r = np.zeros((NSA_HEAD_DIM, NSA_HEAD_DIM), np.float32)
    for l in range(ROPE_HALF):
        r[l + ROPE_HALF, l] = -1.0
        r[l, l + ROPE_HALF] = 1.0
    return jnp.asarray(r, BF16)


def _overlap_matrix_t(ncmp_pad, nsel):
    cs = np.arange(ncmp_pad)[None, :] * CMP_STRIDE
    ss = np.arange(nsel)[:, None] * SEL_BLOCK
    ov = np.clip(np.minimum(cs + CMP_BLOCK, ss + SEL_BLOCK) - np.maximum(cs, ss), 0, None) / CMP_BLOCK
    return jnp.asarray(ov, BF16)


def _pad_cols(x, width):
    return jnp.pad(x, ((0, 0), (0, width - x.shape[1])))


def _layer(x, l, ffn1_norm, ffn1_w_gate, ffn1_w_up, ffn1_w_down, mix_norm, w_in,
           rwkv_mix, rwkv_w0, rwkv_w_up, rwkv_a0, rwkv_a_up, rwkv_g_up,
           rwkv_k_k, rwkv_k_a, rwkv_r_k, rwkv_gn_w, rwkv_gn_b,
           nsa_q_norm, nsa_k_norm, cmp_pos_k, cmp_pos_v,
           cmp_k_w1, cmp_k_w2, cmp_v_w1, cmp_v_w2,
           w_branch_rwkv, w_branch_nsa, w_out,
           ffn2_norm, ffn2_w_gate, ffn2_w_up, ffn2_w_down, *, batch, seq):
    t, d = x.shape
    w = rwkv_w0.shape[1]
    dh = NSA_HEAD_DIM
    qw = NSA_HEADS * dh
    kvw = NSA_KV_HEADS * dh
    prep_ts = 256
    row = lambda v: v.reshape(1, -1)

    x = _ffn(x, row(ffn1_norm[l]), ffn1_w_gate[l].astype(BF16), ffn1_w_up[l].astype(BF16),
             ffn1_w_down[l].astype(BF16))

    wi = w_in[l]
    rwkv_cols = 3 * w + DECAY_LORA + ICLR_LORA + GATE_LORA
    rwkv_pad = 3 * w + 3 * LANE
    nsa_cols = qw + 6 * kvw + 3 * NSA_HEADS
    nsa_pad = qw + 6 * kvw + LANE
    g_mix = row(mix_norm[l])
    w_rwkv = _pad_cols(wi[:, :rwkv_cols], rwkv_pad).astype(BF16)
    w_nsa = _pad_cols(wi[:, rwkv_cols:rwkv_cols + nsa_cols], nsa_pad).astype(BF16)
    p_gate = _norm_proj(x, g_mix, wi[:, rwkv_cols + nsa_cols:].astype(BF16), name="proj_gate")

    wwa = jnp.zeros((LANE, 2 * w), F32)
    wwa = wwa.at[:DECAY_LORA, :w].set(rwkv_w_up[l]).at[DECAY_LORA:, w:].set(rwkv_a_up[l])
    gup = jnp.pad(rwkv_g_up[l], ((0, 2 * LANE - GATE_LORA), (0, 0)))
    seg_w = _block_diag_ones(SEG_WIDTH, RWKV_HEAD_DIM)
    (rt, at, kt, bt, kh, bh, v, dc, bonus, gate) = _rwkv_prep(
        x, g_mix, w_rwkv, _pad_cols(row(rwkv_mix[l]), rwkv_pad), wwa, gup, row(rwkv_w0[l]), row(rwkv_a0[l]),
        row(rwkv_k_k[l]), row(rwkv_k_a[l]), row(rwkv_r_k[l]), seg_w, _chunk_lower_ones(prep_ts),
        seq=seq, ts=prep_ts)
    o_rwkv = _rwkv_chunk(rt, at, kt, bt, kh, bh, v, dc, batch=batch, seq=seq, prep_ts=prep_ts)

    cos_t, sin_a, sin_b = _rope_tables(np.arange(seq))
    two = lambda tab: jnp.concatenate([tab, tab], axis=1)
    qn = jnp.tile(row(nsa_q_norm[l]), (1, NSA_HEADS))
    kn = jnp.tile(nsa_k_norm[l], (1, NSA_KV_HEADS))
    q, ks, vs, kw, vw, gates, grp_k, grp_v = _nsa_prep(
        x, g_mix, w_nsa, two(cos_t), two(sin_a), two(sin_b), qn, kn, _block_diag_ones(SEG_WIDTH, dh),
        batch=batch, seq=seq)

    ngrp = seq // CMP_STRIDE
    cend = np.arange(ngrp) * CMP_STRIDE + CMP_BLOCK - 1
    cos_c, sa_c, sb_c = _rope_tables(cend)
    kc, vc = _nsa_compress(
        grp_k, grp_v,
        cmp_pos_k[l].reshape(1, -1), cmp_pos_v[l].reshape(1, -1),
        cmp_k_w1[l], cmp_k_w2[l], cmp_v_w1[l], cmp_v_w2[l], nsa_k_norm[l],
        cos_c, sb_c - sa_c, _rot_half_matrix())
    nsel = seq // SEL_BLOCK
    y_nsa = _nsa_attn(q, kc, vc, ks, vs, kw, vw, gates, _overlap_matrix_t(ngrp, nsel),
                      batch=batch, seq=seq)

    x = _merge(x, o_rwkv, bonus, gate, y_nsa, p_gate, row(rwkv_gn_w[l]), row(rwkv_gn_b[l]), seg_w,
               w_branch_rwkv[l].astype(BF16), w_branch_nsa[l].astype(BF16), w_out[l].astype(BF16))
    return _ffn(x, row(ffn2_norm[l]), ffn2_w_gate[l].astype(BF16), ffn2_w_up[l].astype(BF16),
                ffn2_w_down[l].astype(BF16))


def kernel(x, ffn1_norm, ffn1_w_gate, ffn1_w_up, ffn1_w_down, mix_norm, w_in, rwkv_mix, rwkv_w0, rwkv_w_up, rwkv_a0, rwkv_a_up, rwkv_g_up, rwkv_k_k, rwkv_k_a, rwkv_r_k, rwkv_gn_w, rwkv_gn_b, nsa_q_norm, nsa_k_norm, cmp_pos_k, cmp_pos_v, cmp_k_w1, cmp_k_w2, cmp_v_w1, cmp_v_w2, w_branch_rwkv, w_branch_nsa, w_out, ffn2_norm, ffn2_w_gate, ffn2_w_up, ffn2_w_down):
    batch, seq, d = x.shape
    params = (ffn1_norm, ffn1_w_gate, ffn1_w_up, ffn1_w_down, mix_norm, w_in, rwkv_mix, rwkv_w0,
              rwkv_w_up, rwkv_a0, rwkv_a_up, rwkv_g_up, rwkv_k_k, rwkv_k_a, rwkv_r_k, rwkv_gn_w,
              rwkv_gn_b, nsa_q_norm, nsa_k_norm, cmp_pos_k, cmp_pos_v, cmp_k_w1, cmp_k_w2, cmp_v_w1,
              cmp_v_w2, w_branch_rwkv, w_branch_nsa, w_out, ffn2_norm, ffn2_w_gate, ffn2_w_up,
              ffn2_w_down)
    y = x.reshape(batch * seq, d)
    for l in range(ffn1_norm.shape[0]):
        y = _layer(y, l, *params, batch=batch, seq=seq)
    return y.reshape(batch, seq, d)
```

```python
import functools

import numpy as np
import jax
import jax.numpy as jnp
from jax import lax
from jax.experimental import pallas as pl
from jax.experimental.pallas import tpu as pltpu

F32 = jnp.float32
BF16 = jnp.bfloat16

RWKV_HEAD_DIM = 64
DECAY_LORA = 64
ICLR_LORA = 64
GATE_LORA = 160
GN_EPS = 64e-5
NSA_HEADS = 16
NSA_KV_HEADS = 4
NSA_GROUP = NSA_HEADS // NSA_KV_HEADS
NSA_HEAD_DIM = 64
ROPE_DIM = NSA_HEAD_DIM // 4
ROPE_HALF = ROPE_DIM // 2
ROPE_THETA = 500000.0
CMP_BLOCK = 32
CMP_STRIDE = 16
SEL_BLOCK = 64
SEL_TOP = 16
WINDOW = 512
NORM_EPS = 1e-6

LANE = 128
SEG_WIDTH = 256
CHUNK = 64
VMEM_LIMIT = 56 * 1024 * 1024
MASKED = -1e30
LOG2_E = 1.4426950408889634
GATE_ROWS = 16
ATTN_KEY_BLOCK = 256
ATTN_KV_PER_STEP = 4


def _cparams(sem):
    return pltpu.CompilerParams(dimension_semantics=sem, vmem_limit_bytes=VMEM_LIMIT)


def _const_spec(shape):
    nd = len(shape)
    return pl.BlockSpec(shape, lambda *_: (0,) * nd, pipeline_mode=pl.Buffered(1))


def _mm(a, b):
    return lax.dot_general(a, b, (((1,), (0,)), ((), ())), preferred_element_type=F32)


def _mm_nt(a, b):
    return lax.dot_general(a, b, (((1,), (1,)), ((), ())), preferred_element_type=F32)


def _mm_tn(a, b):
    return lax.dot_general(a, b, (((0,), (0,)), ((), ())), preferred_element_type=F32)


def _split2(x):
    hi = x.astype(BF16)
    lo = (x - hi.astype(F32)).astype(BF16)
    return hi, lo


def _split3(x):
    h1 = x.astype(BF16)
    r1 = x - h1.astype(F32)
    h2 = r1.astype(BF16)
    h3 = (r1 - h2.astype(F32)).astype(BF16)
    return h1, h2, h3


def _dot3(a, b, mm=_mm):
    a1, a2 = _split2(a)
    b1, b2 = _split2(b)
    return mm(a1, b1) + (mm(a1, b2) + mm(a2, b1))


def _mm_groups(a, b):
    k = b.shape[0]
    if a.shape[1] == k:
        return _mm(a, b)
    return jnp.concatenate([_mm(a[:, i * k:(i + 1) * k], b) for i in range(a.shape[1] // k)], axis=1)


def _dot_exact_rhs(a, b_bf16):
    a1, a2 = _split2(a)
    return _mm_groups(a1, b_bf16) + _mm_groups(a2, b_bf16)


def _head_mean(x, seg_bf16, width):
    return _mm_groups(x.astype(BF16), seg_bf16) * (1.0 / width)


def _dot_exact_lhs(a_bf16, b):
    b1, b2, b3 = _split3(b)
    return _mm(a_bf16, b1) + (_mm(a_bf16, b2) + _mm(a_bf16, b3))


def _sigmoid(x):
    return 1.0 / (1.0 + jnp.exp(-x))


def _iota(shape, dim):
    return lax.broadcasted_iota(jnp.int32, shape, dim)


def _ffn_kernel(x_ref, g_ref, wg_ref, wu_ref, wd_ref, o_ref, h_ref):
    j = pl.program_id(1)

    @pl.when(j == 0)
    def _():
        x = x_ref[...]
        ms = jnp.mean(x * x, axis=-1, keepdims=True)
        h_ref[...] = (x * lax.rsqrt(ms + NORM_EPS) * g_ref[...]).astype(BF16)
        o_ref[...] = jnp.zeros_like(o_ref)

    h = h_ref[...]
    gate = _mm(h, wg_ref[...])
    up = _mm(h, wu_ref[...])
    act = (gate * _sigmoid(gate) * up).astype(BF16)
    o_ref[...] += _mm(act, wd_ref[...])

    @pl.when(j == pl.num_programs(1) - 1)
    def _():
        o_ref[...] = x_ref[...] + 0.5 * o_ref[...]


def _ffn(x, g, wg, wu, wd, *, tm=1024, tf=512):
    t, d = x.shape
    f = wg.shape[1]
    assert t % tm == 0 and f % tf == 0
    return pl.pallas_call(
        _ffn_kernel,
        out_shape=jax.ShapeDtypeStruct((t, d), F32),
        grid=(t // tm, f // tf),
        in_specs=[
            pl.BlockSpec((tm, d), lambda i, j: (i, 0)),
            pl.BlockSpec((1, d), lambda i, j: (0, 0)),
            pl.BlockSpec((d, tf), lambda i, j: (0, j)),
            pl.BlockSpec((d, tf), lambda i, j: (0, j)),
            pl.BlockSpec((tf, d), lambda i, j: (j, 0)),
        ],
        out_specs=pl.BlockSpec((tm, d), lambda i, j: (i, 0)),
        scratch_shapes=[pltpu.VMEM((tm, d), BF16)],
        compiler_params=_cparams(("parallel", "arbitrary")),
        name="ffn",
    )(x, g, wg, wu, wd)


def _norm_proj_kernel(x_ref, g_ref, w_ref, o_ref):
    x = x_ref[...]
    ms = jnp.mean(x * x, axis=-1, keepdims=True)
    h = (x * lax.rsqrt(ms + NORM_EPS) * g_ref[...]).astype(BF16)
    o_ref[...] = _mm(h, w_ref[...]).astype(o_ref.dtype)


def _norm_proj(x, g, w, *, tm=256, name="norm_proj"):
    t, d = x.shape
    n = w.shape[1]
    return pl.pallas_call(
        _norm_proj_kernel,
        out_shape=jax.ShapeDtypeStruct((t, n), F32),
        grid=(t // tm,),
        in_specs=[
            pl.BlockSpec((tm, d), lambda i: (i, 0)),
            _const_spec((1, d)),
            _const_spec((d, n)),
        ],
        out_specs=pl.BlockSpec((tm, n), lambda i: (i, 0)),
        compiler_params=_cparams(("parallel",)),
        name=name,
    )(x, g, w)


def _rwkv_prep_kernel(x_ref, gm_ref, win_ref, mix_ref, wwa_ref, gup_ref, w0_ref, a0_ref, kk_ref, ka_ref,
                      rk_ref, seg_ref, tri_ref,
                      rt_ref, at_ref, kt_ref, bt_ref, kh_ref, bh_ref, v_ref, dc_ref, bonus_ref,
                      gate_ref, last_ref, *, seq_tiles):
    i = pl.program_id(0)
    ts = x_ref.shape[0]
    w = rt_ref.shape[1]
    x = x_ref[...]
    ms = jnp.mean(x * x, axis=-1, keepdims=True)
    p = _mm((x * lax.rsqrt(ms + NORM_EPS) * gm_ref[...]).astype(BF16), win_ref[...])

    @pl.when(i == 0)
    def _():
        last_ref[...] = jnp.zeros_like(last_ref)

    prev = last_ref[7:8, :]
    prev = jnp.where(i % seq_tiles == 0, jnp.zeros_like(prev), prev)
    last_ref[...] = p[ts - 8:ts, :]
    shifted = pltpu.roll(p, 1, axis=0)
    shifted = jnp.where(_iota(p.shape, 0) == 0, prev, shifted)
    xs = p + mix_ref[...] * (shifted - p)

    r = xs[:, 0:w]
    k = xs[:, w:2 * w]
    v = xs[:, 2 * w:3 * w]
    lo = 3 * w
    pwa = xs[:, lo:lo + LANE]
    pg = xs[:, lo + LANE:lo + 3 * LANE]
    lane = _iota(pwa.shape, 1)
    z = jnp.where(lane < DECAY_LORA, jnp.tanh(pwa), pwa)
    wa = _dot3(z, wwa_ref[...])
    wl = w0_ref[...] + wa[:, :w]
    neg = -wl
    softplus = jnp.maximum(neg, 0.0) + jnp.log(1.0 + jnp.exp(-jnp.abs(neg)))
    lw = -jnp.exp(-softplus - 0.5)
    a = _sigmoid(a0_ref[...] + wa[:, w:])
    gate_ref[...] = _dot3(_sigmoid(pg), gup_ref[...])

    seg = seg_ref[...]
    kk = k * kk_ref[...]
    ss = _dot_exact_rhs(kk * kk, seg)
    kk = kk * lax.rsqrt(jnp.maximum(ss, 1e-24))
    k2 = k * (1.0 + (a - 1.0) * ka_ref[...])
    bonus_ref[...] = _dot_exact_rhs(r * k2 * rk_ref[...], seg) * v

    gc = _dot_exact_lhs(tri_ref[...], lw)
    nc = ts // CHUNK
    ends = [gc[(q + 1) * CHUNK - 1:(q + 1) * CHUNK, :] for q in range(nc)]
    gend = jnp.concatenate([jnp.broadcast_to(e, (CHUNK, w)) for e in ends], axis=0)
    to_end = jnp.exp(gend - gc)
    e_in = jnp.exp(gc)
    e_out = jnp.exp(-gc)
    b = kk * a
    rt_ref[...] = r * e_in
    at_ref[...] = -kk * jnp.exp(gc - lw)
    kt_ref[...] = k2 * e_out
    bt_ref[...] = b * e_out
    kh_ref[...] = k2 * to_end
    bh_ref[...] = b * to_end
    v_ref[...] = v
    dc_ref[0] = jnp.concatenate([jnp.exp(e) for e in ends] + [jnp.zeros((8 - nc, w), F32)], axis=0)


def _rwkv_prep(x, g_mix, w_in, mix, wwa, gup, w0, a0, k_k, k_a, r_k, seg, tri, *, seq, ts=256):
    t, d = x.shape
    pc = w_in.shape[1]
    w = w0.shape[1]
    nt = t // ts
    row = lambda i: (i, 0)
    tok = pl.BlockSpec((ts, w), row)
    tok_shape = jax.ShapeDtypeStruct((t, w), F32)
    return pl.pallas_call(
        functools.partial(_rwkv_prep_kernel, seq_tiles=seq // ts),
        out_shape=[tok_shape] * 7 + [jax.ShapeDtypeStruct((nt, 8, w), F32), tok_shape, tok_shape],
        grid=(nt,),
        in_specs=[
            pl.BlockSpec((ts, d), row),
            _const_spec((1, d)),
            _const_spec(w_in.shape),
            _const_spec((1, pc)),
            _const_spec(wwa.shape),
            _const_spec(gup.shape),
            _const_spec((1, w)), _const_spec((1, w)), _const_spec((1, w)), _const_spec((1, w)),
            _const_spec((1, w)),
            _const_spec(seg.shape),
            _const_spec(tri.shape),
        ],
        out_specs=[tok] * 7 + [pl.BlockSpec((1, 8, w), lambda i: (i, 0, 0)), tok, tok],
        scratch_shapes=[pltpu.VMEM((8, pc), F32)],
        compiler_params=_cparams(("arbitrary",)),
        name="rwkv_prep",
    )(x, g_mix, w_in, mix, wwa, gup, w0, a0, k_k, k_a, r_k, seg, tri)


RWKV_PASSES = 3
PASSES_SCAN = RWKV_PASSES
SCAN_CHUNKS_PER_STEP = 4
INTRA_CHUNKS_PER_STEP = 4
QUAD = 4


def _operand(x, passes):
    return _split2(x) if passes == 3 else (x.astype(BF16),)


def _prod(a, b, mm=_mm):
    if len(a) == 2 and len(b) == 2:
        return mm(a[0], b[0]) + (mm(a[0], b[1]) + mm(a[1], b[0]))
    return mm(a[0], b[0])


def _block_diag(y, n):
    c = y.shape[0]
    tiled = jnp.concatenate([y] * (y.shape[1] // n), axis=0)
    keep = (_iota(tiled.shape, 0) // c) == (_iota(tiled.shape, 1) // n)
    return jnp.where(keep, tiled, jnp.zeros_like(tiled))


def _quad_mm(x, y, n, mm=_mm):
    if RWKV_PASSES == 1:
        return mm(x.astype(BF16), _block_diag(y.astype(BF16), n))
    (xh, xl), (yh, yl) = _split2(x), _split2(y)
    dh, dl = _block_diag(yh, n), _block_diag(yl, n)
    return mm(xh, dh) + (mm(xh, dl) + mm(xl, dh))


def _unit_lower_inverse(a_list, row, col):
    n = CHUNK
    eye = (row == col).astype(F32)
    same8 = (row // 8) == (col // 8)
    a8 = [jnp.where(same8, a, 0.0) for a in a_list]
    a8_2 = [_quad_mm(x, x, n) for x in a8]
    a8_4 = [_quad_mm(x, x, n) for x in a8_2]
    p = [eye + x + x2 + _quad_mm(x, x2, n) for x, x2 in zip(a8, a8_2)]
    t = [pp + _quad_mm(pp, x4, n) for pp, x4 in zip(p, a8_4)]
    m = 16
    while m <= CHUNK:
        sel = ((row // m) == (col // m)) & ((row // (m // 2)) != (col // (m // 2)))
        mid = [_quad_mm(x, jnp.where(sel, a, 0.0), n) for x, a in zip(t, a_list)]
        t = [x + _quad_mm(md, x, n) for x, md in zip(t, mid)]
        m *= 2
    return t


def _rwkv_intra_kernel(rt_ref, at_ref, kt_ref, bt_ref, kh_ref, bh_ref, v_ref, dc_ref,
                       rr_ref, o0_ref, gh_ref, *, heads, chunks_per_tile):
    n = RWKV_HEAD_DIM
    qw = QUAD * n
    cps = gh_ref.shape[0]
    i = pl.program_id(0)
    row = _iota((CHUNK, qw), 0)
    col = _iota((CHUNK, qw), 1) % n
    strict = col < row
    incl = col <= row
    dc_all = dc_ref[0]
    dc_rows = []
    for c in range(cps):
        r = dc_all[c:c + 1, :]
        for q in range(1, chunks_per_tile // cps):
            r = jnp.where(i % (chunks_per_tile // cps) == q, dc_all[q * cps + c:q * cps + c + 1, :], r)
        dc_rows.append(r)

    units = [(slice(c * CHUNK, (c + 1) * CHUNK), slice(j * qw, (j + 1) * qw))
             for c in range(cps) for j in range(heads // QUAD)]
    at = [at_ref[r, s] for r, s in units]
    rt = [rt_ref[r, s] for r, s in units]
    v = [v_ref[r, s] for r, s in units]
    ar = [jnp.concatenate([a, r], axis=0) for a, r in zip(at, rt)]
    mb = [_quad_mm(x, bt_ref[r, s], n, _mm_nt) for x, (r, s) in zip(ar, units)]
    mk = [_quad_mm(x, kt_ref[r, s], n, _mm_nt) for x, (r, s) in zip(ar, units)]
    a_ab = [jnp.where(strict, m[:CHUNK], 0.0) for m in mb]
    a_rb = [jnp.where(incl, m[CHUNK:], 0.0) for m in mb]
    akrk = [jnp.concatenate([jnp.where(strict, m[:CHUNK], 0.0), jnp.where(incl, m[CHUNK:], 0.0)], axis=0)
            for m in mk]
    avv = [_quad_mm(x, y, n) for x, y in zip(akrk, v)]
    tinv = _unit_lower_inverse(a_ab, row, col)
    a_new = [_quad_mm(t, a, n) for t, a in zip(tinv, at)]
    u0 = [_quad_mm(t, w[:CHUNK], n) for t, w in zip(tinv, avv)]
    for u, (r, s) in enumerate(units):
        rr_ref[r, s] = rt[u] + _quad_mm(a_rb[u], a_new[u], n)
        o0_ref[r, s] = _quad_mm(a_rb[u], u0[u], n) + avv[u][CHUNK:]
    eye = _iota((n, n), 0) == _iota((n, n), 1)
    nq = heads // QUAD
    gz, kv = [], []
    for c in range(cps):
        r = slice(c * CHUNK, (c + 1) * CHUNK)
        for h in range(heads):
            u, s = c * nq + h // QUAD, slice((h % QUAD) * n, (h % QUAD + 1) * n)
            hs = slice(h * n, (h + 1) * n)
            z = _operand(jnp.concatenate([a_new[u][:, s], u0[u][:, s]], axis=1), RWKV_PASSES)
            gz.append(_prod(_operand(bh_ref[r, hs], RWKV_PASSES), z, _mm_tn))
            kv.append(_prod(_operand(kh_ref[r, hs], RWKV_PASSES), _operand(v[u][:, s], RWKV_PASSES), _mm_tn))
    for c in range(cps):
        for h in range(heads):
            hs = slice(h * n, (h + 1) * n)
            dmat = jnp.where(eye, jnp.broadcast_to(dc_rows[c][:, hs], (n, n)), 0.0)
            gh_ref[c, h] = gz[c * heads + h] + jnp.concatenate([dmat, kv[c * heads + h]], axis=1)


def _rwkv_scan_kernel(rr_ref, o0_ref, gh_ref, o_ref, state_ref, *, heads):
    n = RWKV_HEAD_DIM

    @pl.when(pl.program_id(1) == 0)
    def _():
        state_ref[...] = jnp.zeros_like(state_ref)

    sls = [slice(h * n, (h + 1) * n) for h in range(heads)]
    state = [state_ref[h] for h in range(heads)]
    for c in range(gh_ref.shape[0]):
        rows = slice(c * CHUNK, (c + 1) * CHUNK)
        h0 = [_operand(x, PASSES_SCAN) for x in state]
        outs = [_prod(_operand(rr_ref[rows, s], PASSES_SCAN), x) + o0_ref[rows, s] for s, x in zip(sls, h0)]
        state = [_prod(_operand(gh_ref[c, h, :, 0:n], PASSES_SCAN), h0[h]) + gh_ref[c, h, :, n:2 * n]
                 for h in range(heads)]
        for h, s in enumerate(sls):
            o_ref[rows, s] = outs[h]
    for h in range(heads):
        state_ref[h] = state[h]


def _rwkv_chunk(rt, at, kt, bt, kh, bh, v, dc, *, batch, seq, prep_ts):
    t, w = rt.shape
    n = RWKV_HEAD_DIM
    heads = w // n
    nchunk = seq // CHUNK
    cpt = prep_ts // CHUNK
    cps = INTRA_CHUNKS_PER_STEP
    assert heads % QUAD == 0 and cpt % cps == 0 and nchunk % SCAN_CHUNKS_PER_STEP == 0
    tok = pl.BlockSpec((cps * CHUNK, w), lambda i: (i, 0))
    tok_shape = jax.ShapeDtypeStruct((t, w), F32)
    rr, o0, gh = pl.pallas_call(
        functools.partial(_rwkv_intra_kernel, heads=heads, chunks_per_tile=cpt),
        out_shape=[tok_shape, tok_shape, jax.ShapeDtypeStruct((t // CHUNK, heads, n, 2 * n), F32)],
        grid=(t // (cps * CHUNK),),
        in_specs=[tok] * 7 + [pl.BlockSpec((1, 8, w), lambda i: (i * cps // cpt, 0, 0))],
        out_specs=[tok, tok, pl.BlockSpec((cps, heads, n, 2 * n), lambda i: (i, 0, 0, 0))],
        compiler_params=_cparams(("parallel",)),
        name="rwkv_intra",
    )(rt, at, kt, bt, kh, bh, v, dc)
    steps = nchunk // SCAN_CHUNKS_PER_STEP
    tok2 = pl.BlockSpec((SCAN_CHUNKS_PER_STEP * CHUNK, w), lambda b, c: (b * steps + c, 0))
    return pl.pallas_call(
        functools.partial(_rwkv_scan_kernel, heads=heads),
        out_shape=tok_shape,
        grid=(batch, steps),
        in_specs=[tok2, tok2, pl.BlockSpec((SCAN_CHUNKS_PER_STEP, heads, n, 2 * n),
                                           lambda b, c: (b * steps + c, 0, 0, 0))],
        out_specs=tok2,
        scratch_shapes=[pltpu.VMEM((heads, n, n), F32)],
        compiler_params=_cparams(("parallel", "arbitrary")),
        name="rwkv_scan",
    )(rr, o0, gh)


def _rope_lanes(x, cos_t, sin_a, sin_b):
    width = x.shape[1]
    up = pltpu.roll(x, width - ROPE_HALF, axis=1)
    dn = pltpu.roll(x, ROPE_HALF, axis=1)
    return x * cos_t + up * sin_a + dn * sin_b


def _tile_lanes(tab, width):
    return jnp.concatenate([tab] * (width // tab.shape[1]), axis=1)


def _nsa_prep_kernel(x_ref, gm_ref, win_ref, cos_ref, sa_ref, sb_ref, qn_ref, kn_ref, seg_ref,
                     q_ref, ks_ref, vs_ref, kw_ref, vw_ref, g_ref, gk_ref, gv_ref, *, seq_tiles):
    dh = NSA_HEAD_DIM
    x = x_ref[...]
    ms = jnp.mean(x * x, axis=-1, keepdims=True)
    p = _mm((x * lax.rsqrt(ms + NORM_EPS) * gm_ref[...]).astype(BF16), win_ref[...])
    qw = q_ref.shape[1]
    kvw = NSA_KV_HEADS * dh
    seg = seg_ref[...]
    cos_t, sin_a, sin_b = cos_ref[...], sa_ref[...], sb_ref[...]

    def norm_rope(x, gain):
        wd = x.shape[1]
        ms = _head_mean(x * x, seg, dh)
        y = x * lax.rsqrt(ms + NORM_EPS) * gain
        return _rope_lanes(y, _tile_lanes(cos_t, wd), _tile_lanes(sin_a, wd), _tile_lanes(sin_b, wd))

    q = norm_rope(p[:, 0:qw], qn_ref[...])
    q_ref[...] = q * (dh ** -0.5)
    base = qw + 2 * kvw
    ks = norm_rope(p[:, base:base + kvw], kn_ref[1:2, :])
    vs = p[:, base + kvw:base + 2 * kvw].astype(BF16)
    kw = norm_rope(p[:, base + 2 * kvw:base + 3 * kvw], kn_ref[2:3, :]).astype(BF16)
    vw = p[:, base + 3 * kvw:base + 4 * kvw].astype(BF16)
    ts = x_ref.shape[0]
    tpos = (pl.program_id(0) % seq_tiles) * ts + _iota((ts, dh), 0)
    onehot = jnp.where(tpos // SEL_BLOCK == _iota((ts, dh), 1), 1.0, 0.0)
    for h in range(NSA_KV_HEADS):
        sl = slice(h * dh, (h + 1) * dh)
        ks_ref[0, h] = jnp.concatenate([ks[:, sl], onehot], axis=1).astype(BF16)
        vs_ref[0, h] = vs[:, sl]
        kw_ref[0, h] = kw[:, sl]
        vw_ref[0, h] = vw[:, sl]
    sig = _sigmoid(p[:, base + 4 * kvw:base + 4 * kvw + LANE])
    sig_t = sig.T
    per_head = 3 * NSA_GROUP
    for h in range(NSA_KV_HEADS):
        g_ref[h] = sig_t[per_head * h:per_head * h + GATE_ROWS, :]
    kc3 = p[:, qw:qw + kvw].reshape(ts // CMP_STRIDE, CMP_STRIDE, kvw)
    vc3 = p[:, qw + kvw:qw + 2 * kvw].reshape(ts // CMP_STRIDE, CMP_STRIDE, kvw)
    for i in range(CMP_STRIDE):
        kci, vci = kc3[:, i, :], vc3[:, i, :]
        for h in range(NSA_KV_HEADS):
            gk_ref[0, h, :, i * dh:(i + 1) * dh] = kci[:, h * dh:(h + 1) * dh]
            gv_ref[0, h, :, i * dh:(i + 1) * dh] = vci[:, h * dh:(h + 1) * dh]


def _nsa_prep(x, g_mix, w_in, cos_t, sin_a, sin_b, qn, kn, seg, *, batch, seq, ts=256):
    t, d = x.shape
    qw = NSA_HEADS * NSA_HEAD_DIM
    st = seq // ts
    tab = pl.BlockSpec((ts, LANE), lambda i: (i % st, 0))
    hm = pl.BlockSpec((1, NSA_KV_HEADS, ts, NSA_HEAD_DIM), lambda i: (i // st, 0, i % st, 0))
    hm_shape = jax.ShapeDtypeStruct((batch, NSA_KV_HEADS, seq, NSA_HEAD_DIM), BF16)
    assert seq // SEL_BLOCK <= NSA_HEAD_DIM
    aug = pl.BlockSpec((1, NSA_KV_HEADS, ts, 2 * NSA_HEAD_DIM), lambda i: (i // st, 0, i % st, 0))
    aug_shape = jax.ShapeDtypeStruct((batch, NSA_KV_HEADS, seq, 2 * NSA_HEAD_DIM), BF16)
    grp_w = CMP_STRIDE * NSA_HEAD_DIM
    grp = pl.BlockSpec((1, NSA_KV_HEADS, ts // CMP_STRIDE, grp_w), lambda i: (i // st, 0, i % st, 0))
    grp_shape = jax.ShapeDtypeStruct((batch, NSA_KV_HEADS, seq // CMP_STRIDE, grp_w), F32)
    return pl.pallas_call(
        functools.partial(_nsa_prep_kernel, seq_tiles=st),
        out_shape=[jax.ShapeDtypeStruct((t, qw), F32), aug_shape] + [hm_shape] * 3
        + [jax.ShapeDtypeStruct((NSA_KV_HEADS, GATE_ROWS, t), F32)] + [grp_shape] * 2,
        grid=(t // ts,),
        in_specs=[pl.BlockSpec((ts, d), lambda i: (i, 0)), _const_spec((1, d)), _const_spec(w_in.shape),
                  tab, tab, tab,
                  _const_spec(qn.shape), _const_spec(kn.shape), _const_spec(seg.shape)],
        out_specs=[pl.BlockSpec((ts, qw), lambda i: (i, 0)), aug] + [hm] * 3
        + [pl.BlockSpec((NSA_KV_HEADS, GATE_ROWS, ts), lambda i: (0, 0, i))] + [grp] * 2,
        compiler_params=_cparams(("parallel",)),
        name="nsa_prep",
    )(x, g_mix, w_in, cos_t, sin_a, sin_b, qn, kn, seg)


def _gelu_tanh(x):
    return 0.5 * x * (1.0 + jnp.tanh(np.sqrt(2.0 / np.pi).astype(np.float32) * (x + 0.044715 * (x * x * x))))


def _compress_kernel(gk_ref, gv_ref, pk_ref, pv_ref, k1_ref, k2_ref, v1_ref, v2_ref, kn_ref,
                     cos_ref, sin_ref, rot_ref, kc_ref, vc_ref):
    half = k1_ref.shape[0] // 2

    def mlp(g, pos, w1_ref, w2_ref):
        ya = _dot3(g, w1_ref[0:half, :])
        yb = _dot3(g, w1_ref[half:, :])
        bias = _dot3(jnp.broadcast_to(pos, (8, pos.shape[1])), w1_ref[...])[0:1, :]
        n = g.shape[0]
        hid = ya + pltpu.roll(yb, n - 1, axis=0) + bias
        return _dot3(_gelu_tanh(hid), w2_ref[...])

    hk, ng, gw = gk_ref.shape[1:]
    tile_rows = lambda tab: jnp.concatenate([tab] * hk, axis=0)
    kc = mlp(gk_ref[0].reshape(hk * ng, gw), pk_ref[...], k1_ref, k2_ref)
    ms = jnp.mean(kc * kc, axis=-1, keepdims=True)
    kc = kc * lax.rsqrt(ms + NORM_EPS) * kn_ref[0:1, :]
    kc = kc * tile_rows(cos_ref[...]) + _dot_exact_rhs(kc, rot_ref[...]) * tile_rows(sin_ref[...])
    kc_ref[0] = kc.reshape(hk, ng, kc.shape[1])
    vc = mlp(gv_ref[0].reshape(hk * ng, gw), pv_ref[...], v1_ref, v2_ref)
    vc_ref[0] = vc.reshape(hk, ng, vc.shape[1])


def _nsa_compress(gk, gv, pk, pv, k1, k2, v1, v2, kn, cos_c, sin_c, rot):
    b, hk, ng, gw = gk.shape
    dh = NSA_HEAD_DIM
    grp = pl.BlockSpec((1, hk, ng, gw), lambda i: (i, 0, 0, 0))
    out = pl.BlockSpec((1, hk, ng, dh), lambda i: (i, 0, 0, 0))
    shape = jax.ShapeDtypeStruct((b, hk, ng, dh), F32)
    consts = [pk, pv, k1, k2, v1, v2, kn, cos_c, sin_c, rot]
    return pl.pallas_call(
        _compress_kernel,
        out_shape=[shape, shape],
        grid=(b,),
        in_specs=[grp, grp] + [_const_spec(c.shape) for c in consts],
        out_specs=[out, out],
        compiler_params=_cparams(("parallel",)),
        name="nsa_compress",
    )(gk, gv, *consts)


def _nsa_attn_kernel(q_ref, kc_ref, vc_ref, ks_ref, vs_ref, kw_ref, vw_ref, g_ref, ovt_ref, o_ref):
    dh = NSA_HEAD_DIM
    grp = NSA_GROUP
    nh = kc_ref.shape[1]
    qi = pl.program_id(2)
    tq = q_ref.shape[0]
    hc = grp * tq
    cols = nh * hc
    t0 = qi * tq
    ncmp = kc_ref.shape[2]
    nsel = ovt_ref.shape[0]
    kb = ATTN_KEY_BLOCK
    span = WINDOW + tq
    heads = range(nh)

    def lanes(xs):
        return jnp.concatenate(xs, axis=1)

    def every_head(x):
        return lanes([x] * (nh * grp))

    q = q_ref[...]
    q4 = [jnp.concatenate([q[:, (h * grp + g) * dh:(h * grp + g + 1) * dh] for g in range(grp)], axis=0)
          for h in heads]
    q4b = [(x * LOG2_E).astype(BF16) for x in q4]

    st = lanes([_dot3(kc_ref[0, h], q4[h], _mm_nt) for h in heads])
    tl = t0 + _iota((ncmp, cols), 1) % tq
    cmask = _iota((ncmp, cols), 0) * CMP_STRIDE + (CMP_BLOCK - 1) <= tl
    sm = jnp.where(cmask, st, MASKED)
    e = jnp.where(cmask, jnp.exp(sm - jnp.max(sm, axis=0, keepdims=True)), 0.0)
    pt = e / jnp.maximum(jnp.sum(e, axis=0, keepdims=True), 1e-30)
    ptb = pt.astype(BF16)
    o_cmp = lanes([_mm_tn(vc_ref[0, h].astype(BF16), ptb[:, h * hc:(h + 1) * hc]) for h in heads])
    psum = []
    for h in heads:
        acc = pt[:, h * hc:h * hc + tq]
        for g in range(1, grp):
            acc = acc + pt[:, h * hc + g * tq:h * hc + (g + 1) * tq]
        psum.append(acc)
    psum = lanes(psum)

    imp = _dot_exact_lhs(ovt_ref[...], psum)
    blk = _iota(imp.shape, 0)
    cur = (t0 + _iota(imp.shape, 1) % tq) // SEL_BLOCK
    forced = (blk == 0) | (blk == cur) | (blk == cur - 1)
    imp = jnp.where(forced, jnp.inf, jnp.where(blk > cur, -jnp.inf, imp))
    rank = jnp.zeros(imp.shape, jnp.int32)
    for m in range(nsel):
        im = imp[m:m + 1, :]
        ahead = (im > imp) | ((im == imp) & (m < blk))
        rank = rank + ahead.astype(jnp.int32)
    sel_bias = jnp.where(rank < min(SEL_TOP, nsel), 0.0, MASKED).astype(BF16)

    eye = jnp.where(_iota((nsel, dh), 0) == _iota((nsel, dh), 1), 1.0, 0.0).astype(BF16)
    bias_q = _mm_tn(sel_bias, eye)
    q_aug = [jnp.concatenate([q4[h] * LOG2_E, jnp.concatenate([bias_q[h * tq:(h + 1) * tq]] * grp, axis=0)],
                             axis=1).astype(BF16) for h in heads]

    def attend(k_ref, v_ref, qs, k0, state, bias=None):
        m_run, l_run, acc = state
        rows = pl.ds(pl.multiple_of(k0, kb), kb)
        s = lanes([_mm_nt(k_ref[0, h, rows, :], qs[h]) for h in heads])
        if bias is not None:
            s = s + bias
        m_new = jnp.maximum(m_run, jnp.max(s, axis=0, keepdims=True))
        alpha = jnp.exp2(m_run - m_new)
        pb = jnp.exp2(s - m_new)
        l_new = alpha * l_run + jnp.sum(pb, axis=0, keepdims=True)
        pb = pb.astype(BF16)
        pv = lanes([_mm_tn(v_ref[0, h, rows, :], pb[:, h * hc:(h + 1) * hc]) for h in heads])
        return m_new, l_new, alpha * acc + pv

    init = (jnp.full((1, cols), MASKED, F32), jnp.zeros((1, cols), F32), jnp.zeros((dh, cols), F32))
    state = lax.fori_loop(0, t0 // kb, lambda j, st: attend(ks_ref, vs_ref, q_aug, j * kb, st), init)
    for d in range(tq // kb):
        causal = jnp.where(d * kb + _iota((kb, tq), 0) <= _iota((kb, tq), 1), 0.0, MASKED)
        state = attend(ks_ref, vs_ref, q_aug, t0 + d * kb, state, every_head(causal))
    o_slc = state[2] / state[1]

    w0 = pl.multiple_of(jnp.maximum(t0 - WINDOW, 0), tq)
    wrows = pl.ds(w0, span)
    kpos = w0 + _iota((span, tq), 0)
    tw = t0 + _iota((span, tq), 1)
    wbias = jnp.where((kpos <= tw) & (kpos > tw - WINDOW), 0.0, MASKED)
    s = lanes([_mm_nt(kw_ref[0, h, wrows, :], q4b[h]) for h in heads]) + every_head(wbias)
    p = jnp.exp2(s - jnp.max(s, axis=0, keepdims=True))
    pb = p.astype(BF16)
    o_win = (lanes([_mm_tn(vw_ref[0, h, wrows, :], pb[:, h * hc:(h + 1) * hc]) for h in heads])
             / jnp.sum(p, axis=0, keepdims=True))

    gates = g_ref[...]
    grow = [lanes([gates[h, 3 * g + br:3 * g + br + 1, :] for h in heads for g in range(grp)])
            for br in range(3)]
    o4 = grow[0] * o_cmp + grow[1] * o_slc + grow[2] * o_win
    o_ref[...] = jnp.concatenate([o4[:, c * tq:(c + 1) * tq] for c in range(nh * grp)], axis=0)


def _nsa_attn(q, kc, vc, ks_aug, vs, kw, vw, gates_t, overlap_t, *, batch, seq, tq=256,
              nh=ATTN_KV_PER_STEP):
    t, qw = q.shape
    dh = NSA_HEAD_DIM
    gw = nh * NSA_GROUP * dh
    st = seq // tq
    assert tq % ATTN_KEY_BLOCK == 0 and WINDOW % tq == 0 and seq % tq == 0
    ncmp = kc.shape[2]
    cmp_spec = pl.BlockSpec((1, nh, ncmp, dh), lambda b, h, i: (b, h, 0, 0))
    kv_spec = pl.BlockSpec((1, nh, seq, dh), lambda b, h, i: (b, h, 0, 0))
    aug_spec = pl.BlockSpec((1, nh, seq, 2 * dh), lambda b, h, i: (b, h, 0, 0))
    return pl.pallas_call(
        _nsa_attn_kernel,
        out_shape=jax.ShapeDtypeStruct((qw, t), F32),
        grid=(batch, NSA_KV_HEADS // nh, st),
        in_specs=[
            pl.BlockSpec((tq, gw), lambda b, h, i: (b * st + i, h)),
            cmp_spec, cmp_spec, aug_spec, kv_spec, kv_spec, kv_spec,
            pl.BlockSpec((nh, gates_t.shape[1], tq), lambda b, h, i: (h, 0, b * st + i)),
            _const_spec(overlap_t.shape),
        ],
        out_specs=pl.BlockSpec((gw, tq), lambda b, h, i: (h, b * st + i)),
        compiler_params=_cparams(("parallel", "parallel", "arbitrary")),
        name="nsa_attn",
    )(q, kc, vc, ks_aug, vs, kw, vw, gates_t, overlap_t)


def _merge_kernel(x_ref, o_ref, bonus_ref, gate_ref, ybt_ref, pg_ref, gnw_ref, gnb_ref, seg_ref,
                  ua_ref, ub_ref, wo_ref, out_ref):
    d = x_ref.shape[1]
    n = RWKV_HEAD_DIM
    seg = seg_ref[...]
    o = o_ref[...]
    mu = _dot_exact_rhs(o, seg) * (1.0 / n)
    dlt = o - mu
    var = _head_mean(dlt * dlt, seg, n)
    on = dlt * lax.rsqrt(var + GN_EPS) * gnw_ref[...] + gnb_ref[...]
    ya = ((on + bonus_ref[...]) * gate_ref[...]).astype(BF16)
    yb_t = ybt_ref[...].astype(BF16)
    merged = (_sigmoid(pg_ref[:, 0:d]) * _mm(ya, ua_ref[...])
              + _sigmoid(pg_ref[:, d:2 * d]) * _mm_tn(yb_t, ub_ref[...]))
    out_ref[...] = x_ref[...] + _mm(merged.astype(BF16), wo_ref[...])


def _merge(x, o_rwkv, bonus, gate, yb_t, pg, gnw, gnb, seg, ua, ub, wo, *, tm=256):
    t, d = x.shape
    w = o_rwkv.shape[1]
    row = lambda i: (i, 0)
    tokw = pl.BlockSpec((tm, w), row)
    return pl.pallas_call(
        _merge_kernel,
        out_shape=jax.ShapeDtypeStruct((t, d), F32),
        grid=(t // tm,),
        in_specs=[pl.BlockSpec((tm, d), row), tokw, tokw, tokw,
                  pl.BlockSpec((yb_t.shape[0], tm), lambda i: (0, i)),
                  pl.BlockSpec((tm, 2 * d), row),
                  _const_spec((1, w)), _const_spec((1, w)), _const_spec(seg.shape),
                  _const_spec(ua.shape), _const_spec(ub.shape), _const_spec(wo.shape)],
        out_specs=pl.BlockSpec((tm, d), row),
        compiler_params=_cparams(("parallel",)),
        name="merge",
    )(x, o_rwkv, bonus, gate, yb_t, pg, gnw, gnb, seg, ua, ub, wo)


def _block_diag_ones(width, block):
    idx = np.arange(width) // block
    return jnp.asarray(idx[:, None] == idx[None, :], BF16)


def _chunk_lower_ones(ts):
    i = np.arange(ts)
    return jnp.asarray((i[:, None] // CHUNK == i[None, :] // CHUNK) & (i[None, :] <= i[:, None]), BF16)


def _rope_tables(pos):
    inv = ROPE_THETA ** (-jnp.arange(ROPE_HALF, dtype=F32) / ROPE_HALF)
    ang = jnp.asarray(pos).astype(F32)[:, None] * inv[None, :]
    cos, sin = jnp.cos(ang), jnp.sin(ang)
    n = ang.shape[0]
    pad = jnp.zeros((n, NSA_HEAD_DIM - ROPE_DIM), F32)
    zero = jnp.zeros_like(sin)
    cos_h = jnp.concatenate([cos, cos, pad + 1.0], axis=1)
    sa_h = jnp.concatenate([-sin, zero, pad], axis=1)
    sb_h = jnp.concatenate([zero, sin, pad], axis=1)
    return cos_h, sa_h, sb_h


def _rot_half_matrix():
    r = np.zeros((NSA_HEAD_DIM, NSA_HEAD_DIM), np.float32)
    for l in range(ROPE_HALF):
        r[l + ROPE_HALF, l] = -1.0
        r[l, l + ROPE_HALF] = 1.0
    return jnp.asarray(r, BF16)


def _overlap_matrix_t(ncmp_pad, nsel):
    cs = np.arange(ncmp_pad)[None, :] * CMP_STRIDE
    ss = np.arange(nsel)[:, None] * SEL_BLOCK
    ov = np.clip(np.minimum(cs + CMP_BLOCK, ss + SEL_BLOCK) - np.maximum(cs, ss), 0, None) / CMP_BLOCK
    return jnp.asarray(ov, BF16)


def _pad_cols(x, width):
    return jnp.pad(x, ((0, 0), (0, width - x.shape[1])))


def _layer(x, l, ffn1_norm, ffn1_w_gate, ffn1_w_up, ffn1_w_down, mix_norm, w_in,
           rwkv_mix, rwkv_w0, rwkv_w_up, rwkv_a0, rwkv_a_up, rwkv_g_up,
           rwkv_k_k, rwkv_k_a, rwkv_r_k, rwkv_gn_w, rwkv_gn_b,
           nsa_q_norm, nsa_k_norm, cmp_pos_k, cmp_pos_v,
           cmp_k_w1, cmp_k_w2, cmp_v_w1, cmp_v_w2,
           w_branch_rwkv, w_branch_nsa, w_out,
           ffn2_norm, ffn2_w_gate, ffn2_w_up, ffn2_w_down, *, batch, seq):
    t, d = x.shape
    w = rwkv_w0.shape[1]
    dh = NSA_HEAD_DIM
    qw = NSA_HEADS * dh
    kvw = NSA_KV_HEADS * dh
    prep_ts = 256
    row = lambda v: v.reshape(1, -1)

    x = _ffn(x, row(ffn1_norm[l]), ffn1_w_gate[l].astype(BF16), ffn1_w_up[l].astype(BF16),
             ffn1_w_down[l].astype(BF16))

    wi = w_in[l]
    rwkv_cols = 3 * w + DECAY_LORA + ICLR_LORA + GATE_LORA
    rwkv_pad = 3 * w + 3 * LANE
    nsa_cols = qw + 6 * kvw + 3 * NSA_HEADS
    nsa_pad = qw + 6 * kvw + LANE
    g_mix = row(mix_norm[l])
    w_rwkv = _pad_cols(wi[:, :rwkv_cols], rwkv_pad).astype(BF16)
    w_nsa = _pad_cols(wi[:, rwkv_cols:rwkv_cols + nsa_cols], nsa_pad).astype(BF16)
    p_gate = _norm_proj(x, g_mix, wi[:, rwkv_cols + nsa_cols:].astype(BF16), name="proj_gate")

    wwa = jnp.zeros((LANE, 2 * w), F32)
    wwa = wwa.at[:DECAY_LORA, :w].set(rwkv_w_up[l]).at[DECAY_LORA:, w:].set(rwkv_a_up[l])
    gup = jnp.pad(rwkv_g_up[l], ((0, 2 * LANE - GATE_LORA), (0, 0)))
    seg_w = _block_diag_ones(SEG_WIDTH, RWKV_HEAD_DIM)
    (rt, at, kt, bt, kh, bh, v, dc, bonus, gate) = _rwkv_prep(
        x, g_mix, w_rwkv, _pad_cols(row(rwkv_mix[l]), rwkv_pad), wwa, gup, row(rwkv_w0[l]), row(rwkv_a0[l]),
        row(rwkv_k_k[l]), row(rwkv_k_a[l]), row(rwkv_r_k[l]), seg_w, _chunk_lower_ones(prep_ts),
        seq=seq, ts=prep_ts)
    o_rwkv = _rwkv_chunk(rt, at, kt, bt, kh, bh, v, dc, batch=batch, seq=seq, prep_ts=prep_ts)

    cos_t, sin_a, sin_b = _rope_tables(np.arange(seq))
    two = lambda tab: jnp.concatenate([tab, tab], axis=1)
    qn = jnp.tile(row(nsa_q_norm[l]), (1, NSA_HEADS))
    kn = jnp.tile(nsa_k_norm[l], (1, NSA_KV_HEADS))
    q, ks, vs, kw, vw, gates, grp_k, grp_v = _nsa_prep(
        x, g_mix, w_nsa, two(cos_t), two(sin_a), two(sin_b), qn, kn, _block_diag_ones(SEG_WIDTH, dh),
        batch=batch, seq=seq)

    ngrp = seq // CMP_STRIDE
    cend = np.arange(ngrp) * CMP_STRIDE + CMP_BLOCK - 1
    cos_c, sa_c, sb_c = _rope_tables(cend)
    kc, vc = _nsa_compress(
        grp_k, grp_v,
        cmp_pos_k[l].reshape(1, -1), cmp_pos_v[l].reshape(1, -1),
        cmp_k_w1[l], cmp_k_w2[l], cmp_v_w1[l], cmp_v_w2[l], nsa_k_norm[l],
        cos_c, sb_c - sa_c, _rot_half_matrix())
    nsel = seq // SEL_BLOCK
    y_nsa = _nsa_attn(q, kc, vc, ks, vs, kw, vw, gates, _overlap_matrix_t(ngrp, nsel),
                      batch=batch, seq=seq)

    x = _merge(x, o_rwkv, bonus, gate, y_nsa, p_gate, row(rwkv_gn_w[l]), row(rwkv_gn_b[l]), seg_w,
               w_branch_rwkv[l].astype(BF16), w_branch_nsa[l].astype(BF16), w_out[l].astype(BF16))
    return _ffn(x, row(ffn2_norm[l]), ffn2_w_gate[l].astype(BF16), ffn2_w_up[l].astype(BF16),
                ffn2_w_down[l].astype(BF16))


def kernel(x, ffn1_norm, ffn1_w_gate, ffn1_w_up, ffn1_w_down, mix_norm, w_in, rwkv_mix, rwkv_w0, rwkv_w_up, rwkv_a0, rwkv_a_up, rwkv_g_up, rwkv_k_k, rwkv_k_a, rwkv_r_k, rwkv_gn_w, rwkv_gn_b, nsa_q_norm, nsa_k_norm, cmp_pos_k, cmp_pos_v, cmp_k_w1, cmp_k_w2, cmp_v_w1, cmp_v_w2, w_branch_rwkv, w_branch_nsa, w_out, ffn2_norm, ffn2_w_gate, ffn2_w_up, ffn2_w_down):
    batch, seq, d = x.shape
    params = (ffn1_norm, ffn1_w_gate, ffn1_w_up, ffn1_w_down, mix_norm, w_in, rwkv_mix, rwkv_w0,
              rwkv_w_up, rwkv_a0, rwkv_a_up, rwkv_g_up, rwkv_k_k, rwkv_k_a, rwkv_r_k, rwkv_gn_w,
              rwkv_gn_b, nsa_q_norm, nsa_k_norm, cmp_pos_k, cmp_pos_v, cmp_k_w1, cmp_k_w2, cmp_v_w1,
              cmp_v_w2, w_branch_rwkv, w_branch_nsa, w_out, ffn2_norm, ffn2_w_gate, ffn2_w_up,
              ffn2_w_down)
    y = x.reshape(batch * seq, d)
    for l in range(ffn1_norm.shape[0]):
        y = _layer(y, l, *params, batch=batch, seq=seq)
    return y.reshape(batch, seq, d)
```

```python
import functools

import numpy as np
import jax
import jax.numpy as jnp
from jax import lax
from jax.experimental import pallas as pl
from jax.experimental.pallas import tpu as pltpu

F32 = jnp.float32
BF16 = jnp.bfloat16

RWKV_HEAD_DIM = 64
DECAY_LORA = 64
ICLR_LORA = 64
GATE_LORA = 160
GN_EPS = 64e-5
NSA_HEADS = 16
NSA_KV_HEADS = 4
NSA_GROUP = NSA_HEADS // NSA_KV_HEADS
NSA_HEAD_DIM = 64
ROPE_DIM = NSA_HEAD_DIM // 4
ROPE_HALF = ROPE_DIM // 2
ROPE_THETA = 500000.0
CMP_BLOCK = 32
CMP_STRIDE = 16
SEL_BLOCK = 64
SEL_TOP = 16
WINDOW = 512
NORM_EPS = 1e-6

LANE = 128
SEG_WIDTH = 256
CHUNK = 64
VMEM_LIMIT = 56 * 1024 * 1024
MASKED = -1e30
LOG2_E = 1.4426950408889634
GATE_ROWS = 16
ATTN_KEY_BLOCK = 256
ATTN_KV_PER_STEP = 4


def _cparams(sem):
    return pltpu.CompilerParams(dimension_semantics=sem, vmem_limit_bytes=VMEM_LIMIT)


def _const_spec(shape):
    nd = len(shape)
    return pl.BlockSpec(shape, lambda *_: (0,) * nd, pipeline_mode=pl.Buffered(1))


def _mm(a, b):
    return lax.dot_general(a, b, (((1,), (0,)), ((), ())), preferred_element_type=F32)


def _mm_nt(a, b):
    return lax.dot_general(a, b, (((1,), (1,)), ((), ())), preferred_element_type=F32)


def _mm_tn(a, b):
    return lax.dot_general(a, b, (((0,), (0,)), ((), ())), preferred_element_type=F32)


def _split2(x):
    hi = x.astype(BF16)
    lo = (x - hi.astype(F32)).astype(BF16)
    return hi, lo


def _split3(x):
    h1 = x.astype(BF16)
    r1 = x - h1.astype(F32)
    h2 = r1.astype(BF16)
    h3 = (r1 - h2.astype(F32)).astype(BF16)
    return h1, h2, h3


def _dot3(a, b, mm=_mm):
    a1, a2 = _split2(a)
    b1, b2 = _split2(b)
    return mm(a1, b1) + (mm(a1, b2) + mm(a2, b1))


def _mm_groups(a, b):
    k = b.shape[0]
    if a.shape[1] == k:
        return _mm(a, b)
    return jnp.concatenate([_mm(a[:, i * k:(i + 1) * k], b) for i in range(a.shape[1] // k)], axis=1)


def _dot_exact_rhs(a, b_bf16):
    a1, a2 = _split2(a)
    return _mm_groups(a1, b_bf16) + _mm_groups(a2, b_bf16)


def _head_mean(x, seg_bf16, width):
    return _mm_groups(x.astype(BF16), seg_bf16) * (1.0 / width)


def _dot_exact_lhs(a_bf16, b):
    b1, b2, b3 = _split3(b)
    return _mm(a_bf16, b1) + (_mm(a_bf16, b2) + _mm(a_bf16, b3))


def _sigmoid(x):
    return 1.0 / (1.0 + jnp.exp(-x))


def _iota(shape, dim):
    return lax.broadcasted_iota(jnp.int32, shape, dim)


def _ffn_kernel(x_ref, g_ref, wg_ref, wu_ref, wd_ref, o_ref, h_ref):
    j = pl.program_id(1)

    @pl.when(j == 0)
    def _():
        x = x_ref[...]
        ms = jnp.mean(x * x, axis=-1, keepdims=True)
        h_ref[...] = (x * lax.rsqrt(ms + NORM_EPS) * g_ref[...]).astype(BF16)
        o_ref[...] = jnp.zeros_like(o_ref)

    h = h_ref[...]
    gate = _mm(h, wg_ref[...])
    up = _mm(h, wu_ref[...])
    act = (gate * _sigmoid(gate) * up).astype(BF16)
    o_ref[...] += _mm(act, wd_ref[...])

    @pl.when(j == pl.num_programs(1) - 1)
    def _():
        o_ref[...] = x_ref[...] + 0.5 * o_ref[...]


def _ffn(x, g, wg, wu, wd, *, tm=1024, tf=512):
    t, d = x.shape
    f = wg.shape[1]
    assert t % tm == 0 and f % tf == 0
    return pl.pallas_call(
        _ffn_kernel,
        out_shape=jax.ShapeDtypeStruct((t, d), F32),
        grid=(t // tm, f // tf),
        in_specs=[
            pl.BlockSpec((tm, d), lambda i, j: (i, 0)),
            pl.BlockSpec((1, d), lambda i, j: (0, 0)),
            pl.BlockSpec((d, tf), lambda i, j: (0, j)),
            pl.BlockSpec((d, tf), lambda i, j: (0, j)),
            pl.BlockSpec((tf, d), lambda i, j: (j, 0)),
        ],
        out_specs=pl.BlockSpec((tm, d), lambda i, j: (i, 0)),
        scratch_shapes=[pltpu.VMEM((tm, d), BF16)],
        compiler_params=_cparams(("parallel", "arbitrary")),
        name="ffn",
    )(x, g, wg, wu, wd)


def _norm_proj_kernel(x_ref, g_ref, w_ref, o_ref):
    x = x_ref[...]
    ms = jnp.mean(x * x, axis=-1, keepdims=True)
    h = (x * lax.rsqrt(ms + NORM_EPS) * g_ref[...]).astype(BF16)
    o_ref[...] = _mm(h, w_ref[...]).astype(o_ref.dtype)


def _norm_proj(x, g, w, *, tm=256, name="norm_proj"):
    t, d = x.shape
    n = w.shape[1]
    return pl.pallas_call(
        _norm_proj_kernel,
        out_shape=jax.ShapeDtypeStruct((t, n), BF16),
        grid=(t // tm,),
        in_specs=[
            pl.BlockSpec((tm, d), lambda i: (i, 0)),
            _const_spec((1, d)),
            _const_spec((d, n)),
        ],
        out_specs=pl.BlockSpec((tm, n), lambda i: (i, 0)),
        compiler_params=_cparams(("parallel",)),
        name=name,
    )(x, g, w)


def _rwkv_prep_kernel(x_ref, gm_ref, win_ref, mix_ref, wwa_ref, gup_ref, w0_ref, a0_ref, kk_ref, ka_ref,
                      rk_ref, seg_ref, tri_ref,
                      rt_ref, at_ref, kt_ref, bt_ref, kh_ref, bh_ref, v_ref, dc_ref, bonus_ref,
                      gate_ref, last_ref, *, seq_tiles):
    i = pl.program_id(0)
    ts = x_ref.shape[0]
    w = rt_ref.shape[1]
    x = x_ref[...]
    ms = jnp.mean(x * x, axis=-1, keepdims=True)
    p = _mm((x * lax.rsqrt(ms + NORM_EPS) * gm_ref[...]).astype(BF16), win_ref[...])

    @pl.when(i == 0)
    def _():
        last_ref[...] = jnp.zeros_like(last_ref)

    prev = last_ref[7:8, :]
    prev = jnp.where(i % seq_tiles == 0, jnp.zeros_like(prev), prev)
    last_ref[...] = p[ts - 8:ts, :]
    shifted = pltpu.roll(p, 1, axis=0)
    shifted = jnp.where(_iota(p.shape, 0) == 0, prev, shifted)
    xs = p + mix_ref[...] * (shifted - p)

    r = xs[:, 0:w]
    k = xs[:, w:2 * w]
    v = xs[:, 2 * w:3 * w]
    lo = 3 * w
    pwa = xs[:, lo:lo + LANE]
    pg = xs[:, lo + LANE:lo + 3 * LANE]
    lane = _iota(pwa.shape, 1)
    z = jnp.where(lane < DECAY_LORA, jnp.tanh(pwa), pwa)
    wa = _dot3(z, wwa_ref[...])
    wl = w0_ref[...] + wa[:, :w]
    neg = -wl
    softplus = jnp.maximum(neg, 0.0) + jnp.log(1.0 + jnp.exp(-jnp.abs(neg)))
    lw = -jnp.exp(-softplus - 0.5)
    a = _sigmoid(a0_ref[...] + wa[:, w:])
    gate_ref[...] = _dot3(_sigmoid(pg), gup_ref[...])

    seg = seg_ref[...]
    kk = k * kk_ref[...]
    ss = _dot_exact_rhs(kk * kk, seg)
    kk = kk * lax.rsqrt(jnp.maximum(ss, 1e-24))
    k2 = k * (1.0 + (a - 1.0) * ka_ref[...])
    bonus_ref[...] = _dot_exact_rhs(r * k2 * rk_ref[...], seg) * v

    gc = _dot_exact_lhs(tri_ref[...], lw)
    nc = ts // CHUNK
    ends = [gc[(q + 1) * CHUNK - 1:(q + 1) * CHUNK, :] for q in range(nc)]
    gend = jnp.concatenate([jnp.broadcast_to(e, (CHUNK, w)) for e in ends], axis=0)
    to_end = jnp.exp(gend - gc)
    e_in = jnp.exp(gc)
    e_out = jnp.exp(-gc)
    b = kk * a
    rt_ref[...] = r * e_in
    at_ref[...] = -kk * jnp.exp(gc - lw)
    kt_ref[...] = k2 * e_out
    bt_ref[...] = b * e_out
    kh_ref[...] = k2 * to_end
    bh_ref[...] = b * to_end
    v_ref[...] = v
    dc_ref[0] = jnp.concatenate([jnp.exp(e) for e in ends] + [jnp.zeros((8 - nc, w), F32)], axis=0)


def _rwkv_prep(x, g_mix, w_in, mix, wwa, gup, w0, a0, k_k, k_a, r_k, seg, tri, *, seq, ts=256):
    t, d = x.shape
    pc = w_in.shape[1]
    w = w0.shape[1]
    nt = t // ts
    row = lambda i: (i, 0)
    tok = pl.BlockSpec((ts, w), row)
    tok_shape = jax.ShapeDtypeStruct((t, w), F32)
    return pl.pallas_call(
        functools.partial(_rwkv_prep_kernel, seq_tiles=seq // ts),
        out_shape=[tok_shape] * 7 + [jax.ShapeDtypeStruct((nt, 8, w), F32), tok_shape, tok_shape],
        grid=(nt,),
        in_specs=[
            pl.BlockSpec((ts, d), row),
            _const_spec((1, d)),
            _const_spec(w_in.shape),
            _const_spec((1, pc)),
            _const_spec(wwa.shape),
            _const_spec(gup.shape),
            _const_spec((1, w)), _const_spec((1, w)), _const_spec((1, w)), _const_spec((1, w)),
            _const_spec((1, w)),
            _const_spec(seg.shape),
            _const_spec(tri.shape),
        ],
        out_specs=[tok] * 7 + [pl.BlockSpec((1, 8, w), lambda i: (i, 0, 0)), tok, tok],
        scratch_shapes=[pltpu.VMEM((8, pc), F32)],
        compiler_params=_cparams(("arbitrary",)),
        name="rwkv_prep",
    )(x, g_mix, w_in, mix, wwa, gup, w0, a0, k_k, k_a, r_k, seg, tri)


PASSES_SCAN = 1
SCAN_CHUNKS_PER_STEP = 4
INTRA_CHUNKS_PER_STEP = 4
QUAD = 4


def _operand(x, passes):
    return _split2(x) if passes == 3 else (x.astype(BF16),)


def _prod(a, b, mm=_mm):
    if len(a) == 2 and len(b) == 2:
        return mm(a[0], b[0]) + (mm(a[0], b[1]) + mm(a[1], b[0]))
    return mm(a[0], b[0])


def _block_diag(y, n):
    c = y.shape[0]
    tiled = jnp.concatenate([y] * (y.shape[1] // n), axis=0)
    keep = (_iota(tiled.shape, 0) // c) == (_iota(tiled.shape, 1) // n)
    return jnp.where(keep, tiled, jnp.zeros_like(tiled))


def _quad_mm(x, y, n, mm=_mm):
    return mm(x.astype(BF16), _block_diag(y.astype(BF16), n))


def _unit_lower_inverse(a_list, row, col):
    n = CHUNK
    eye = (row == col).astype(F32)
    same8 = (row // 8) == (col // 8)
    a8 = [jnp.where(same8, a, 0.0) for a in a_list]
    d8 = [_block_diag(x.astype(BF16), n) for x in a8]
    a8_2 = [_mm(x.astype(BF16), d) for x, d in zip(a8, d8)]
    d8_2 = [_block_diag(x.astype(BF16), n) for x in a8_2]
    a8_4 = [_mm(x.astype(BF16), d) for x, d in zip(a8_2, d8_2)]
    p = [eye + x + x2 + _mm(x.astype(BF16), d2) for x, x2, d2 in zip(a8, a8_2, d8_2)]
    t = [pp + _quad_mm(pp, x4, n) for pp, x4 in zip(p, a8_4)]
    m = 16
    while m <= CHUNK:
        sel = ((row // m) == (col // m)) & ((row // (m // 2)) != (col // (m // 2)))
        mid = [_quad_mm(x, jnp.where(sel, a, 0.0), n) for x, a in zip(t, a_list)]
        t = [x + _quad_mm(md, x, n) for x, md in zip(t, mid)]
        m *= 2
    return t


def _rwkv_intra_kernel(rt_ref, at_ref, kt_ref, bt_ref, kh_ref, bh_ref, v_ref, dc_ref,
                       rr_ref, o0_ref, gh_ref, *, heads, chunks_per_tile):
    n = RWKV_HEAD_DIM
    qw = QUAD * n
    cps = gh_ref.shape[0]
    i = pl.program_id(0)
    row = _iota((CHUNK, qw), 0)
    col = _iota((CHUNK, qw), 1) % n
    strict = col < row
    incl = col <= row
    dc_all = dc_ref[0]
    dc_rows = []
    for c in range(cps):
        r = dc_all[c:c + 1, :]
        for q in range(1, chunks_per_tile // cps):
            r = jnp.where(i % (chunks_per_tile // cps) == q, dc_all[q * cps + c:q * cps + c + 1, :], r)
        dc_rows.append(r)

    units = [(slice(c * CHUNK, (c + 1) * CHUNK), slice(j * qw, (j + 1) * qw))
             for c in range(cps) for j in range(heads // QUAD)]
    at = [at_ref[r, s] for r, s in units]
    rt = [rt_ref[r, s] for r, s in units]
    v = [v_ref[r, s] for r, s in units]
    bd_b = [_block_diag(bt_ref[r, s].astype(BF16), n) for r, s in units]
    bd_k = [_block_diag(kt_ref[r, s].astype(BF16), n) for r, s in units]
    ar = [jnp.concatenate([a, r], axis=0).astype(BF16) for a, r in zip(at, rt)]
    mb = [_mm_nt(x, d) for x, d in zip(ar, bd_b)]
    mk = [_mm_nt(x, d) for x, d in zip(ar, bd_k)]
    a_ab = [jnp.where(strict, m[:CHUNK], 0.0) for m in mb]
    a_rb = [jnp.where(incl, m[CHUNK:], 0.0) for m in mb]
    akrk = [jnp.concatenate([jnp.where(strict, m[:CHUNK], 0.0), jnp.where(incl, m[CHUNK:], 0.0)], axis=0)
            for m in mk]
    avv = [_quad_mm(x, y, n) for x, y in zip(akrk, v)]
    tinv = _unit_lower_inverse(a_ab, row, col)
    a_new = [_quad_mm(t, a, n) for t, a in zip(tinv, at)]
    u0 = [_quad_mm(t, w[:CHUNK], n) for t, w in zip(tinv, avv)]
    for u, (r, s) in enumerate(units):
        rr_ref[r, s] = rt[u] + _quad_mm(a_rb[u], a_new[u], n)
        o0_ref[r, s] = _quad_mm(a_rb[u], u0[u], n) + avv[u][CHUNK:]
    eye = _iota((n, n), 0) == _iota((n, n), 1)
    nq = heads // QUAD
    gz, kv = [], []
    for c in range(cps):
        r = slice(c * CHUNK, (c + 1) * CHUNK)
        for h in range(heads):
            u, s = c * nq + h // QUAD, slice((h % QUAD) * n, (h % QUAD + 1) * n)
            hs = slice(h * n, (h + 1) * n)
            z = jnp.concatenate([a_new[u][:, s], u0[u][:, s]], axis=1).astype(BF16)
            gz.append(_mm_tn(bh_ref[r, hs].astype(BF16), z))
            kv.append(_mm_tn(kh_ref[r, hs].astype(BF16), v[u][:, s].astype(BF16)))
    for c in range(cps):
        for h in range(heads):
            hs = slice(h * n, (h + 1) * n)
            dmat = jnp.where(eye, jnp.broadcast_to(dc_rows[c][:, hs], (n, n)), 0.0)
            gh_ref[c, h] = gz[c * heads + h] + jnp.concatenate([dmat, kv[c * heads + h]], axis=1)


def _rwkv_scan_kernel(rr_ref, o0_ref, gh_ref, o_ref, state_ref, *, heads):
    n = RWKV_HEAD_DIM

    @pl.when(pl.program_id(1) == 0)
    def _():
        state_ref[...] = jnp.zeros_like(state_ref)

    sls = [slice(h * n, (h + 1) * n) for h in range(heads)]
    state = [state_ref[h] for h in range(heads)]
    for c in range(gh_ref.shape[0]):
        rows = slice(c * CHUNK, (c + 1) * CHUNK)
        h0 = [_operand(x, PASSES_SCAN) for x in state]
        outs = [_prod(_operand(rr_ref[rows, s], PASSES_SCAN), x) + o0_ref[rows, s] for s, x in zip(sls, h0)]
        state = [_prod(_operand(gh_ref[c, h, :, 0:n], PASSES_SCAN), h0[h]) + gh_ref[c, h, :, n:2 * n]
                 for h in range(heads)]
        for h, s in enumerate(sls):
            o_ref[rows, s] = outs[h]
    for h in range(heads):
        state_ref[h] = state[h]


def _rwkv_chunk(rt, at, kt, bt, kh, bh, v, dc, *, batch, seq, prep_ts):
    t, w = rt.shape
    n = RWKV_HEAD_DIM
    heads = w // n
    nchunk = seq // CHUNK
    cpt = prep_ts // CHUNK
    cps = INTRA_CHUNKS_PER_STEP
    assert heads % QUAD == 0 and cpt % cps == 0 and nchunk % SCAN_CHUNKS_PER_STEP == 0
    tok = pl.BlockSpec((cps * CHUNK, w), lambda i: (i, 0))
    tok_shape = jax.ShapeDtypeStruct((t, w), F32)
    rr, o0, gh = pl.pallas_call(
        functools.partial(_rwkv_intra_kernel, heads=heads, chunks_per_tile=cpt),
        out_shape=[tok_shape, tok_shape, jax.ShapeDtypeStruct((t // CHUNK, heads, n, 2 * n), F32)],
        grid=(t // (cps * CHUNK),),
        in_specs=[tok] * 7 + [pl.BlockSpec((1, 8, w), lambda i: (i * cps // cpt, 0, 0))],
        out_specs=[tok, tok, pl.BlockSpec((cps, heads, n, 2 * n), lambda i: (i, 0, 0, 0))],
        compiler_params=_cparams(("parallel",)),
        name="rwkv_intra",
    )(rt, at, kt, bt, kh, bh, v, dc)
    steps = nchunk // SCAN_CHUNKS_PER_STEP
    tok2 = pl.BlockSpec((SCAN_CHUNKS_PER_STEP * CHUNK, w), lambda b, c: (b * steps + c, 0))
    return pl.pallas_call(
        functools.partial(_rwkv_scan_kernel, heads=heads),
        out_shape=tok_shape,
        grid=(batch, steps),
        in_specs=[tok2, tok2, pl.BlockSpec((SCAN_CHUNKS_PER_STEP, heads, n, 2 * n),
                                           lambda b, c: (b * steps + c, 0, 0, 0))],
        out_specs=tok2,
        scratch_shapes=[pltpu.VMEM((heads, n, n), F32)],
        compiler_params=_cparams(("parallel", "arbitrary")),
        name="rwkv_scan",
    )(rr, o0, gh)


def _rope_lanes(x, cos_t, sin_a, sin_b):
    width = x.shape[1]
    up = pltpu.roll(x, width - ROPE_HALF, axis=1)
    dn = pltpu.roll(x, ROPE_HALF, axis=1)
    return x * cos_t + up * sin_a + dn * sin_b


def _tile_lanes(tab, width):
    return jnp.concatenate([tab] * (width // tab.shape[1]), axis=1)


def _nsa_prep_kernel(x_ref, gm_ref, win_ref, cos_ref, sa_ref, sb_ref, qn_ref, kn_ref, seg_ref,
                     q_ref, ks_ref, vs_ref, kw_ref, vw_ref, g_ref, gk_ref, gv_ref, *, seq_tiles):
    dh = NSA_HEAD_DIM
    x = x_ref[...]
    ms = jnp.mean(x * x, axis=-1, keepdims=True)
    p = _mm((x * lax.rsqrt(ms + NORM_EPS) * gm_ref[...]).astype(BF16), win_ref[...])
    qw = q_ref.shape[1]
    kvw = NSA_KV_HEADS * dh
    seg = seg_ref[...]
    cos_t, sin_a, sin_b = cos_ref[...], sa_ref[...], sb_ref[...]

    def norm_rope(x, gain):
        wd = x.shape[1]
        ms = _head_mean(x * x, seg, dh)
        y = x * lax.rsqrt(ms + NORM_EPS) * gain
        return _rope_lanes(y, _tile_lanes(cos_t, wd), _tile_lanes(sin_a, wd), _tile_lanes(sin_b, wd))

    q = norm_rope(p[:, 0:qw], qn_ref[...])
    q_ref[...] = q * (dh ** -0.5)
    base = qw + 2 * kvw
    ks = norm_rope(p[:, base:base + kvw], kn_ref[1:2, :])
    vs = p[:, base + kvw:base + 2 * kvw].astype(BF16)
    kw = norm_rope(p[:, base + 2 * kvw:base + 3 * kvw], kn_ref[2:3, :]).astype(BF16)
    vw = p[:, base + 3 * kvw:base + 4 * kvw].astype(BF16)
    ts = x_ref.shape[0]
    tpos = (pl.program_id(0) % seq_tiles) * ts + _iota((ts, dh), 0)
    onehot = jnp.where(tpos // SEL_BLOCK == _iota((ts, dh), 1), 1.0, 0.0)
    for h in range(NSA_KV_HEADS):
        sl = slice(h * dh, (h + 1) * dh)
        ks_ref[0, h] = jnp.concatenate([ks[:, sl], onehot], axis=1).astype(BF16)
        vs_ref[0, h] = vs[:, sl]
        kw_ref[0, h] = kw[:, sl]
        vw_ref[0, h] = vw[:, sl]
    sig = _sigmoid(p[:, base + 4 * kvw:base + 4 * kvw + LANE])
    sig_t = sig.T
    per_head = 3 * NSA_GROUP
    for h in range(NSA_KV_HEADS):
        g_ref[h] = sig_t[per_head * h:per_head * h + GATE_ROWS, :]
    kc3 = p[:, qw:qw + kvw].reshape(ts // CMP_STRIDE, CMP_STRIDE, kvw)
    vc3 = p[:, qw + kvw:qw + 2 * kvw].reshape(ts // CMP_STRIDE, CMP_STRIDE, kvw)
    for i in range(CMP_STRIDE):
        kci, vci = kc3[:, i, :], vc3[:, i, :]
        for h in range(NSA_KV_HEADS):
            gk_ref[0, h, :, i * dh:(i + 1) * dh] = kci[:, h * dh:(h + 1) * dh]
            gv_ref[0, h, :, i * dh:(i + 1) * dh] = vci[:, h * dh:(h + 1) * dh]


def _nsa_prep(x, g_mix, w_in, cos_t, sin_a, sin_b, qn, kn, seg, *, batch, seq, ts=256):
    t, d = x.shape
    qw = NSA_HEADS * NSA_HEAD_DIM
    st = seq // ts
    tab = pl.BlockSpec((ts, LANE), lambda i: (i % st, 0))
    hm = pl.BlockSpec((1, NSA_KV_HEADS, ts, NSA_HEAD_DIM), lambda i: (i // st, 0, i % st, 0))
    hm_shape = jax.ShapeDtypeStruct((batch, NSA_KV_HEADS, seq, NSA_HEAD_DIM), BF16)
    assert seq // SEL_BLOCK <= NSA_HEAD_DIM
    aug = pl.BlockSpec((1, NSA_KV_HEADS, ts, 2 * NSA_HEAD_DIM), lambda i: (i // st, 0, i % st, 0))
    aug_shape = jax.ShapeDtypeStruct((batch, NSA_KV_HEADS, seq, 2 * NSA_HEAD_DIM), BF16)
    grp_w = CMP_STRIDE * NSA_HEAD_DIM
    grp = pl.BlockSpec((1, NSA_KV_HEADS, ts // CMP_STRIDE, grp_w), lambda i: (i // st, 0, i % st, 0))
    grp_shape = jax.ShapeDtypeStruct((batch, NSA_KV_HEADS, seq // CMP_STRIDE, grp_w), F32)
    return pl.pallas_call(
        functools.partial(_nsa_prep_kernel, seq_tiles=st),
        out_shape=[jax.ShapeDtypeStruct((t, qw), F32), aug_shape] + [hm_shape] * 3
        + [jax.ShapeDtypeStruct((NSA_KV_HEADS, GATE_ROWS, t), F32)] + [grp_shape] * 2,
        grid=(t // ts,),
        in_specs=[pl.BlockSpec((ts, d), lambda i: (i, 0)), _const_spec((1, d)), _const_spec(w_in.shape),
                  tab, tab, tab,
                  _const_spec(qn.shape), _const_spec(kn.shape), _const_spec(seg.shape)],
        out_specs=[pl.BlockSpec((ts, qw), lambda i: (i, 0)), aug] + [hm] * 3
        + [pl.BlockSpec((NSA_KV_HEADS, GATE_ROWS, ts), lambda i: (0, 0, i))] + [grp] * 2,
        compiler_params=_cparams(("parallel",)),
        name="nsa_prep",
    )(x, g_mix, w_in, cos_t, sin_a, sin_b, qn, kn, seg)


def _gelu_tanh(x):
    return 0.5 * x * (1.0 + jnp.tanh(np.sqrt(2.0 / np.pi).astype(np.float32) * (x + 0.044715 * (x * x * x))))


def _compress_kernel(gk_ref, gv_ref, pk_ref, pv_ref, k1_ref, k2_ref, v1_ref, v2_ref, kn_ref,
                     cos_ref, sin_ref, rot_ref, kc_ref, vc_ref):
    half = k1_ref.shape[0] // 2

    def mlp(g, pos, w1_ref, w2_ref):
        ya = _dot3(g, w1_ref[0:half, :])
        yb = _dot3(g, w1_ref[half:, :])
        bias = _dot3(jnp.broadcast_to(pos, (8, pos.shape[1])), w1_ref[...])[0:1, :]
        n = g.shape[0]
        hid = ya + pltpu.roll(yb, n - 1, axis=0) + bias
        return _dot3(_gelu_tanh(hid), w2_ref[...])

    hk, ng, gw = gk_ref.shape[1:]
    tile_rows = lambda tab: jnp.concatenate([tab] * hk, axis=0)
    kc = mlp(gk_ref[0].reshape(hk * ng, gw), pk_ref[...], k1_ref, k2_ref)
    ms = jnp.mean(kc * kc, axis=-1, keepdims=True)
    kc = kc * lax.rsqrt(ms + NORM_EPS) * kn_ref[0:1, :]
    kc = kc * tile_rows(cos_ref[...]) + _dot_exact_rhs(kc, rot_ref[...]) * tile_rows(sin_ref[...])
    kc_ref[0] = kc.reshape(hk, ng, kc.shape[1])
    vc = mlp(gv_ref[0].reshape(hk * ng, gw), pv_ref[...], v1_ref, v2_ref)
    vc_ref[0] = vc.reshape(hk, ng, vc.shape[1])


def _nsa_compress(gk, gv, pk, pv, k1, k2, v1, v2, kn, cos_c, sin_c, rot):
    b, hk, ng, gw = gk.shape
    dh = NSA_HEAD_DIM
    grp = pl.BlockSpec((1, hk, ng, gw), lambda i: (i, 0, 0, 0))
    out = pl.BlockSpec((1, hk, ng, dh), lambda i: (i, 0, 0, 0))
    shape = jax.ShapeDtypeStruct((b, hk, ng, dh), F32)
    consts = [pk, pv, k1, k2, v1, v2, kn, cos_c, sin_c, rot]
    return pl.pallas_call(
        _compress_kernel,
        out_shape=[shape, shape],
        grid=(b,),
        in_specs=[grp, grp] + [_const_spec(c.shape) for c in consts],
        out_specs=[out, out],
        compiler_params=_cparams(("parallel",)),
        name="nsa_compress",
    )(gk, gv, *consts)


def _nsa_attn_kernel(q_ref, kc_ref, vc_ref, ks_ref, vs_ref, kw_ref, vw_ref, g_ref, ovt_ref, o_ref):
    dh = NSA_HEAD_DIM
    grp = NSA_GROUP
    nh = kc_ref.shape[1]
    qi = pl.program_id(2)
    tq = q_ref.shape[0]
    hc = grp * tq
    cols = nh * hc
    t0 = qi * tq
    ncmp = kc_ref.shape[2]
    nsel = ovt_ref.shape[0]
    kb = ATTN_KEY_BLOCK
    span = WINDOW + tq
    heads = range(nh)

    def lanes(xs):
        return jnp.concatenate(xs, axis=1)

    def every_head(x):
        return lanes([x] * (nh * grp))

    q = q_ref[...]
    q4 = [jnp.concatenate([q[:, (h * grp + g) * dh:(h * grp + g + 1) * dh] for g in range(grp)], axis=0)
          for h in heads]
    q4b = [(x * LOG2_E).astype(BF16) for x in q4]

    st = lanes([_dot3(kc_ref[0, h], q4[h], _mm_nt) for h in heads])
    tl = t0 + _iota((ncmp, cols), 1) % tq
    cmask = _iota((ncmp, cols), 0) * CMP_STRIDE + (CMP_BLOCK - 1) <= tl
    sm = jnp.where(cmask, st, MASKED)
    e = jnp.where(cmask, jnp.exp(sm - jnp.max(sm, axis=0, keepdims=True)), 0.0)
    pt = e / jnp.maximum(jnp.sum(e, axis=0, keepdims=True), 1e-30)
    ptb = pt.astype(BF16)
    o_cmp = lanes([_mm_tn(vc_ref[0, h].astype(BF16), ptb[:, h * hc:(h + 1) * hc]) for h in heads])
    psum = []
    for h in heads:
        acc = pt[:, h * hc:h * hc + tq]
        for g in range(1, grp):
            acc = acc + pt[:, h * hc + g * tq:h * hc + (g + 1) * tq]
        psum.append(acc)
    psum = lanes(psum)

    imp = _dot_exact_lhs(ovt_ref[...], psum)
    blk = _iota(imp.shape, 0)
    cur = (t0 + _iota(imp.shape, 1) % tq) // SEL_BLOCK
    forced = (blk == 0) | (blk == cur) | (blk == cur - 1)
    imp = jnp.where(forced, jnp.inf, jnp.where(blk > cur, -jnp.inf, imp))
    rank = jnp.zeros(imp.shape, jnp.int32)
    for m in range(nsel):
        im = imp[m:m + 1, :]
        ahead = (im > imp) | ((im == imp) & (m < blk))
        rank = rank + ahead.astype(jnp.int32)
    sel_bias = jnp.where(rank < min(SEL_TOP, nsel), 0.0, MASKED).astype(BF16)

    eye = jnp.where(_iota((nsel, dh), 0) == _iota((nsel, dh), 1), 1.0, 0.0).astype(BF16)
    bias_q = _mm_tn(sel_bias, eye)
    q_aug = [jnp.concatenate([q4[h] * LOG2_E, jnp.concatenate([bias_q[h * tq:(h + 1) * tq]] * grp, axis=0)],
                             axis=1).astype(BF16) for h in heads]

    def attend(k_ref, v_ref, qs, k0, state, bias=None):
        m_run, l_run, acc = state
        rows = pl.ds(pl.multiple_of(k0, kb), kb)
        s = lanes([_mm_nt(k_ref[0, h, rows, :], qs[h]) for h in heads])
        if bias is not None:
            s = s + bias
        m_new = jnp.maximum(m_run, jnp.max(s, axis=0, keepdims=True))
        alpha = jnp.exp2(m_run - m_new)
        pb = jnp.exp2(s - m_new)
        l_new = alpha * l_run + jnp.sum(pb, axis=0, keepdims=True)
        pb = pb.astype(BF16)
        pv = lanes([_mm_tn(v_ref[0, h, rows, :], pb[:, h * hc:(h + 1) * hc]) for h in heads])
        return m_new, l_new, alpha * acc + pv

    init = (jnp.full((1, cols), MASKED, F32), jnp.zeros((1, cols), F32), jnp.zeros((dh, cols), F32))
    state = lax.fori_loop(0, t0 // kb, lambda j, st: attend(ks_ref, vs_ref, q_aug, j * kb, st), init)
    for d in range(tq // kb):
        causal = jnp.where(d * kb + _iota((kb, tq), 0) <= _iota((kb, tq), 1), 0.0, MASKED)
        state = attend(ks_ref, vs_ref, q_aug, t0 + d * kb, state, every_head(causal))
    o_slc = state[2] / state[1]

    w0 = pl.multiple_of(jnp.maximum(t0 - WINDOW, 0), tq)
    wrows = pl.ds(w0, span)
    kpos = w0 + _iota((span, tq), 0)
    tw = t0 + _iota((span, tq), 1)
    wbias = jnp.where((kpos <= tw) & (kpos > tw - WINDOW), 0.0, MASKED)
    s = lanes([_mm_nt(kw_ref[0, h, wrows, :], q4b[h]) for h in heads]) + every_head(wbias)
    p = jnp.exp2(s - jnp.max(s, axis=0, keepdims=True))
    pb = p.astype(BF16)
    o_win = (lanes([_mm_tn(vw_ref[0, h, wrows, :], pb[:, h * hc:(h + 1) * hc]) for h in heads])
             / jnp.sum(p, axis=0, keepdims=True))

    gates = g_ref[...]
    grow = [lanes([gates[h, 3 * g + br:3 * g + br + 1, :] for h in heads for g in range(grp)])
            for br in range(3)]
    o4 = grow[0] * o_cmp + grow[1] * o_slc + grow[2] * o_win
    o_ref[...] = jnp.concatenate([o4[:, c * tq:(c + 1) * tq] for c in range(nh * grp)], axis=0)


def _nsa_attn(q, kc, vc, ks_aug, vs, kw, vw, gates_t, overlap_t, *, batch, seq, tq=256,
              nh=ATTN_KV_PER_STEP):
    t, qw = q.shape
    dh = NSA_HEAD_DIM
    gw = nh * NSA_GROUP * dh
    st = seq // tq
    assert tq % ATTN_KEY_BLOCK == 0 and WINDOW % tq == 0 and seq % tq == 0
    ncmp = kc.shape[2]
    cmp_spec = pl.BlockSpec((1, nh, ncmp, dh), lambda b, h, i: (b, h, 0, 0))
    kv_spec = pl.BlockSpec((1, nh, seq, dh), lambda b, h, i: (b, h, 0, 0))
    aug_spec = pl.BlockSpec((1, nh, seq, 2 * dh), lambda b, h, i: (b, h, 0, 0))
    return pl.pallas_call(
        _nsa_attn_kernel,
        out_shape=jax.ShapeDtypeStruct((qw, t), F32),
        grid=(batch, NSA_KV_HEADS // nh, st),
        in_specs=[
            pl.BlockSpec((tq, gw), lambda b, h, i: (b * st + i, h)),
            cmp_spec, cmp_spec, aug_spec, kv_spec, kv_spec, kv_spec,
            pl.BlockSpec((nh, gates_t.shape[1], tq), lambda b, h, i: (h, 0, b * st + i)),
            _const_spec(overlap_t.shape),
        ],
        out_specs=pl.BlockSpec((gw, tq), lambda b, h, i: (h, b * st + i)),
        compiler_params=_cparams(("parallel", "parallel", "arbitrary")),
        name="nsa_attn",
    )(q, kc, vc, ks_aug, vs, kw, vw, gates_t, overlap_t)


def _merge_kernel(x_ref, o_ref, bonus_ref, gate_ref, ybt_ref, pg_ref, gnw_ref, gnb_ref, seg_ref,
                  ua_ref, ub_ref, wo_ref, out_ref):
    d = x_ref.shape[1]
    n = RWKV_HEAD_DIM
    seg = seg_ref[...]
    o = o_ref[...]
    mu = _dot_exact_rhs(o, seg) * (1.0 / n)
    dlt = o - mu
    var = _head_mean(dlt * dlt, seg, n)
    on = dlt * lax.rsqrt(var + GN_EPS) * gnw_ref[...] + gnb_ref[...]
    ya = ((on + bonus_ref[...]) * gate_ref[...]).astype(BF16)
    yb_t = ybt_ref[...].astype(BF16)
    merged = (_sigmoid(pg_ref[:, 0:d].astype(F32)) * _mm(ya, ua_ref[...])
              + _sigmoid(pg_ref[:, d:2 * d].astype(F32)) * _mm_tn(yb_t, ub_ref[...]))
    out_ref[...] = x_ref[...] + _mm(merged.astype(BF16), wo_ref[...])


def _merge(x, o_rwkv, bonus, gate, yb_t, pg, gnw, gnb, seg, ua, ub, wo, *, tm=256):
    t, d = x.shape
    w = o_rwkv.shape[1]
    row = lambda i: (i, 0)
    tokw = pl.BlockSpec((tm, w), row)
    return pl.pallas_call(
        _merge_kernel,
        out_shape=jax.ShapeDtypeStruct((t, d), F32),
        grid=(t // tm,),
        in_specs=[pl.BlockSpec((tm, d), row), tokw, tokw, tokw,
                  pl.BlockSpec((yb_t.shape[0], tm), lambda i: (0, i)),
                  pl.BlockSpec((tm, 2 * d), row),
                  _const_spec((1, w)), _const_spec((1, w)), _const_spec(seg.shape),
                  _const_spec(ua.shape), _const_spec(ub.shape), _const_spec(wo.shape)],
        out_specs=pl.BlockSpec((tm, d), row),
        compiler_params=_cparams(("parallel",)),
        name="merge",
    )(x, o_rwkv, bonus, gate, yb_t, pg, gnw, gnb, seg, ua, ub, wo)


def _block_diag_ones(width, block):
    idx = np.arange(width) // block
    return jnp.asarray(idx[:, None] == idx[None, :], BF16)


def _chunk_lower_ones(ts):
    i = np.arange(ts)
    return jnp.asarray((i[:, None] // CHUNK == i[None, :] // CHUNK) & (i[None, :] <= i[:, None]), BF16)


def _rope_tables(pos):
    inv = ROPE_THETA ** (-jnp.arange(ROPE_HALF, dtype=F32) / ROPE_HALF)
    ang = jnp.asarray(pos).astype(F32)[:, None] * inv[None, :]
    cos, sin = jnp.cos(ang), jnp.sin(ang)
    n = ang.shape[0]
    pad = jnp.zeros((n, NSA_HEAD_DIM - ROPE_DIM), F32)
    zero = jnp.zeros_like(sin)
    cos_h = jnp.concatenate([cos, cos, pad + 1.0], axis=1)
    sa_h = jnp.concatenate([-sin, zero, pad], axis=1)
    sb_h = jnp.concatenate([zero, sin, pad], axis=1)
    return cos_h, sa_h, sb_h


def _rot_half_matrix():
    r = np.zeros((NSA_HEAD_DIM, NSA_HEAD_DIM), np.float32)
    for l in range(ROPE_HALF):
        r[l + ROPE_HALF, l] = -1.0
        r[l, l + ROPE_HALF] = 1.0
    return jnp.asarray(r, BF16)


def _overlap_matrix_t(ncmp_pad, nsel):
    cs = np.arange(ncmp_pad)[None, :] * CMP_STRIDE
    ss = np.arange(nsel)[:, None] * SEL_BLOCK
    ov = np.clip(np.minimum(cs + CMP_BLOCK, ss + SEL_BLOCK) - np.maximum(cs, ss), 0, None) / CMP_BLOCK
    return jnp.asarray(ov, BF16)


def _pad_cols(x, width):
    return jnp.pad(x, ((0, 0), (0, width - x.shape[1])))


def _layer(x, l, ffn1_norm, ffn1_w_gate, ffn1_w_up, ffn1_w_down, mix_norm, w_in,
           rwkv_mix, rwkv_w0, rwkv_w_up, rwkv_a0, rwkv_a_up, rwkv_g_up,
           rwkv_k_k, rwkv_k_a, rwkv_r_k, rwkv_gn_w, rwkv_gn_b,
           nsa_q_norm, nsa_k_norm, cmp_pos_k, cmp_pos_v,
           cmp_k_w1, cmp_k_w2, cmp_v_w1, cmp_v_w2,
           w_branch_rwkv, w_branch_nsa, w_out,
           ffn2_norm, ffn2_w_gate, ffn2_w_up, ffn2_w_down, *, batch, seq):
    t, d = x.shape
    w = rwkv_w0.shape[1]
    dh = NSA_HEAD_DIM
    qw = NSA_HEADS * dh
    kvw = NSA_KV_HEADS * dh
    prep_ts = 256
    row = lambda v: v.reshape(1, -1)

    x = _ffn(x, row(ffn1_norm[l]), ffn1_w_gate[l].astype(BF16), ffn1_w_up[l].astype(BF16),
             ffn1_w_down[l].astype(BF16))

    wi = w_in[l]
    rwkv_cols = 3 * w + DECAY_LORA + ICLR_LORA + GATE_LORA
    rwkv_pad = 3 * w + 3 * LANE
    nsa_cols = qw + 6 * kvw + 3 * NSA_HEADS
    nsa_pad = qw + 6 * kvw + LANE
    g_mix = row(mix_norm[l])
    w_rwkv = _pad_cols(wi[:, :rwkv_cols], rwkv_pad).astype(BF16)
    w_nsa = _pad_cols(wi[:, rwkv_cols:rwkv_cols + nsa_cols], nsa_pad).astype(BF16)
    p_gate = _norm_proj(x, g_mix, wi[:, rwkv_cols + nsa_cols:].astype(BF16), name="proj_gate")

    wwa = jnp.zeros((LANE, 2 * w), F32)
    wwa = wwa.at[:DECAY_LORA, :w].set(rwkv_w_up[l]).at[DECAY_LORA:, w:].set(rwkv_a_up[l])
    gup = jnp.pad(rwkv_g_up[l], ((0, 2 * LANE - GATE_LORA), (0, 0)))
    seg_w = _block_diag_ones(SEG_WIDTH, RWKV_HEAD_DIM)
    (rt, at, kt, bt, kh, bh, v, dc, bonus, gate) = _rwkv_prep(
        x, g_mix, w_rwkv, _pad_cols(row(rwkv_mix[l]), rwkv_pad), wwa, gup, row(rwkv_w0[l]), row(rwkv_a0[l]),
        row(rwkv_k_k[l]), row(rwkv_k_a[l]), row(rwkv_r_k[l]), seg_w, _chunk_lower_ones(prep_ts),
        seq=seq, ts=prep_ts)
    o_rwkv = _rwkv_chunk(rt, at, kt, bt, kh, bh, v, dc, batch=batch, seq=seq, prep_ts=prep_ts)

    cos_t, sin_a, sin_b = _rope_tables(np.arange(seq))
    two = lambda tab: jnp.concatenate([tab, tab], axis=1)
    qn = jnp.tile(row(nsa_q_norm[l]), (1, NSA_HEADS))
    kn = jnp.tile(nsa_k_norm[l], (1, NSA_KV_HEADS))
    q, ks, vs, kw, vw, gates, grp_k, grp_v = _nsa_prep(
        x, g_mix, w_nsa, two(cos_t), two(sin_a), two(sin_b), qn, kn, _block_diag_ones(SEG_WIDTH, dh),
        batch=batch, seq=seq)

    ngrp = seq // CMP_STRIDE
    cend = np.arange(ngrp) * CMP_STRIDE + CMP_BLOCK - 1
    cos_c, sa_c, sb_c = _rope_tables(cend)
    kc, vc = _nsa_compress(
        grp_k, grp_v,
        cmp_pos_k[l].reshape(1, -1), cmp_pos_v[l].reshape(1, -1),
        cmp_k_w1[l], cmp_k_w2[l], cmp_v_w1[l], cmp_v_w2[l], nsa_k_norm[l],
        cos_c, sb_c - sa_c, _rot_half_matrix())
    nsel = seq // SEL_BLOCK
    y_nsa = _nsa_attn(q, kc, vc, ks, vs, kw, vw, gates, _overlap_matrix_t(ngrp, nsel),
                      batch=batch, seq=seq)

    x = _merge(x, o_rwkv, bonus, gate, y_nsa, p_gate, row(rwkv_gn_w[l]), row(rwkv_gn_b[l]), seg_w,
               w_branch_rwkv[l].astype(BF16), w_branch_nsa[l].astype(BF16), w_out[l].astype(BF16))
    return _ffn(x, row(ffn2_norm[l]), ffn2_w_gate[l].astype(BF16), ffn2_w_up[l].astype(BF16),
                ffn2_w_down[l].astype(BF16))


def kernel(x, ffn1_norm, ffn1_w_gate, ffn1_w_up, ffn1_w_down, mix_norm, w_in, rwkv_mix, rwkv_w0, rwkv_w_up, rwkv_a0, rwkv_a_up, rwkv_g_up, rwkv_k_k, rwkv_k_a, rwkv_r_k, rwkv_gn_w, rwkv_gn_b, nsa_q_norm, nsa_k_norm, cmp_pos_k, cmp_pos_v, cmp_k_w1, cmp_k_w2, cmp_v_w1, cmp_v_w2, w_branch_rwkv, w_branch_nsa, w_out, ffn2_norm, ffn2_w_gate, ffn2_w_up, ffn2_w_down):
    batch, seq, d = x.shape
    params = (ffn1_norm, ffn1_w_gate, ffn1_w_up, ffn1_w_down, mix_norm, w_in, rwkv_mix, rwkv_w0,
              rwkv_w_up, rwkv_a0, rwkv_a_up, rwkv_g_up, rwkv_k_k, rwkv_k_a, rwkv_r_k, rwkv_gn_w,
              rwkv_gn_b, nsa_q_norm, nsa_k_norm, cmp_pos_k, cmp_pos_v, cmp_k_w1, cmp_k_w2, cmp_v_w1,
              cmp_v_w2, w_branch_rwkv, w_branch_nsa, w_out, ffn2_norm, ffn2_w_gate, ffn2_w_up,
              ffn2_w_down)
    y = x.reshape(batch * seq, d)
    for l in range(ffn1_norm.shape[0]):
        y = _layer(y, l, *params, batch=batch, seq=seq)
    return y.reshape(batch, seq, d)
```
